```python
import math
import jax, jax.numpy as jnp
from jax import lax
import numpy as np

D_MODEL = 2048
BATCH = 16
SEQ = 2048
DEPTH = 2

DIFF_HEADS = D_MODEL // 256
DIFF_HALF_DIM = 64
DIFF_VDIM = 2 * DIFF_HALF_DIM
DIFF_QK_WIDTH = DIFF_HEADS * 2 * DIFF_HALF_DIM
DIFF_WIDTH = DIFF_HEADS * DIFF_VDIM
DIL_HEADS = D_MODEL // 256
DIL_HEAD_DIM = 128
DIL_WIDTH = DIL_HEADS * DIL_HEAD_DIM
DIL_CONFIGS = ((128, 1), (512, 4), (2048, 16))
BLOCK = 128
MIX_WIDTH = DIFF_WIDTH + DIL_WIDTH
IN_COLS = 2 * DIFF_QK_WIDTH + DIFF_WIDTH + 3 * DIL_WIDTH
MEM_LEN = 256
MEM_HEADS = 4
MEM_HEAD_DIM = D_MODEL // MEM_HEADS
D_FF = 256 * (-(-(8 * D_MODEL // 3) // 256))
CONV_WIDTH = 3
ALPHA = (2 * DEPTH) ** 0.25
BETA = (8 * DEPTH) ** -0.25
LN_EPS = 1e-5
RMS_EPS = 1e-5

kernel_name = "hybrid_diffattn_dilated_deepnorm_block"


def alibi_slopes(n):
    return (2.0 ** (-8.0 * np.arange(1, n + 1) / n)).astype(np.float32)


def layer_norm(x, g, b):
    xf = x.astype(jnp.float32)
    mu = jnp.mean(xf, -1, keepdims=True)
    var = jnp.mean(jnp.square(xf - mu), -1, keepdims=True)
    return ((xf - mu) * lax.rsqrt(var + LN_EPS) * g.astype(jnp.float32) + b.astype(jnp.float32)).astype(x.dtype)


def rms_norm(x, g):
    xf = x.astype(jnp.float32)
    y = xf * lax.rsqrt(jnp.mean(jnp.square(xf), -1, keepdims=True) + RMS_EPS)
    return (y * g.astype(jnp.float32)).astype(x.dtype)


def diff_attention(q, k, v, lam, slopes):
    S = q.shape[1]
    dk = q.shape[-1]
    scale = dk ** -0.5
    outs = []
    for i in range(S // BLOCK):
        q0, kend = i * BLOCK, (i + 1) * BLOCK
        s = jnp.einsum('bqhcd,bkhcd->bhcqk', q[:, q0:kend], k[:, :kend]).astype(jnp.float32) * scale
        dist = (q0 + jnp.arange(BLOCK))[:, None] - jnp.arange(kend)[None, :]
        bias = -slopes[:, None, None, None] * dist.astype(jnp.float32)
        s = jnp.where(dist >= 0, s + bias, -jnp.inf)
        p = jax.nn.softmax(s, axis=-1)
        a = p[:, :, 0] - lam * p[:, :, 1]
        outs.append(jnp.einsum('bhqk,bkhd->bqhd', a.astype(v.dtype), v[:, :kend]))
    return jnp.concatenate(outs, axis=1)


def dilated_branch(q, k, v, slopes, window, dil):
    B, S, H, dh = q.shape
    L = S // dil
    nb = -(-L // BLOCK)
    Lp = nb * BLOCK
    n_back = window // dil
    scale = dh ** -0.5

    def to_sub(t):
        t = t.reshape(B, L, dil, H, dh).transpose(0, 2, 1, 3, 4)
        t = jnp.pad(t, ((0, 0), (0, 0), (0, Lp - L), (0, 0), (0, 0)))
        return t.reshape(B, dil, nb, BLOCK, H, dh)

    def with_prev(t):
        prev = jnp.pad(t, ((0, 0), (0, 0), (1, 0), (0, 0), (0, 0), (0, 0)))[:, :, :-1]
        return jnp.concatenate([prev, t], axis=3)

    qs = to_sub(q)
    kb, vb = with_prev(to_sub(k)), with_prev(to_sub(v))
    s = jnp.einsum('bcnqhd,bcnkhd->bchnqk', qs, kb).astype(jnp.float32) * scale
    qi = jnp.arange(BLOCK)[:, None]
    kj = jnp.arange(2 * BLOCK)[None, :]
    rel = qi + BLOCK - kj
    key_sub = jnp.arange(nb)[:, None, None] * BLOCK - BLOCK + kj[None]
    valid = (rel >= 0) & (rel <= n_back) & (key_sub >= 0)
    bias = -slopes[:, None, None, None] * (rel * dil).astype(jnp.float32)
    s = jnp.where(valid, s + bias, -jnp.inf)
    m = jnp.max(s, -1, keepdims=True)
    e = jnp.exp(s - m)
    l = jnp.sum(e, -1, keepdims=True)
    p = e / l
    lse = (m + jnp.log(l))[..., 0]
    o = jnp.einsum('bchnqk,bcnkhd->bcnqhd', p.astype(v.dtype), vb)
    o = o.reshape(B, dil, Lp, H, dh)[:, :, :L].transpose(0, 2, 1, 3, 4).reshape(B, S, H, dh)
    lse = lse.transpose(0, 1, 3, 4, 2).reshape(B, dil, Lp, H)[:, :, :L].transpose(0, 2, 1, 3).reshape(B, S, H)
    return o, lse


def dilated_attention(q, k, v, slopes):
    outs, lses = [], []
    for window, dil in DIL_CONFIGS:
        o, lse = dilated_branch(q, k, v, slopes, window, dil)
        outs.append(o)
        lses.append(lse)
    w = jax.nn.softmax(jnp.stack(lses, 0), axis=0)
    return jnp.einsum('cbsh,cbshd->bshd', w.astype(q.dtype), jnp.stack(outs, 0))


def hybrid_mixer(x, w_in, w_out, lq1, lk1, lq2, lk2, g_diff, g_dil, layer_idx):
    B, S, _ = x.shape
    proj = x @ w_in
    c1 = DIFF_QK_WIDTH
    c2 = 2 * DIFF_QK_WIDTH
    c3 = c2 + DIFF_WIDTH
    c4 = c3 + DIL_WIDTH
    c5 = c4 + DIL_WIDTH
    dq = proj[..., :c1].reshape(B, S, DIFF_HEADS, 2, DIFF_HALF_DIM)
    dk = proj[..., c1:c2].reshape(B, S, DIFF_HEADS, 2, DIFF_HALF_DIM)
    dv = proj[..., c2:c3].reshape(B, S, DIFF_HEADS, DIFF_VDIM)
    sq = proj[..., c3:c4].reshape(B, S, DIL_HEADS, DIL_HEAD_DIM)
    sk = proj[..., c4:c5].reshape(B, S, DIL_HEADS, DIL_HEAD_DIM)
    sv = proj[..., c5:].reshape(B, S, DIL_HEADS, DIL_HEAD_DIM)

    slopes = jnp.asarray(alibi_slopes(DIFF_HEADS + DIL_HEADS))
    slopes_diff, slopes_dil = slopes[0::2], slopes[1::2]

    lam_init = 0.8 - 0.6 * math.exp(-0.3 * layer_idx)
    f32 = jnp.float32
    lam = (jnp.exp(jnp.sum(lq1.astype(f32) * lk1.astype(f32)))
           - jnp.exp(jnp.sum(lq2.astype(f32) * lk2.astype(f32))) + lam_init)
    o_diff = diff_attention(dq, dk, dv, lam, slopes_diff)
    o_diff = rms_norm(o_diff, g_diff) * (1.0 - lam_init)
    o_dil = rms_norm(dilated_attention(sq, sk, sv, slopes_dil), g_dil)
    o = jnp.concatenate([o_diff.reshape(B, S, DIFF_WIDTH), o_dil.reshape(B, S, DIL_WIDTH)], axis=-1)
    return o @ w_out


def memory_attention(x, mem, w_q, w_kv, w_o):
    B, S, _ = x.shape
    M = mem.shape[1]
    q = (x @ w_q).reshape(B, S, MEM_HEADS, MEM_HEAD_DIM)
    kv = (mem @ w_kv).reshape(B, M, 2, MEM_HEADS, MEM_HEAD_DIM)
    s = jnp.einsum('bshd,bmhd->bhsm', q, kv[:, :, 0]).astype(jnp.float32) * MEM_HEAD_DIM ** -0.5
    p = jax.nn.softmax(s, axis=-1)
    o = jnp.einsum('bhsm,bmhd->bshd', p.astype(x.dtype), kv[:, :, 1]).reshape(B, S, D_MODEL)
    return o @ w_o


def conv_ffn(x, w_up, conv_w, conv_b, w_down):
    S = x.shape[1]
    h = x @ w_up
    hp = jnp.pad(h, ((0, 0), (CONV_WIDTH - 1, 0), (0, 0)))
    h = hp[:, 0:S] * conv_w[0] + hp[:, 1:S + 1] * conv_w[1] + hp[:, 2:S + 2] * conv_w[2] + conv_b
    gate, up = h[..., :D_FF], h[..., D_FF:]
    return (jax.nn.silu(gate) * up) @ w_down


def setup_inputs(seed: int = 0) -> dict:
    key = jax.random.key(seed)
    ks = jax.random.split(key, 32)
    f32 = jnp.float32

    def nrm(k, shape, std):
        return jax.random.normal(k, shape, f32) * std

    d_std = D_MODEL ** -0.5
    x = nrm(ks[0], (BATCH, SEQ, D_MODEL), 1.0)
    mem = nrm(ks[1], (BATCH, MEM_LEN, D_MODEL), 1.0)
    w_in = jnp.concatenate([
        nrm(ks[2], (DEPTH, D_MODEL, 2 * DIFF_QK_WIDTH), d_std),
        nrm(ks[3], (DEPTH, D_MODEL, DIFF_WIDTH), d_std * BETA),
        nrm(ks[4], (DEPTH, D_MODEL, 2 * DIL_WIDTH), d_std),
        nrm(ks[5], (DEPTH, D_MODEL, DIL_WIDTH), d_std * BETA),
    ], axis=-1)
    w_mix_out = nrm(ks[6], (DEPTH, MIX_WIDTH, D_MODEL), MIX_WIDTH ** -0.5 * BETA)
    lambda_q1 = nrm(ks[7], (DEPTH, DIFF_HALF_DIM), 0.1)
    lambda_k1 = nrm(ks[8], (DEPTH, DIFF_HALF_DIM), 0.1)
    lambda_q2 = nrm(ks[9], (DEPTH, DIFF_HALF_DIM), 0.1)
    lambda_k2 = nrm(ks[10], (DEPTH, DIFF_HALF_DIM), 0.1)
    g_diff = 1.0 + nrm(ks[11], (DEPTH, DIFF_VDIM), 0.02)
    g_dil = 1.0 + nrm(ks[12], (DEPTH, DIL_HEAD_DIM), 0.02)
    ln1_g = 1.0 + nrm(ks[13], (DEPTH, D_MODEL), 0.02)
    ln1_b = nrm(ks[14], (DEPTH, D_MODEL), 0.02)
    w_mem_q = nrm(ks[15], (DEPTH, D_MODEL, D_MODEL), d_std)
    w_mem_kv = jnp.concatenate([
        nrm(ks[16], (DEPTH, D_MODEL, D_MODEL), d_std),
        nrm(ks[17], (DEPTH, D_MODEL, D_MODEL), d_std * BETA),
    ], axis=-1)
    w_mem_o = nrm(ks[18], (DEPTH, D_MODEL, D_MODEL), d_std * BETA)
    ln2_g = 1.0 + nrm(ks[19], (DEPTH, D_MODEL), 0.02)
    ln2_b = nrm(ks[20], (DEPTH, D_MODEL), 0.02)
    w_up = nrm(ks[21], (DEPTH, D_MODEL, 2 * D_FF), d_std * BETA)
    conv_w = nrm(ks[22], (DEPTH, CONV_WIDTH, 2 * D_FF), CONV_WIDTH ** -0.5)
    conv_b = nrm(ks[23], (DEPTH, 2 * D_FF), 0.02)
    w_down = nrm(ks[24], (DEPTH, D_FF, D_MODEL), D_FF ** -0.5 * BETA)
    ln3_g = 1.0 + nrm(ks[25], (DEPTH, D_MODEL), 0.02)
    ln3_b = nrm(ks[26], (DEPTH, D_MODEL), 0.02)
    return {"x": x, "mem": mem, "w_in": w_in, "w_mix_out": w_mix_out,
            "lambda_q1": lambda_q1, "lambda_k1": lambda_k1, "lambda_q2": lambda_q2, "lambda_k2": lambda_k2,
            "g_diff": g_diff, "g_dil": g_dil, "ln1_g": ln1_g, "ln1_b": ln1_b,
            "w_mem_q": w_mem_q, "w_mem_kv": w_mem_kv, "w_mem_o": w_mem_o, "ln2_g": ln2_g, "ln2_b": ln2_b,
            "w_up": w_up, "conv_w": conv_w, "conv_b": conv_b, "w_down": w_down, "ln3_g": ln3_g, "ln3_b": ln3_b}


def reference(x, mem, w_in, w_mix_out, lambda_q1, lambda_k1, lambda_q2, lambda_k2, g_diff, g_dil,
              ln1_g, ln1_b, w_mem_q, w_mem_kv, w_mem_o, ln2_g, ln2_b,
              w_up, conv_w, conv_b, w_down, ln3_g, ln3_b):
    h = x
    for l in range(DEPTH):
        h = layer_norm(ALPHA * h + hybrid_mixer(h, w_in[l], w_mix_out[l], lambda_q1[l], lambda_k1[l],
                                                lambda_q2[l], lambda_k2[l], g_diff[l], g_dil[l], l),
                       ln1_g[l], ln1_b[l])
        h = layer_norm(ALPHA * h + memory_attention(h, mem, w_mem_q[l], w_mem_kv[l], w_mem_o[l]),
                       ln2_g[l], ln2_b[l])
        h = layer_norm(ALPHA * h + conv_ffn(h, w_up[l], conv_w[l], conv_b[l], w_down[l]),
                       ln3_g[l], ln3_b[l])
    return h
```

```python
import functools
import math

import numpy as np
import jax
import jax.numpy as jnp
from jax import lax
from jax.experimental import pallas as pl
from jax.experimental.pallas import tpu as pltpu

F32 = jnp.float32
BF16 = jnp.bfloat16

LN_EPS = 1e-5
RMS_EPS = 1e-5
BLOCK = 128
DIFF_HALF_DIM = 64
DIL_CONFIGS = ((128, 1), (512, 4), (2048, 16))
MEM_HEADS = 4
CONV_WIDTH = 3
NEG_INF = float("-inf")

V7X_VMEM_LIMIT_BYTES = 56 * 1024 * 1024


def _params(semantics):
    return pltpu.CompilerParams(dimension_semantics=semantics, vmem_limit_bytes=V7X_VMEM_LIMIT_BYTES)


def _resident(shape, index_map):
    return pl.BlockSpec(shape, index_map, pipeline_mode=pl.Buffered(1))


def _layer_norm(z, g, b):
    mu = jnp.mean(z, axis=-1, keepdims=True)
    zc = z - mu
    var = jnp.mean(zc * zc, axis=-1, keepdims=True)
    return zc * lax.rsqrt(var + LN_EPS) * g + b


def _rms_norm(o, g):
    return o * lax.rsqrt(jnp.mean(o * o, axis=-1, keepdims=True) + RMS_EPS) * g


def _dot(a, b):
    return jnp.dot(a, b, preferred_element_type=F32)


def _dot_nt(a, b):
    return lax.dot_general(a, b, (((1,), (1,)), ((), ())), preferred_element_type=F32)


def _proj_kernel(x_ref, w_ref, o_ref, xb_ref):
    @pl.when(pl.program_id(1) == 0)
    def _():
        xb_ref[...] = x_ref[...].astype(BF16)

    o_ref[...] = _dot(xb_ref[...], w_ref[...]).astype(o_ref.dtype)


def _project(x, w, tm, tn):
    m, k = x.shape
    n = w.shape[1]
    tm = min(tm, m)
    return pl.pallas_call(
        _proj_kernel,
        grid=(m // tm, n // tn),
        in_specs=[pl.BlockSpec((tm, k), lambda i, j: (i, 0)),
                  pl.BlockSpec((k, tn), lambda i, j: (0, j))],
        out_specs=pl.BlockSpec((tm, tn), lambda i, j: (i, j)),
        out_shape=jax.ShapeDtypeStruct((m, n), BF16),
        scratch_shapes=[pltpu.VMEM((tm, k), BF16)],
        compiler_params=_params(("arbitrary", "arbitrary")),
    )(x, w)


def _diff_attn_kernel(slope_ref, q_ref, k_ref, v_ref, lq1_ref, lk1_ref, lq2_ref, lk2_ref, g_ref, o_ref,
                      *, tq, lam_init):
    h = pl.program_id(1)
    qi = pl.program_id(2)
    slope = slope_ref[h]

    q = q_ref[0]
    lane = lax.broadcasted_iota(jnp.int32, q.shape, 1)
    zero = jnp.zeros_like(q)
    qq = jnp.concatenate([jnp.where(lane < DIFF_HALF_DIM, q, zero),
                          jnp.where(lane >= DIFF_HALF_DIM, q, zero)], axis=0)
    qq = qq * jnp.asarray(DIFF_HALF_DIM ** -0.5, BF16)

    col = lax.broadcasted_iota(jnp.int32, (1, tq), 1).astype(F32)

    def scores(j):
        k = k_ref[0, pl.ds(pl.multiple_of(j * tq, tq), tq), :]
        s = _dot_nt(qq, k)
        return s + slope * (col + ((j - qi) * tq).astype(F32))

    def update(j, s, carry):
        m, l, acc = carry
        v = v_ref[0, pl.ds(pl.multiple_of(j * tq, tq), tq), :]
        m_new = jnp.maximum(m, jnp.max(s, axis=-1, keepdims=True))
        a = jnp.exp(m - m_new)
        p = jnp.exp(s - m_new)
        l = a * l + jnp.sum(p, axis=-1, keepdims=True)
        acc = a * acc + _dot(p.astype(BF16), v)
        return m_new, l, acc

    init = (jnp.full((2 * tq, 1), -1e30, F32), jnp.zeros((2 * tq, 1), F32), jnp.zeros((2 * tq, BLOCK), F32))
    carry = lax.fori_loop(0, qi, lambda j, c: update(j, scores(j), c), init)

    row = lax.broadcasted_iota(jnp.int32, (2 * tq, tq), 0)
    row = jnp.where(row >= tq, row - tq, row)
    kcol = lax.broadcasted_iota(jnp.int32, (2 * tq, tq), 1)
    s_diag = jnp.where(row >= kcol, scores(qi), NEG_INF)
    _, l, acc = update(qi, s_diag, carry)

    o = acc / l
    lam = (jnp.exp(jnp.sum(lq1_ref[...] * lk1_ref[...], axis=-1, keepdims=True))
           - jnp.exp(jnp.sum(lq2_ref[...] * lk2_ref[...], axis=-1, keepdims=True)) + lam_init)
    o = o[:tq] - lam * o[tq:]
    o_ref[0] = (_rms_norm(o, g_ref[...]) * (1.0 - lam_init)).astype(o_ref.dtype)


def _diff_attention(proj, slopes, lq1, lk1, lq2, lk2, g, lam_init, heads, tq):
    b, s, _ = proj.shape
    q_blk, k_blk, v_blk = 0, heads, 2 * heads
    vec = lambda n: pl.BlockSpec((1, n), lambda bi, hi, qi: (0, 0))
    return pl.pallas_call(
        functools.partial(_diff_attn_kernel, tq=tq, lam_init=lam_init),
        grid=(b, heads, s // tq),
        in_specs=[pl.BlockSpec(memory_space=pltpu.SMEM),
                  pl.BlockSpec((1, tq, BLOCK), lambda bi, hi, qi: (bi, qi, q_blk + hi)),
                  pl.BlockSpec((1, s, BLOCK), lambda bi, hi, qi: (bi, 0, k_blk + hi)),
                  pl.BlockSpec((1, s, BLOCK), lambda bi, hi, qi: (bi, 0, v_blk + hi)),
                  vec(DIFF_HALF_DIM), vec(DIFF_HALF_DIM), vec(DIFF_HALF_DIM), vec(DIFF_HALF_DIM), vec(BLOCK)],
        out_specs=pl.BlockSpec((1, tq, BLOCK), lambda bi, hi, qi: (bi, qi, hi)),
        out_shape=jax.ShapeDtypeStruct((b, s, heads * BLOCK), BF16),
        compiler_params=_params(("arbitrary", "arbitrary", "arbitrary")),
    )(slopes, proj, proj, proj, lq1, lk1, lq2, lk2, g)


def _dilated_attn_kernel(slope_ref, q_ref, k_ref, v_ref, g_ref, o_ref, qf, kf, vf, m_ref, l_ref, acc_ref,
                         *, seq, tc):
    h = pl.program_id(1)
    slope = slope_ref[h]
    qf[...] = q_ref[0].astype(F32)
    kf[...] = k_ref[0].astype(F32)
    vf[...] = v_ref[0].astype(F32)
    scale = BLOCK ** -0.5

    qi = lax.broadcasted_iota(jnp.int32, (BLOCK, 2 * BLOCK), 0)
    kj = lax.broadcasted_iota(jnp.int32, (BLOCK, 2 * BLOCK), 1)
    rel = qi + BLOCK - kj

    def rows(start, size, dil):
        return pl.ds(start, size) if dil == 1 else pl.ds(start, size, stride=dil)

    def tile(branch, dil, q_start, k_start, bias):
        nkeys = bias.shape[1]
        q_rows = rows(q_start, BLOCK, dil)
        k_rows = rows(k_start, nkeys, dil)
        q = qf[q_rows, :].astype(BF16)
        k = kf[k_rows, :].astype(BF16)
        v = vf[k_rows, :].astype(BF16)
        s = _dot_nt(q, k) * scale + bias
        m = jnp.max(s, axis=-1, keepdims=True)
        e = jnp.exp(s - m)
        m_ref[branch, q_rows, :] = m
        l_ref[branch, q_rows, :] = jnp.sum(e, axis=-1, keepdims=True)
        acc_ref[branch, q_rows, :] = _dot(e.astype(BF16), v)

    for branch, (window, dil) in enumerate(DIL_CONFIGS):
        n_back = window // dil
        nb = seq // dil // BLOCK
        step = dil * BLOCK
        valid = (rel >= 0) & (rel <= n_back)
        bias = jnp.where(valid, (-slope * dil) * rel.astype(F32), NEG_INF)
        bias_first = bias[:, BLOCK:]

        def class_body(c, carry, branch=branch, dil=dil, nb=nb, step=step, bias=bias, bias_first=bias_first):
            tile(branch, dil, c, c, bias_first)
            if nb > 1:
                def block_body(n, carry2):
                    tile(branch, dil, c + n * step, c + (n - 1) * step, bias)
                    return carry2
                lax.fori_loop(1, nb, block_body, 0)
            return carry

        if dil == 1:
            class_body(0, 0)
        else:
            lax.fori_loop(0, dil, class_body, 0)

    g = g_ref[...]
    for t in range(seq // tc):
        r = pl.ds(t * tc, tc)
        ms = [m_ref[i, r, :] for i in range(len(DIL_CONFIGS))]
        m_all = functools.reduce(jnp.maximum, ms)
        ws = [jnp.exp(m - m_all) for m in ms]
        num = sum(w * acc_ref[i, r, :] for i, w in enumerate(ws))
        den = sum(w * l_ref[i, r, :] for i, w in enumerate(ws))
        o_ref[0, r, :] = _rms_norm(num / den, g).astype(o_ref.dtype)


def _dilated_attention(proj, slopes, g, heads, first_blk):
    b, s, _ = proj.shape
    nbr = len(DIL_CONFIGS)
    spec = lambda blk: pl.BlockSpec((1, s, BLOCK), lambda bi, hi: (bi, 0, blk + hi))
    return pl.pallas_call(
        functools.partial(_dilated_attn_kernel, seq=s, tc=256),
        grid=(b, heads),
        in_specs=[pl.BlockSpec(memory_space=pltpu.SMEM),
                  spec(first_blk), spec(first_blk + heads), spec(first_blk + 2 * heads),
                  pl.BlockSpec((1, BLOCK), lambda bi, hi: (0, 0))],
        out_specs=pl.BlockSpec((1, s, BLOCK), lambda bi, hi: (bi, 0, hi)),
        out_shape=jax.ShapeDtypeStruct((b, s, heads * BLOCK), BF16),
        scratch_shapes=[pltpu.VMEM((s, BLOCK), F32), pltpu.VMEM((s, BLOCK), F32), pltpu.VMEM((s, BLOCK), F32),
                        pltpu.VMEM((nbr, s, 1), F32), pltpu.VMEM((nbr, s, 1), F32),
                        pltpu.VMEM((nbr, s, BLOCK), F32)],
        compiler_params=_params(("arbitrary", "arbitrary")),
    )(slopes, proj, proj, proj, g)


def _outproj_ln_kernel(*refs, n_in, alpha):
    o_refs, w_refs = refs[:n_in], refs[n_in:2 * n_in]
    h_ref, g_ref, b_ref, out_ref = refs[2 * n_in:]
    y = _dot(o_refs[0][...], w_refs[0][...])
    for o_ref, w_ref in zip(o_refs[1:], w_refs[1:]):
        y = y + _dot(o_ref[...], w_ref[...])
    out_ref[...] = _layer_norm(alpha * h_ref[...] + y, g_ref[...], b_ref[...])


def _outproj_ln(os, w, h, g, b, alpha, tm):
    m, d = h.shape
    n_in = len(os)
    kw = w.shape[0] // n_in
    return pl.pallas_call(
        functools.partial(_outproj_ln_kernel, n_in=n_in, alpha=alpha),
        grid=(m // tm,),
        in_specs=([pl.BlockSpec((tm, kw), lambda i: (i, 0)) for _ in os]
                  + [_resident((kw, d), functools.partial(lambda i, r: (r, 0), r=r)) for r in range(n_in)]
                  + [pl.BlockSpec((tm, d), lambda i: (i, 0)),
                     pl.BlockSpec((1, d), lambda i: (0, 0)), pl.BlockSpec((1, d), lambda i: (0, 0))]),
        out_specs=pl.BlockSpec((tm, d), lambda i: (i, 0)),
        out_shape=jax.ShapeDtypeStruct((m, d), F32),
        compiler_params=_params(("arbitrary",)),
    )(*os, *([w] * n_in), h, g, b)


def _mem_attn_kernel(h_ref, wq_ref, kv_ref, o_ref, *, heads):
    d = h_ref.shape[1]
    hd = d // heads
    q = _dot(h_ref[...].astype(BF16), wq_ref[...]).astype(BF16)
    scale = hd ** -0.5
    for i in range(heads):
        k = kv_ref[0, :, i * hd:(i + 1) * hd]
        v = kv_ref[0, :, d + i * hd:d + (i + 1) * hd]
        s = _dot_nt(q[:, i * hd:(i + 1) * hd], k) * scale
        e = jnp.exp(s - jnp.max(s, axis=-1, keepdims=True))
        p = e * (1.0 / jnp.sum(e, axis=-1, keepdims=True))
        o_ref[:, i * hd:(i + 1) * hd] = _dot(p.astype(BF16), v).astype(o_ref.dtype)


def _mem_attention(h, wq, kv, seq, tm):
    m, d = h.shape
    mem_len = kv.shape[1]
    per_seq = seq // tm
    return pl.pallas_call(
        functools.partial(_mem_attn_kernel, heads=MEM_HEADS),
        grid=(m // tm,),
        in_specs=[pl.BlockSpec((tm, d), lambda i: (i, 0)),
                  _resident((d, d), lambda i: (0, 0)),
                  pl.BlockSpec((1, mem_len, 2 * d), lambda i: (i // per_seq, 0, 0))],
        out_specs=pl.BlockSpec((tm, d), lambda i: (i, 0)),
        out_shape=jax.ShapeDtypeStruct((m, d), BF16),
        compiler_params=_params(("arbitrary",)),
    )(h, wq, kv)


HALO = 8


def _ffn_kernel(x_ref, wg_ref, wu_ref, cwg_ref, cwu_ref, cbg_ref, cbu_ref, wd_ref, g_ref, b_ref, o_ref,
                xb_ref, hg_ref, hu_ref, carry_g_ref, carry_u_ref, acc_ref, *, tm, tiles_per_seq, alpha):
    i = pl.program_id(0)
    f = pl.program_id(1)
    nf = pl.num_programs(1)

    @pl.when(f == 0)
    def _():
        xb_ref[...] = x_ref[...].astype(BF16)

    xb = xb_ref[...]
    seq_start = (i % tiles_per_seq) == 0

    def conv(w_ref, hs_ref, carry_ref, cw_ref, cb_ref):
        hs_ref[HALO:, :] = _dot(xb, w_ref[...])

        @pl.when(seq_start)
        def _():
            hs_ref[:HALO, :] = jnp.zeros((HALO, hs_ref.shape[1]), F32)

        @pl.when(jnp.logical_not(seq_start))
        def _():
            hs_ref[:HALO, :] = carry_ref[f]

        carry_ref[f] = hs_ref[tm:, :]
        out = cb_ref[...]
        for tap in range(CONV_WIDTH):
            off = HALO - (CONV_WIDTH - 1) + tap
            out = out + hs_ref[off:off + tm, :] * cw_ref[tap:tap + 1, :]
        return out

    gate = conv(wg_ref, hg_ref, carry_g_ref, cwg_ref, cbg_ref)
    up = conv(wu_ref, hu_ref, carry_u_ref, cwu_ref, cbu_ref)
    act = (gate * (1.0 / (1.0 + jnp.exp(-gate))) * up).astype(BF16)
    contrib = _dot(act, wd_ref[...])

    @pl.when(f == 0)
    def _():
        acc_ref[...] = contrib

    @pl.when(f > 0)
    def _():
        acc_ref[...] += contrib

    @pl.when(f == nf - 1)
    def _():
        o_ref[...] = _layer_norm(alpha * x_ref[...] + acc_ref[...], g_ref[...], b_ref[...])


def _conv_ffn_ln(x, w_up, conv_w, conv_b, w_down, g, b, alpha, seq, tm, tf):
    m, d = x.shape
    d_ff = w_down.shape[0]
    nf = d_ff // tf
    row = lambda i, f: (i, 0)
    fixed = lambda i, f: (0, 0)
    gate_col = lambda i, f: (0, f)
    up_col = lambda i, f: (0, nf + f)
    return pl.pallas_call(
        functools.partial(_ffn_kernel, tm=tm, tiles_per_seq=seq // tm, alpha=alpha),
        grid=(m // tm, nf),
        in_specs=[pl.BlockSpec((tm, d), row),
                  pl.BlockSpec((d, tf), gate_col), pl.BlockSpec((d, tf), up_col),
                  pl.BlockSpec((CONV_WIDTH, tf), gate_col), pl.BlockSpec((CONV_WIDTH, tf), up_col),
                  pl.BlockSpec((1, tf), gate_col), pl.BlockSpec((1, tf), up_col),
                  pl.BlockSpec((tf, d), lambda i, f: (f, 0)),
                  pl.BlockSpec((1, d), fixed), pl.BlockSpec((1, d), fixed)],
        out_specs=pl.BlockSpec((tm, d), row),
        out_shape=jax.ShapeDtypeStruct((m, d), F32),
        scratch_shapes=[pltpu.VMEM((tm, d), BF16),
                        pltpu.VMEM((tm + HALO, tf), F32), pltpu.VMEM((tm + HALO, tf), F32),
                        pltpu.VMEM((nf, HALO, tf), F32), pltpu.VMEM((nf, HALO, tf), F32),
                        pltpu.VMEM((tm, d), F32)],
        compiler_params=_params(("arbitrary", "arbitrary")),
    )(x, w_up, w_up, conv_w, conv_w, conv_b, conv_b, w_down, g, b)


def _alibi_slopes(n):
    return (2.0 ** (-8.0 * np.arange(1, n + 1) / n)).astype(np.float32)


def kernel(x, mem, w_in, w_mix_out, lambda_q1, lambda_k1, lambda_q2, lambda_k2, g_diff, g_dil, ln1_g, ln1_b,
           w_mem_q, w_mem_kv, w_mem_o, ln2_g, ln2_b, w_up, conv_w, conv_b, w_down, ln3_g, ln3_b):
    batch, seq, d = x.shape
    depth = w_in.shape[0]
    mem_len = mem.shape[1]
    diff_heads = dil_heads = d // 256
    alpha = (2 * depth) ** 0.25
    slopes = _alibi_slopes(diff_heads + dil_heads)
    slopes_diff, slopes_dil = jnp.asarray(slopes[0::2]), jnp.asarray(slopes[1::2])
    row = lambda a: a.reshape(1, -1)

    h = x.reshape(batch * seq, d)
    mem2 = mem.reshape(batch * mem_len, d)
    for l in range(depth):
        lam_init = 0.8 - 0.6 * math.exp(-0.3 * l)
        proj = _project(h, w_in[l].astype(BF16), tm=1024, tn=1024).reshape(batch, seq, -1)
        o_diff = _diff_attention(proj, slopes_diff, row(lambda_q1[l]), row(lambda_k1[l]), row(lambda_q2[l]),
                                 row(lambda_k2[l]), row(g_diff[l]), lam_init, diff_heads, tq=256)
        o_dil = _dilated_attention(proj, slopes_dil, row(g_dil[l]), dil_heads, first_blk=3 * diff_heads)
        h = _outproj_ln([o_diff.reshape(batch * seq, -1), o_dil.reshape(batch * seq, -1)],
                        w_mix_out[l].astype(BF16), h, row(ln1_g[l]), row(ln1_b[l]), alpha, tm=512)

        kv = _project(mem2, w_mem_kv[l].astype(BF16), tm=1024, tn=1024).reshape(batch, mem_len, 2 * d)
        o_mem = _mem_attention(h, w_mem_q[l].astype(BF16), kv, seq, tm=512)
        h = _outproj_ln([o_mem], w_mem_o[l].astype(BF16), h, row(ln2_g[l]), row(ln2_b[l]), alpha, tm=512)

        h = _conv_ffn_ln(h, w_up[l].astype(BF16), conv_w[l], row(conv_b[l]), w_down[l].astype(BF16),
                         row(ln3_g[l]), row(ln3_b[l]), alpha, seq, tm=512, tf=512)
    return h.reshape(batch, seq, d)
```

```python
import functools
import math

import numpy as np
import jax
import jax.numpy as jnp
from jax import lax
from jax.experimental import pallas as pl
from jax.experimental.pallas import tpu as pltpu

F32 = jnp.float32
BF16 = jnp.bfloat16

LN_EPS = 1e-5
RMS_EPS = 1e-5
BLOCK = 128
DIFF_HALF_DIM = 64
DIL_CONFIGS = ((128, 1), (512, 4), (2048, 16))
TILES_PER_BODY = 4
MEM_HEADS = 4
CONV_WIDTH = 3
NEG_INF = float("-inf")

V7X_VMEM_LIMIT_BYTES = 56 * 1024 * 1024


def _params(semantics):
    return pltpu.CompilerParams(dimension_semantics=semantics, vmem_limit_bytes=V7X_VMEM_LIMIT_BYTES)


def _resident(shape, index_map):
    return pl.BlockSpec(shape, index_map, pipeline_mode=pl.Buffered(1))


def _layer_norm(z, g, b):
    mu = jnp.mean(z, axis=-1, keepdims=True)
    zc = z - mu
    var = jnp.mean(zc * zc, axis=-1, keepdims=True)
    return zc * lax.rsqrt(var + LN_EPS) * g + b


def _rms_norm(o, g):
    return o * lax.rsqrt(jnp.mean(o * o, axis=-1, keepdims=True) + RMS_EPS) * g


def _dot(a, b):
    return jnp.dot(a, b, preferred_element_type=F32)


def _dot_nt(a, b):
    return lax.dot_general(a, b, (((1,), (1,)), ((), ())), preferred_element_type=F32)


def _proj_kernel(x_ref, w_ref, o_ref, xb_ref):
    @pl.when(pl.program_id(1) == 0)
    def _():
        xb_ref[...] = x_ref[...].astype(BF16)

    o_ref[...] = _dot(xb_ref[...], w_ref[...]).astype(o_ref.dtype)


def _project(x, w, tm, tn):
    m, k = x.shape
    n = w.shape[1]
    tm = min(tm, m)
    return pl.pallas_call(
        _proj_kernel,
        grid=(m // tm, n // tn),
        in_specs=[pl.BlockSpec((tm, k), lambda i, j: (i, 0)),
                  pl.BlockSpec((k, tn), lambda i, j: (0, j))],
        out_specs=pl.BlockSpec((tm, tn), lambda i, j: (i, j)),
        out_shape=jax.ShapeDtypeStruct((m, n), BF16),
        scratch_shapes=[pltpu.VMEM((tm, k), BF16)],
        compiler_params=_params(("arbitrary", "arbitrary")),
    )(x, w)


def _diff_attn_kernel(slope_ref, q_ref, k_ref, v_ref, lq1_ref, lk1_ref, lq2_ref, lk2_ref, g_ref, o_ref,
                      vt_ref, *, t, lam_init):
    slope = slope_ref[pl.program_id(1)]
    n_tiles = vt_ref.shape[0]
    for j in range(n_tiles):
        vt_ref[j] = v_ref[0, j * t:(j + 1) * t, :].astype(F32).T.astype(BF16)

    lam = (jnp.exp(jnp.sum(lq1_ref[...] * lk1_ref[...], axis=-1, keepdims=True))
           - jnp.exp(jnp.sum(lq2_ref[...] * lk2_ref[...], axis=-1, keepdims=True)) + lam_init)
    lane = lax.broadcasted_iota(jnp.int32, (t, BLOCK), 1)
    key_bias = slope * lax.broadcasted_iota(jnp.int32, (t, 1), 0).astype(F32)
    key = lax.broadcasted_iota(jnp.int32, (t, 2 * t), 0)
    query = lax.broadcasted_iota(jnp.int32, (t, 2 * t), 1)
    causal = key <= jnp.where(query >= t, query - t, query)

    for qi in range(n_tiles):
        q = q_ref[0, qi * t:(qi + 1) * t, :]
        zero = jnp.zeros_like(q)
        qq = jnp.concatenate([jnp.where(lane < DIFF_HALF_DIM, q, zero),
                              jnp.where(lane >= DIFF_HALF_DIM, q, zero)], axis=0)
        qq = qq * jnp.asarray(DIFF_HALF_DIM ** -0.5, BF16)
        m = l = acc = None
        for j in range(qi + 1):
            s = _dot_nt(k_ref[0, j * t:(j + 1) * t, :], qq) + key_bias
            if j == qi:
                s = jnp.where(causal, s, NEG_INF)
            shift = slope * float((j - qi) * t)
            m_tile = jnp.max(s, axis=0, keepdims=True) + shift
            m_new = m_tile if j == 0 else jnp.maximum(m, m_tile)
            p = jnp.exp(s - (m_new - shift))
            l_tile = jnp.sum(p, axis=0, keepdims=True)
            acc_tile = _dot(vt_ref[j], p.astype(BF16))
            if j == 0:
                l, acc = l_tile, acc_tile
            else:
                a = jnp.exp(m - m_new)
                l, acc = a * l + l_tile, a * acc + acc_tile
            m = m_new
        o = acc / l
        o = (o[:, :t] - lam * o[:, t:]).T
        o_ref[0, qi * t:(qi + 1) * t, :] = (_rms_norm(o, g_ref[...]) * (1.0 - lam_init)).astype(o_ref.dtype)


def _diff_attention(proj, slopes, lq1, lk1, lq2, lk2, g, lam_init, heads, t):
    b, s, _ = proj.shape
    q_blk, k_blk, v_blk = 0, heads, 2 * heads
    vec = lambda n: pl.BlockSpec((1, n), lambda bi, hi: (0, 0))
    slab = lambda blk: pl.BlockSpec((1, s, BLOCK), lambda bi, hi: (bi, 0, blk + hi))
    return pl.pallas_call(
        functools.partial(_diff_attn_kernel, t=t, lam_init=lam_init),
        grid=(b, heads),
        in_specs=[pl.BlockSpec(memory_space=pltpu.SMEM), slab(q_blk), slab(k_blk), slab(v_blk),
                  vec(DIFF_HALF_DIM), vec(DIFF_HALF_DIM), vec(DIFF_HALF_DIM), vec(DIFF_HALF_DIM), vec(BLOCK)],
        out_specs=slab(0),
        out_shape=jax.ShapeDtypeStruct((b, s, heads * BLOCK), BF16),
        scratch_shapes=[pltpu.VMEM((s // t, BLOCK, t), BF16)],
        compiler_params=_params(("arbitrary", "arbitrary")),
    )(slopes, proj, proj, proj, lq1, lk1, lq2, lk2, g)


def _dilated_attn_kernel(slope_ref, q_ref, k_ref, v_ref, g_ref, o_ref, qf, kf, vf, m_ref, l_ref, acc_ref,
                         *, seq, tc):
    h = pl.program_id(1)
    slope = slope_ref[h]
    qf[...] = q_ref[0].astype(F32)
    kf[...] = k_ref[0].astype(F32)
    vf[...] = v_ref[0].astype(F32)
    scale = BLOCK ** -0.5

    qi = lax.broadcasted_iota(jnp.int32, (BLOCK, 2 * BLOCK), 0)
    kj = lax.broadcasted_iota(jnp.int32, (BLOCK, 2 * BLOCK), 1)
    rel = qi + BLOCK - kj

    def rows(start, size, dil):
        return pl.ds(start, size) if dil == 1 else pl.ds(start, size, stride=dil)

    def tile(branch, dil, q_start, k_start, bias):
        nkeys = bias.shape[1]
        q_rows = rows(q_start, BLOCK, dil)
        k_rows = rows(k_start, nkeys, dil)
        q = qf[q_rows, :].astype(BF16)
        k = kf[k_rows, :].astype(BF16)
        v = vf[k_rows, :].astype(BF16)
        s = _dot_nt(q, k) * scale + bias
        m = jnp.max(s, axis=-1, keepdims=True)
        e = jnp.exp(s - m)
        m_ref[branch, q_rows, :] = m
        l_ref[branch, q_rows, :] = jnp.sum(e, axis=-1, keepdims=True)
        acc_ref[branch, q_rows, :] = _dot(e.astype(BF16), v)

    for branch, (window, dil) in enumerate(DIL_CONFIGS):
        n_back = window // dil
        nb = seq // dil // BLOCK
        step = dil * BLOCK
        valid = (rel >= 0) & (rel <= n_back)
        bias = jnp.where(valid, (-slope * dil) * rel.astype(F32), NEG_INF)
        bias_first = bias[:, BLOCK:]

        def first_tile(c, branch=branch, dil=dil, bias_first=bias_first):
            tile(branch, dil, c, c, bias_first)

        def later_tile(c, n, branch=branch, dil=dil, step=step, bias=bias):
            tile(branch, dil, c + n * step, c + (n - 1) * step, bias)

        if nb == 1:
            def class_group(i, carry, first_tile=first_tile):
                for u in range(TILES_PER_BODY):
                    first_tile(i * TILES_PER_BODY + u)
                return carry
            lax.fori_loop(0, dil // TILES_PER_BODY, class_group, 0)
        elif nb <= TILES_PER_BODY:
            def one_class(c, carry, first_tile=first_tile, later_tile=later_tile, nb=nb):
                first_tile(c)
                for n in range(1, nb):
                    later_tile(c, n)
                return carry
            lax.fori_loop(0, dil, one_class, 0)
        else:
            group = 3
            assert dil == 1 and (nb - 1) % group == 0
            first_tile(0)
            def block_group(i, carry, later_tile=later_tile, group=group):
                for u in range(group):
                    later_tile(0, 1 + i * group + u)
                return carry
            lax.fori_loop(0, (nb - 1) // group, block_group, 0)

    g = g_ref[...]
    for t in range(seq // tc):
        r = pl.ds(t * tc, tc)
        ms = [m_ref[i, r, :] for i in range(len(DIL_CONFIGS))]
        m_all = functools.reduce(jnp.maximum, ms)
        ws = [jnp.exp(m - m_all) for m in ms]
        num = sum(w * acc_ref[i, r, :] for i, w in enumerate(ws))
        den = sum(w * l_ref[i, r, :] for i, w in enumerate(ws))
        o_ref[0, r, :] = _rms_norm(num / den, g).astype(o_ref.dtype)


def _dilated_attention(proj, slopes, g, heads, first_blk):
    b, s, _ = proj.shape
    nbr = len(DIL_CONFIGS)
    spec = lambda blk: pl.BlockSpec((1, s, BLOCK), lambda bi, hi: (bi, 0, blk + hi))
    return pl.pallas_call(
        functools.partial(_dilated_attn_kernel, seq=s, tc=256),
        grid=(b, heads),
        in_specs=[pl.BlockSpec(memory_space=pltpu.SMEM),
                  spec(first_blk), spec(first_blk + heads), spec(first_blk + 2 * heads),
                  pl.BlockSpec((1, BLOCK), lambda bi, hi: (0, 0))],
        out_specs=pl.BlockSpec((1, s, BLOCK), lambda bi, hi: (bi, 0, hi)),
        out_shape=jax.ShapeDtypeStruct((b, s, heads * BLOCK), BF16),
        scratch_shapes=[pltpu.VMEM((s, BLOCK), F32), pltpu.VMEM((s, BLOCK), F32), pltpu.VMEM((s, BLOCK), F32),
                        pltpu.VMEM((nbr, s, 1), F32), pltpu.VMEM((nbr, s, 1), F32),
                        pltpu.VMEM((nbr, s, BLOCK), F32)],
        compiler_params=_params(("arbitrary", "arbitrary")),
    )(slopes, proj, proj, proj, g)


def _outproj_ln_kernel(*refs, n_in, alpha):
    o_refs, w_refs = refs[:n_in], refs[n_in:2 * n_in]
    h_ref, g_ref, b_ref, out_ref = refs[2 * n_in:]
    y = _dot(o_refs[0][...], w_refs[0][...])
    for o_ref, w_ref in zip(o_refs[1:], w_refs[1:]):
        y = y + _dot(o_ref[...], w_ref[...])
    out_ref[...] = _layer_norm(alpha * h_ref[...] + y, g_ref[...], b_ref[...])


def _outproj_ln(os, w, h, g, b, alpha, tm):
    m, d = h.shape
    n_in = len(os)
    kw = w.shape[0] // n_in
    return pl.pallas_call(
        functools.partial(_outproj_ln_kernel, n_in=n_in, alpha=alpha),
        grid=(m // tm,),
        in_specs=([pl.BlockSpec((tm, kw), lambda i: (i, 0)) for _ in os]
                  + [_resident((kw, d), functools.partial(lambda i, r: (r, 0), r=r)) for r in range(n_in)]
                  + [pl.BlockSpec((tm, d), lambda i: (i, 0)),
                     pl.BlockSpec((1, d), lambda i: (0, 0)), pl.BlockSpec((1, d), lambda i: (0, 0))]),
        out_specs=pl.BlockSpec((tm, d), lambda i: (i, 0)),
        out_shape=jax.ShapeDtypeStruct((m, d), F32),
        compiler_params=_params(("arbitrary",)),
    )(*os, *([w] * n_in), h, g, b)


def _mem_attn_kernel(h_ref, wq_ref, kv_ref, o_ref, *, heads):
    d = h_ref.shape[1]
    hd = d // heads
    q = _dot(h_ref[...].astype(BF16), wq_ref[...]).astype(BF16)
    scale = hd ** -0.5
    for i in range(heads):
        k = kv_ref[0, :, i * hd:(i + 1) * hd]
        v = kv_ref[0, :, d + i * hd:d + (i + 1) * hd]
        s = _dot_nt(q[:, i * hd:(i + 1) * hd], k) * scale
        e = jnp.exp(s - jnp.max(s, axis=-1, keepdims=True))
        p = e * (1.0 / jnp.sum(e, axis=-1, keepdims=True))
        o_ref[:, i * hd:(i + 1) * hd] = _dot(p.astype(BF16), v).astype(o_ref.dtype)


def _mem_attention(h, wq, kv, seq, tm):
    m, d = h.shape
    mem_len = kv.shape[1]
    per_seq = seq // tm
    return pl.pallas_call(
        functools.partial(_mem_attn_kernel, heads=MEM_HEADS),
        grid=(m // tm,),
        in_specs=[pl.BlockSpec((tm, d), lambda i: (i, 0)),
                  _resident((d, d), lambda i: (0, 0)),
                  pl.BlockSpec((1, mem_len, 2 * d), lambda i: (i // per_seq, 0, 0))],
        out_specs=pl.BlockSpec((tm, d), lambda i: (i, 0)),
        out_shape=jax.ShapeDtypeStruct((m, d), BF16),
        compiler_params=_params(("arbitrary",)),
    )(h, wq, kv)


HALO = 8


def _ffn_kernel(x_ref, wg_ref, wu_ref, cwg_ref, cwu_ref, cbg_ref, cbu_ref, wd_ref, g_ref, b_ref, o_ref,
                xb_ref, hg0_ref, hu0_ref, hg1_ref, hu1_ref, carry_g_ref, carry_u_ref, acc_ref,
                *, tm, tiles_per_seq, alpha):
    i = pl.program_id(0)
    f = pl.program_id(1)
    nf = pl.num_programs(1) - 1
    seq_start = (i % tiles_per_seq) == 0
    h_refs = ((hg0_ref, hu0_ref), (hg1_ref, hu1_ref))

    def up_project(slot):
        xb = xb_ref[...]
        for w_ref, hs_ref, carry_ref in zip((wg_ref, wu_ref), h_refs[slot], (carry_g_ref, carry_u_ref)):
            hs_ref[HALO:, :] = _dot(xb, w_ref[...])
            hs_ref[:HALO, :] = jnp.where(seq_start, 0.0, carry_ref[f])
            carry_ref[f] = hs_ref[tm:, :]

    def conv(hs_ref, cw_ref, cb_ref):
        out = cb_ref[...]
        for tap in range(CONV_WIDTH):
            off = HALO - (CONV_WIDTH - 1) + tap
            out = out + hs_ref[off:off + tm, :] * cw_ref[tap:tap + 1, :]
        return out

    def down_project(slot):
        gate = conv(h_refs[slot][0], cwg_ref, cbg_ref)
        up = conv(h_refs[slot][1], cwu_ref, cbu_ref)
        act = (gate * (1.0 / (1.0 + jnp.exp(-gate))) * up).astype(BF16)
        acc_ref[...] += _dot(act, wd_ref[...])

    @pl.when(f == 0)
    def _():
        @pl.when(i == 0)
        def _():
            carry_g_ref[...] = jnp.zeros(carry_g_ref.shape, F32)
            carry_u_ref[...] = jnp.zeros(carry_u_ref.shape, F32)

        xb_ref[...] = x_ref[...].astype(BF16)
        acc_ref[...] = jnp.zeros(acc_ref.shape, F32)
        up_project(0)

    for parity in range(2):
        @pl.when((f > 0) & (f < nf) & (f % 2 == parity))
        def _(parity=parity):
            up_project(parity)
            down_project(1 - parity)

    @pl.when(f == nf)
    def _():
        down_project((nf - 1) % 2)
        o_ref[...] = _layer_norm(alpha * x_ref[...] + acc_ref[...], g_ref[...], b_ref[...])


def _conv_ffn_ln(x, w_up, conv_w, conv_b, w_down, g, b, alpha, seq, tm, tf):
    m, d = x.shape
    d_ff = w_down.shape[0]
    nf = d_ff // tf
    assert nf % 2 == 1
    row = lambda i, f: (i, 0)
    fixed = lambda i, f: (0, 0)
    ahead = lambda f: jnp.minimum(f, nf - 1)
    behind = lambda f: jnp.maximum(f - 1, 0)
    h_buf = pltpu.VMEM((tm + HALO, tf), F32)
    return pl.pallas_call(
        functools.partial(_ffn_kernel, tm=tm, tiles_per_seq=seq // tm, alpha=alpha),
        grid=(m // tm, nf + 1),
        in_specs=[pl.BlockSpec((tm, d), row),
                  pl.BlockSpec((d, tf), lambda i, f: (0, ahead(f))),
                  pl.BlockSpec((d, tf), lambda i, f: (0, nf + ahead(f))),
                  pl.BlockSpec((CONV_WIDTH, tf), lambda i, f: (0, behind(f))),
                  pl.BlockSpec((CONV_WIDTH, tf), lambda i, f: (0, nf + behind(f))),
                  pl.BlockSpec((1, tf), lambda i, f: (0, behind(f))),
                  pl.BlockSpec((1, tf), lambda i, f: (0, nf + behind(f))),
                  pl.BlockSpec((tf, d), lambda i, f: (behind(f), 0)),
                  pl.BlockSpec((1, d), fixed), pl.BlockSpec((1, d), fixed)],
        out_specs=pl.BlockSpec((tm, d), row),
        out_shape=jax.ShapeDtypeStruct((m, d), F32),
        scratch_shapes=[pltpu.VMEM((tm, d), BF16), h_buf, h_buf, h_buf, h_buf,
                        pltpu.VMEM((nf, HALO, tf), F32), pltpu.VMEM((nf, HALO, tf), F32),
                        pltpu.VMEM((tm, d), F32)],
        compiler_params=_params(("arbitrary", "arbitrary")),
    )(x, w_up, w_up, conv_w, conv_w, conv_b, conv_b, w_down, g, b)


def _alibi_slopes(n):
    return (2.0 ** (-8.0 * np.arange(1, n + 1) / n)).astype(np.float32)


def kernel(x, mem, w_in, w_mix_out, lambda_q1, lambda_k1, lambda_q2, lambda_k2, g_diff, g_dil, ln1_g, ln1_b,
           w_mem_q, w_mem_kv, w_mem_o, ln2_g, ln2_b, w_up, conv_w, conv_b, w_down, ln3_g, ln3_b):
    batch, seq, d = x.shape
    depth = w_in.shape[0]
    mem_len = mem.shape[1]
    diff_heads = dil_heads = d // 256
    alpha = (2 * depth) ** 0.25
    slopes = _alibi_slopes(diff_heads + dil_heads)
    slopes_diff, slopes_dil = jnp.asarray(slopes[0::2]), jnp.asarray(slopes[1::2])
    row = lambda a: a.reshape(1, -1)

    h = x.reshape(batch * seq, d)
    mem2 = mem.reshape(batch * mem_len, d)
    for l in range(depth):
        lam_init = 0.8 - 0.6 * math.exp(-0.3 * l)
        proj = _project(h, w_in[l].astype(BF16), tm=1024, tn=1024).reshape(batch, seq, -1)
        o_diff = _diff_attention(proj, slopes_diff, row(lambda_q1[l]), row(lambda_k1[l]), row(lambda_q2[l]),
                                 row(lambda_k2[l]), row(g_diff[l]), lam_init, diff_heads, t=512)
        o_dil = _dilated_attention(proj, slopes_dil, row(g_dil[l]), dil_heads, first_blk=3 * diff_heads)
        h = _outproj_ln([o_diff.reshape(batch * seq, -1), o_dil.reshape(batch * seq, -1)],
                        w_mix_out[l].astype(BF16), h, row(ln1_g[l]), row(ln1_b[l]), alpha, tm=512)

        kv = _project(mem2, w_mem_kv[l].astype(BF16), tm=1024, tn=1024).reshape(batch, mem_len, 2 * d)
        o_mem = _mem_attention(h, w_mem_q[l].astype(BF16), kv, seq, tm=512)
        h = _outproj_ln([o_mem], w_mem_o[l].astype(BF16), h, row(ln2_g[l]), row(ln2_b[l]), alpha, tm=512)

        h = _conv_ffn_ln(h, w_up[l].astype(BF16), conv_w[l], row(conv_b[l]), w_down[l].astype(BF16),
                         row(ln3_g[l]), row(ln3_b[l]), alpha, seq, tm=512, tf=512)
    return h.reshape(batch, seq, d)
```

```python
import functools
import math

import numpy as np
import jax
import jax.numpy as jnp
from jax import lax
from jax.experimental import pallas as pl
from jax.experimental.pallas import tpu as pltpu

F32 = jnp.float32
BF16 = jnp.bfloat16

LN_EPS = 1e-5
RMS_EPS = 1e-5
BLOCK = 128
DIFF_HALF_DIM = 64
DIL_CONFIGS = ((128, 1), (512, 4), (2048, 16))
TILES_PER_BODY = 8
MEM_HEADS = 4
CONV_WIDTH = 3
NEG_INF = float("-inf")

V7X_VMEM_LIMIT_BYTES = 56 * 1024 * 1024


def _params(semantics):
    return pltpu.CompilerParams(dimension_semantics=semantics, vmem_limit_bytes=V7X_VMEM_LIMIT_BYTES)


def _resident(shape, index_map):
    return pl.BlockSpec(shape, index_map, pipeline_mode=pl.Buffered(1))


def _layer_norm(z, g, b):
    mu = jnp.mean(z, axis=-1, keepdims=True)
    zc = z - mu
    var = jnp.mean(zc * zc, axis=-1, keepdims=True)
    return zc * lax.rsqrt(var + LN_EPS) * g + b


def _rms_norm(o, g):
    return o * lax.rsqrt(jnp.mean(o * o, axis=-1, keepdims=True) + RMS_EPS) * g


def _dot(a, b):
    return jnp.dot(a, b, preferred_element_type=F32)


def _dot_nt(a, b):
    return lax.dot_general(a, b, (((1,), (1,)), ((), ())), preferred_element_type=F32)


def _proj_kernel(x_ref, w_ref, cs_ref, o_ref, xb_ref):
    @pl.when(pl.program_id(1) == 0)
    def _():
        xb_ref[...] = x_ref[...].astype(BF16)

    o_ref[...] = (_dot(xb_ref[...], w_ref[...]) * cs_ref[...]).astype(o_ref.dtype)


def _project(x, w, col_scale, tm, tn):
    m, k = x.shape
    n = w.shape[1]
    tm = min(tm, m)
    return pl.pallas_call(
        _proj_kernel,
        grid=(m // tm, n // tn),
        in_specs=[pl.BlockSpec((tm, k), lambda i, j: (i, 0)),
                  pl.BlockSpec((k, tn), lambda i, j: (0, j)),
                  pl.BlockSpec((1, tn), lambda i, j: (0, j))],
        out_specs=pl.BlockSpec((tm, tn), lambda i, j: (i, j)),
        out_shape=jax.ShapeDtypeStruct((m, n), BF16),
        scratch_shapes=[pltpu.VMEM((tm, k), BF16)],
        compiler_params=_params(("arbitrary", "arbitrary")),
    )(x, w, col_scale)


def _diff_attn_kernel(slope_ref, q_ref, k_ref, v_ref, lq1_ref, lk1_ref, lq2_ref, lk2_ref, g_ref, o_ref,
                      vt_ref, *, t, lam_init):
    h = pl.program_id(1)
    slope = slope_ref[h, SLOPE_PARTS]
    n_tiles = vt_ref.shape[0]
    for j in range(n_tiles):
        vt_ref[j, :BLOCK, :] = v_ref[0, j * t:(j + 1) * t, :].astype(F32).T.astype(BF16)
        vt_ref[j, BLOCK:, :] = jnp.ones((vt_ref.shape[1] - BLOCK, t), BF16)

    lam = (jnp.exp(jnp.sum(lq1_ref[...] * lk1_ref[...], axis=-1, keepdims=True))
           - jnp.exp(jnp.sum(lq2_ref[...] * lk2_ref[...], axis=-1, keepdims=True)) + lam_init)
    lane = lax.broadcasted_iota(jnp.int32, (t, BLOCK), 1)
    key_index = lax.broadcasted_iota(jnp.int32, (t, BLOCK), 0)
    low, high = key_index & (V7X_MXU_COLUMNS - 1), key_index >> 8
    assert t <= 2 * V7X_MXU_COLUMNS and V7X_MXU_COLUMNS == 1 << 8
    k_extra = jnp.where(lane < SLOPE_PARTS, low, jnp.where(lane < 2 * SLOPE_PARTS, high, 0)).astype(F32).astype(BF16)
    q_extra = jnp.zeros((2 * t, BLOCK), F32)
    lane2 = lax.broadcasted_iota(jnp.int32, (2 * t, BLOCK), 1)
    for i in range(SLOPE_PARTS):
        q_extra = jnp.where(lane2 == i, slope_ref[h, i], q_extra)
        q_extra = jnp.where(lane2 == SLOPE_PARTS + i, slope_ref[h, i] * float(V7X_MXU_COLUMNS), q_extra)
    q_extra = q_extra.astype(BF16)
    key = lax.broadcasted_iota(jnp.int32, (t, 2 * t), 0)
    query = lax.broadcasted_iota(jnp.int32, (t, 2 * t), 1)
    causal = key <= jnp.where(query >= t, query - t, query)

    for qi in range(n_tiles):
        q = q_ref[0, qi * t:(qi + 1) * t, :]
        zero = jnp.zeros_like(q)
        qq = jnp.concatenate([jnp.where(lane < DIFF_HALF_DIM, q, zero),
                              jnp.where(lane >= DIFF_HALF_DIM, q, zero)], axis=0)
        qq = jnp.concatenate([qq, q_extra], axis=1)
        m = acc = None
        for j in range(qi + 1):
            k = jnp.concatenate([k_ref[0, j * t:(j + 1) * t, :], k_extra], axis=1)
            s = _dot_nt(k, qq)
            if j == qi:
                s = jnp.where(causal, s, NEG_INF)
            shift = slope * float((j - qi) * t)
            m_tile = jnp.max(s, axis=0, keepdims=True) + shift
            m_new = m_tile if j == 0 else jnp.maximum(m, m_tile)
            p = jnp.exp2(s - (m_new - shift))
            acc_tile = _dot(vt_ref[j], p.astype(BF16))
            acc = acc_tile if j == 0 else jnp.exp2(m - m_new) * acc + acc_tile
            m = m_new
        o = acc[:BLOCK] / acc[BLOCK:BLOCK + 1]
        o = (o[:, :t] - lam * o[:, t:]).T
        o_ref[0, qi * t:(qi + 1) * t, :] = (_rms_norm(o, g_ref[...]) * (1.0 - lam_init)).astype(o_ref.dtype)


def _diff_attention(proj, slopes, lq1, lk1, lq2, lk2, g, lam_init, heads, t):
    b, s, _ = proj.shape
    q_blk, k_blk, v_blk = 0, heads, 2 * heads
    vec = lambda n: pl.BlockSpec((1, n), lambda bi, hi: (0, 0))
    slab = lambda blk: pl.BlockSpec((1, s, BLOCK), lambda bi, hi: (bi, 0, blk + hi))
    return pl.pallas_call(
        functools.partial(_diff_attn_kernel, t=t, lam_init=lam_init),
        grid=(b, heads),
        in_specs=[pl.BlockSpec(memory_space=pltpu.SMEM), slab(q_blk), slab(k_blk), slab(v_blk),
                  vec(DIFF_HALF_DIM), vec(DIFF_HALF_DIM), vec(DIFF_HALF_DIM), vec(DIFF_HALF_DIM), vec(BLOCK)],
        out_specs=slab(0),
        out_shape=jax.ShapeDtypeStruct((b, s, heads * BLOCK), BF16),
        scratch_shapes=[pltpu.VMEM((s // t, BLOCK + BF16_SUBLANES, t), BF16)],
        compiler_params=_params(("arbitrary", "arbitrary")),
    )(slopes, proj, proj, proj, lq1, lk1, lq2, lk2, g)


def _split_slopes(slopes):
    x = np.asarray(slopes, np.float64) * math.log2(math.e)
    x = x.astype(np.float32)
    parts, rest = [], x.copy()
    for _ in range(SLOPE_PARTS):
        part = rest.astype(BF16).astype(np.float32)
        parts.append(part)
        rest = rest - part
    return jnp.asarray(np.stack(parts + [x], axis=1), F32)


def _dilated_attn_kernel(slope_ref, q_ref, k_ref, v_ref, g_ref, o_ref, qf, kf, vf, out_ref, lse_ref,
                         *, seq, tc):
    h = pl.program_id(1)
    slope = slope_ref[h]
    qf[...] = q_ref[0].astype(F32)
    kf[...] = k_ref[0].astype(F32)
    vf[...] = v_ref[0].astype(F32)

    qi = lax.broadcasted_iota(jnp.int32, (BLOCK, 2 * BLOCK), 0)
    kj = lax.broadcasted_iota(jnp.int32, (BLOCK, 2 * BLOCK), 1)
    rel = qi + BLOCK - kj

    def rows(start, size, dil):
        return pl.ds(start, size) if dil == 1 else pl.ds(start, size, stride=dil)

    def tiles(branch, dil, starts, bias):
        nkeys = bias.shape[1]
        q_rows = [rows(q_start, BLOCK, dil) for q_start, _ in starts]
        k_rows = [rows(k_start, nkeys, dil) for _, k_start in starts]
        ones = jnp.ones((nkeys, BLOCK), BF16)
        s = [_dot_nt(qf[qr, :].astype(BF16), kf[kr, :].astype(BF16)) + bias for qr, kr in zip(q_rows, k_rows)]
        m = [jnp.max(si, axis=-1, keepdims=True) for si in s]
        p = [jnp.exp2(si - mi).astype(BF16) for si, mi in zip(s, m)]
        v1 = [jnp.concatenate([vf[kr, :].astype(BF16), ones], axis=1) for kr in k_rows]
        r = [_dot(pi, vi) for pi, vi in zip(p, v1)]
        for qr, mi, ri in zip(q_rows, m, r):
            l = ri[:, BLOCK:]
            out_ref[branch, qr, :] = ri[:, :BLOCK] / l
            lse_ref[branch, qr, :] = mi + jnp.log2(l)

    for branch, (window, dil) in enumerate(DIL_CONFIGS):
        n_back = window // dil
        nb = seq // dil // BLOCK
        step = dil * BLOCK
        valid = (rel >= 0) & (rel <= n_back)
        bias = jnp.where(valid, (-slope * dil) * rel.astype(F32), NEG_INF)
        bias_first = bias[:, BLOCK:]

        def first_tiles(classes, branch=branch, dil=dil, bias_first=bias_first):
            tiles(branch, dil, [(c, c) for c in classes], bias_first)

        def later_tiles(c, blocks, branch=branch, dil=dil, step=step, bias=bias):
            tiles(branch, dil, [(c + n * step, c + (n - 1) * step) for n in blocks], bias)

        if nb == 1:
            def class_group(i, carry, first_tiles=first_tiles):
                first_tiles([i * TILES_PER_BODY + u for u in range(TILES_PER_BODY)])
                return carry
            lax.fori_loop(0, dil // TILES_PER_BODY, class_group, 0)
        elif nb <= TILES_PER_BODY:
            per_body = TILES_PER_BODY // nb
            def class_group(i, carry, first_tiles=first_tiles, later_tiles=later_tiles, nb=nb, per_body=per_body):
                classes = [i * per_body + u for u in range(per_body)]
                first_tiles(classes)
                for c in classes:
                    later_tiles(c, range(1, nb))
                return carry
            lax.fori_loop(0, dil // per_body, class_group, 0)
        else:
            group = 5
            assert dil == 1 and (nb - 1) % group == 0
            first_tiles([0])
            def block_group(i, carry, later_tiles=later_tiles, group=group):
                later_tiles(0, [1 + i * group + u for u in range(group)])
                return carry
            lax.fori_loop(0, (nb - 1) // group, block_group, 0)

    g = g_ref[...]
    for t in range(seq // tc):
        r = pl.ds(t * tc, tc)
        lses = [lse_ref[i, r, :] for i in range(len(DIL_CONFIGS))]
        top = functools.reduce(jnp.maximum, lses)
        ws = [jnp.exp2(lse - top) for lse in lses]
        num = sum(w * out_ref[i, r, :] for i, w in enumerate(ws))
        o_ref[0, r, :] = _rms_norm(num / sum(ws), g).astype(o_ref.dtype)


def _dilated_attention(proj, slopes, g, heads, first_blk):
    b, s, _ = proj.shape
    nbr = len(DIL_CONFIGS)
    spec = lambda blk: pl.BlockSpec((1, s, BLOCK), lambda bi, hi: (bi, 0, blk + hi))
    return pl.pallas_call(
        functools.partial(_dilated_attn_kernel, seq=s, tc=BLOCK),
        grid=(b, heads),
        in_specs=[pl.BlockSpec(memory_space=pltpu.SMEM),
                  spec(first_blk), spec(first_blk + heads), spec(first_blk + 2 * heads),
                  pl.BlockSpec((1, BLOCK), lambda bi, hi: (0, 0))],
        out_specs=pl.BlockSpec((1, s, BLOCK), lambda bi, hi: (bi, 0, hi)),
        out_shape=jax.ShapeDtypeStruct((b, s, heads * BLOCK), BF16),
        scratch_shapes=[pltpu.VMEM((s, BLOCK), F32), pltpu.VMEM((s, BLOCK), F32), pltpu.VMEM((s, BLOCK), F32),
                        pltpu.VMEM((nbr, s, BLOCK), F32), pltpu.VMEM((nbr, s, BLOCK), F32)],
        compiler_params=_params(("arbitrary", "arbitrary")),
    )(slopes, proj, proj, proj, g)


def _outproj_ln_kernel(*refs, n_in, alpha):
    o_refs, w_refs = refs[:n_in], refs[n_in:2 * n_in]
    h_ref, g_ref, b_ref, out_ref = refs[2 * n_in:]
    y = _dot(o_refs[0][...], w_refs[0][...])
    for o_ref, w_ref in zip(o_refs[1:], w_refs[1:]):
        y = y + _dot(o_ref[...], w_ref[...])
    out_ref[...] = _layer_norm(alpha * h_ref[...] + y, g_ref[...], b_ref[...])


def _outproj_ln(os, w, h, g, b, alpha, tm):
    m, d = h.shape
    n_in = len(os)
    kw = w.shape[0] // n_in
    return pl.pallas_call(
        functools.partial(_outproj_ln_kernel, n_in=n_in, alpha=alpha),
        grid=(m // tm,),
        in_specs=([pl.BlockSpec((tm, kw), lambda i: (i, 0)) for _ in os]
                  + [_resident((kw, d), functools.partial(lambda i, r: (r, 0), r=r)) for r in range(n_in)]
                  + [pl.BlockSpec((tm, d), lambda i: (i, 0)),
                     pl.BlockSpec((1, d), lambda i: (0, 0)), pl.BlockSpec((1, d), lambda i: (0, 0))]),
        out_specs=pl.BlockSpec((tm, d), lambda i: (i, 0)),
        out_shape=jax.ShapeDtypeStruct((m, d), F32),
        compiler_params=_params(("arbitrary",)),
    )(*os, *([w] * n_in), h, g, b)


def _mem_attn_kernel(h_ref, wq_ref, kv_ref, o_ref, *, heads):
    d = h_ref.shape[1]
    hd = d // heads
    q = _dot(h_ref[...].astype(BF16), wq_ref[...]).astype(BF16)
    scale = hd ** -0.5
    for i in range(heads):
        k = kv_ref[0, :, i * hd:(i + 1) * hd]
        v = kv_ref[0, :, d + i * hd:d + (i + 1) * hd]
        s = _dot_nt(q[:, i * hd:(i + 1) * hd], k) * scale
        e = jnp.exp(s - jnp.max(s, axis=-1, keepdims=True))
        p = e * (1.0 / jnp.sum(e, axis=-1, keepdims=True))
        o_ref[:, i * hd:(i + 1) * hd] = _dot(p.astype(BF16), v).astype(o_ref.dtype)


def _mem_attention(h, wq, kv, seq, tm):
    m, d = h.shape
    mem_len = kv.shape[1]
    per_seq = seq // tm
    return pl.pallas_call(
        functools.partial(_mem_attn_kernel, heads=MEM_HEADS),
        grid=(m // tm,),
        in_specs=[pl.BlockSpec((tm, d), lambda i: (i, 0)),
                  _resident((d, d), lambda i: (0, 0)),
                  pl.BlockSpec((1, mem_len, 2 * d), lambda i: (i // per_seq, 0, 0))],
        out_specs=pl.BlockSpec((tm, d), lambda i: (i, 0)),
        out_shape=jax.ShapeDtypeStruct((m, d), BF16),
        compiler_params=_params(("arbitrary",)),
    )(h, wq, kv)


HALO = 8
FFN_ROW_CHUNK = 64
V7X_MXU_COLUMNS = 256
BF16_SUBLANES = 16
SLOPE_PARTS = 3


def _ffn_kernel(x_ref, wg_ref, wu_ref, cwg_ref, cwu_ref, cbg_ref, cbu_ref, wd_ref, g_ref, b_ref, o_ref,
                xb_ref, hg0_ref, hu0_ref, hg1_ref, hu1_ref, act0_ref, act1_ref, carry_g_ref, carry_u_ref, acc_ref,
                *, tm, tiles_per_seq, alpha):
    i = pl.program_id(0)
    f = pl.program_id(1)
    nf = pl.num_programs(1) - 2
    seq_start = (i % tiles_per_seq) == 0
    h_refs = ((hg0_ref, hu0_ref), (hg1_ref, hu1_ref))
    act_refs = (act0_ref, act1_ref)

    tf = wg_ref.shape[1]
    d = wd_ref.shape[1]

    def up_project(slot):
        def part(which, c0):
            w_ref, hs_ref, carry_ref = ((wg_ref, wu_ref)[which], h_refs[slot][which],
                                        (carry_g_ref, carry_u_ref)[which])
            cols = slice(c0, c0 + V7X_MXU_COLUMNS)
            hs_ref[HALO:, cols] = _dot(xb_ref[...], w_ref[:, cols])
            hs_ref[:HALO, cols] = jnp.where(seq_start, 0.0, carry_ref[f, :, cols])
            carry_ref[f, :, cols] = hs_ref[tm:, cols]
        return [functools.partial(part, which, c0) for which in range(2) for c0 in range(0, tf, V7X_MXU_COLUMNS)]

    def conv(hs_ref, cw_ref, cb_ref, row0, cols):
        out = cb_ref[:, cols]
        for tap in range(CONV_WIDTH):
            off = row0 + HALO - (CONV_WIDTH - 1) + tap
            out = out + hs_ref[off:off + FFN_ROW_CHUNK, cols] * cw_ref[tap:tap + 1, cols]
        return out

    def activate(slot):
        def part(c0, row0):
            cols = slice(c0, c0 + BLOCK)
            gate = conv(h_refs[slot][0], cwg_ref, cbg_ref, row0, cols)
            up = conv(h_refs[slot][1], cwu_ref, cbu_ref, row0, cols)
            act = gate * (1.0 / (1.0 + jnp.exp(-gate))) * up
            act_refs[slot][row0:row0 + FFN_ROW_CHUNK, cols] = act.astype(BF16)
        return [functools.partial(part, c0, row0) for c0 in range(0, tf, BLOCK)
                for row0 in range(0, tm, FFN_ROW_CHUNK)]

    def down_project(slot):
        def part(c0):
            cols = slice(c0, c0 + V7X_MXU_COLUMNS)
            acc_ref[:, cols] += _dot(act_refs[slot][...], wd_ref[:, cols])
        return [functools.partial(part, c0) for c0 in range(0, d, V7X_MXU_COLUMNS)]

    def run(*stages):
        order = sorted(((k + 0.5) / len(parts), s, k) for s, parts in enumerate(stages) for k in range(len(parts)))
        for _, s, k in order:
            stages[s][k]()

    @pl.when(f == 0)
    def _():
        @pl.when(i == 0)
        def _():
            carry_g_ref[...] = jnp.zeros(carry_g_ref.shape, F32)
            carry_u_ref[...] = jnp.zeros(carry_u_ref.shape, F32)

        xb_ref[...] = x_ref[...].astype(BF16)
        acc_ref[...] = jnp.zeros(acc_ref.shape, F32)
        run(up_project(0))

    @pl.when(f == 1)
    def _():
        run(up_project(1), activate(0))

    for parity in range(2):
        @pl.when((f > 1) & (f < nf) & (f % 2 == parity))
        def _(parity=parity):
            run(up_project(parity), activate(1 - parity), down_project(parity))

    @pl.when(f == nf)
    def _():
        run(down_project((nf - 2) % 2), activate((nf - 1) % 2))

    @pl.when(f == nf + 1)
    def _():
        run(down_project((nf - 1) % 2))
        o_ref[...] = _layer_norm(alpha * x_ref[...] + acc_ref[...], g_ref[...], b_ref[...])


def _conv_ffn_ln(x, w_up, conv_w, conv_b, w_down, g, b, alpha, seq, tm, tf):
    m, d = x.shape
    d_ff = w_down.shape[0]
    nf = d_ff // tf
    row = lambda i, f: (i, 0)
    fixed = lambda i, f: (0, 0)
    chunk = lambda f, lag: jnp.clip(f - lag, 0, nf - 1)
    h_buf = pltpu.VMEM((tm + HALO, tf), F32)
    act_buf = pltpu.VMEM((tm, tf), BF16)
    return pl.pallas_call(
        functools.partial(_ffn_kernel, tm=tm, tiles_per_seq=seq // tm, alpha=alpha),
        grid=(m // tm, nf + 2),
        in_specs=[pl.BlockSpec((tm, d), row),
                  pl.BlockSpec((d, tf), lambda i, f: (0, chunk(f, 0))),
                  pl.BlockSpec((d, tf), lambda i, f: (0, nf + chunk(f, 0))),
                  pl.BlockSpec((CONV_WIDTH, tf), lambda i, f: (0, chunk(f, 1))),
                  pl.BlockSpec((CONV_WIDTH, tf), lambda i, f: (0, nf + chunk(f, 1))),
                  pl.BlockSpec((1, tf), lambda i, f: (0, chunk(f, 1))),
                  pl.BlockSpec((1, tf), lambda i, f: (0, nf + chunk(f, 1))),
                  pl.BlockSpec((tf, d), lambda i, f: (chunk(f, 2), 0)),
                  pl.BlockSpec((1, d), fixed), pl.BlockSpec((1, d), fixed)],
        out_specs=pl.BlockSpec((tm, d), row),
        out_shape=jax.ShapeDtypeStruct((m, d), F32),
        scratch_shapes=[pltpu.VMEM((tm, d), BF16), h_buf, h_buf, h_buf, h_buf, act_buf, act_buf,
                        pltpu.VMEM((nf, HALO, tf), F32), pltpu.VMEM((nf, HALO, tf), F32),
                        pltpu.VMEM((tm, d), F32)],
        compiler_params=_params(("arbitrary", "arbitrary")),
    )(x, w_up, w_up, conv_w, conv_w, conv_b, conv_b, w_down, g, b)


def _alibi_slopes(n):
    return (2.0 ** (-8.0 * np.arange(1, n + 1) / n)).astype(np.float32)


def kernel(x, mem, w_in, w_mix_out, lambda_q1, lambda_k1, lambda_q2, lambda_k2, g_diff, g_dil, ln1_g, ln1_b,
           w_mem_q, w_mem_kv, w_mem_o, ln2_g, ln2_b, w_up, conv_w, conv_b, w_down, ln3_g, ln3_b):
    batch, seq, d = x.shape
    depth = w_in.shape[0]
    mem_len = mem.shape[1]
    diff_heads = dil_heads = d // 256
    alpha = (2 * depth) ** 0.25
    slopes = _alibi_slopes(diff_heads + dil_heads)
    log2e = math.log2(math.e)
    slopes_diff = _split_slopes(slopes[0::2])
    slopes_dil = jnp.asarray(slopes[1::2].astype(np.float64) * log2e, F32)
    row = lambda a: a.reshape(1, -1)

    group = diff_heads * BLOCK
    col_scale = np.ones((1, 6 * group), np.float32)
    col_scale[:, :group] = DIFF_HALF_DIM ** -0.5 * log2e
    col_scale[:, 3 * group:4 * group] = BLOCK ** -0.5 * log2e
    col_scale = jnp.asarray(col_scale)
    kv_scale = jnp.ones((1, 2 * d), F32)

    h = x.reshape(batch * seq, d)
    mem2 = mem.reshape(batch * mem_len, d)
    for l in range(depth):
        lam_init = 0.8 - 0.6 * math.exp(-0.3 * l)
        proj = _project(h, w_in[l].astype(BF16), col_scale, tm=1024, tn=1024).reshape(batch, seq, -1)
        o_diff = _diff_attention(proj, slopes_diff, row(lambda_q1[l]), row(lambda_k1[l]), row(lambda_q2[l]),
                                 row(lambda_k2[l]), row(g_diff[l]), lam_init, diff_heads, t=512)
        o_dil = _dilated_attention(proj, slopes_dil, row(g_dil[l]), dil_heads, first_blk=3 * diff_heads)
        h = _outproj_ln([o_diff.reshape(batch * seq, -1), o_dil.reshape(batch * seq, -1)],
                        w_mix_out[l].astype(BF16), h, row(ln1_g[l]), row(ln1_b[l]), alpha, tm=512)

        kv = _project(mem2, w_mem_kv[l].astype(BF16), kv_scale, tm=1024, tn=1024).reshape(batch, mem_len, 2 * d)
        o_mem = _mem_attention(h, w_mem_q[l].astype(BF16), kv, seq, tm=512)
        h = _outproj_ln([o_mem], w_mem_o[l].astype(BF16), h, row(ln2_g[l]), row(ln2_b[l]), alpha, tm=512)

        h = _conv_ffn_ln(h, w_up[l].astype(BF16), conv_w[l], row(conv_b[l]), w_down[l].astype(BF16),
                         row(ln3_g[l]), row(ln3_b[l]), alpha, seq, tm=512, tf=512)
    return h.reshape(batch, seq, d)
```

```python
import functools
import math

import numpy as np
import jax
import jax.numpy as jnp
from jax import lax
from jax.experimental import pallas as pl
from jax.experimental.pallas import tpu as pltpu

F32 = jnp.float32
BF16 = jnp.bfloat16

LN_EPS = 1e-5
RMS_EPS = 1e-5
BLOCK = 128
DIFF_HALF_DIM = 64
DIL_CONFIGS = ((128, 1), (512, 4), (2048, 16))
TILES_PER_BODY = 8
MEM_HEADS = 4
CONV_WIDTH = 3
NEG_INF = float("-inf")

V7X_VMEM_LIMIT_BYTES = 56 * 1024 * 1024


def _params(semantics):
    return pltpu.CompilerParams(dimension_semantics=semantics, vmem_limit_bytes=V7X_VMEM_LIMIT_BYTES)


def _resident(shape, index_map):
    return pl.BlockSpec(shape, index_map, pipeline_mode=pl.Buffered(1))


def _layer_norm(z, g, b):
    mu = jnp.mean(z, axis=-1, keepdims=True)
    zc = z - mu
    var = jnp.mean(zc * zc, axis=-1, keepdims=True)
    return zc * lax.rsqrt(var + LN_EPS) * g + b


def _rms_norm(o, g):
    return o * lax.rsqrt(jnp.mean(o * o, axis=-1, keepdims=True) + RMS_EPS) * g


def _dot(a, b):
    return jnp.dot(a, b, preferred_element_type=F32)


def _dot_nt(a, b):
    return lax.dot_general(a, b, (((1,), (1,)), ((), ())), preferred_element_type=F32)


def _proj_kernel(x_ref, w_ref, cs_ref, o_ref, xb_ref):
    @pl.when(pl.program_id(1) == 0)
    def _():
        xb_ref[...] = x_ref[...].astype(BF16)

    o_ref[...] = (_dot(xb_ref[...], w_ref[...]) * cs_ref[...]).astype(o_ref.dtype)


def _project(x, w, col_scale, tm, tn):
    m, k = x.shape
    n = w.shape[1]
    tm = min(tm, m)
    return pl.pallas_call(
        _proj_kernel,
        grid=(m // tm, n // tn),
        in_specs=[pl.BlockSpec((tm, k), lambda i, j: (i, 0)),
                  pl.BlockSpec((k, tn), lambda i, j: (0, j)),
                  pl.BlockSpec((1, tn), lambda i, j: (0, j))],
        out_specs=pl.BlockSpec((tm, tn), lambda i, j: (i, j)),
        out_shape=jax.ShapeDtypeStruct((m, n), BF16),
        scratch_shapes=[pltpu.VMEM((tm, k), BF16)],
        compiler_params=_params(("arbitrary", "arbitrary")),
    )(x, w, col_scale)


def _diff_attn_kernel(slope_ref, q_ref, k_ref, v_ref, lq1_ref, lk1_ref, lq2_ref, lk2_ref, g_ref, o_ref,
                      vt_ref, *, t, lam_init):
    h = pl.program_id(1)
    slope = slope_ref[h, SLOPE_PARTS]
    n_tiles = vt_ref.shape[0]
    for j in range(n_tiles):
        vt_ref[j, :BLOCK, :] = v_ref[0, j * t:(j + 1) * t, :].astype(F32).T.astype(BF16)
        vt_ref[j, BLOCK:, :] = jnp.ones((vt_ref.shape[1] - BLOCK, t), BF16)

    lam = (jnp.exp(jnp.sum(lq1_ref[...] * lk1_ref[...], axis=-1, keepdims=True))
           - jnp.exp(jnp.sum(lq2_ref[...] * lk2_ref[...], axis=-1, keepdims=True)) + lam_init)
    lane = lax.broadcasted_iota(jnp.int32, (t, BLOCK), 1)
    key_index = lax.broadcasted_iota(jnp.int32, (t, BLOCK), 0)
    low, high = key_index & (V7X_MXU_COLUMNS - 1), key_index >> 8
    assert t <= 2 * V7X_MXU_COLUMNS and V7X_MXU_COLUMNS == 1 << 8
    k_extra = jnp.where(lane < SLOPE_PARTS, low, jnp.where(lane < 2 * SLOPE_PARTS, high, 0)).astype(F32).astype(BF16)
    q_extra = jnp.zeros((2 * t, BLOCK), F32)
    lane2 = lax.broadcasted_iota(jnp.int32, (2 * t, BLOCK), 1)
    for i in range(SLOPE_PARTS):
        q_extra = jnp.where(lane2 == i, slope_ref[h, i], q_extra)
        q_extra = jnp.where(lane2 == SLOPE_PARTS + i, slope_ref[h, i] * float(V7X_MXU_COLUMNS), q_extra)
    q_extra = q_extra.astype(BF16)
    key = lax.broadcasted_iota(jnp.int32, (t, 2 * t), 0)
    query = lax.broadcasted_iota(jnp.int32, (t, 2 * t), 1)
    causal = key <= jnp.where(query >= t, query - t, query)

    def queries(qi):
        q = q_ref[0, qi * t:(qi + 1) * t, :]
        zero = jnp.zeros_like(q)
        qq = jnp.concatenate([jnp.where(lane < DIFF_HALF_DIM, q, zero),
                              jnp.where(lane >= DIFF_HALF_DIM, q, zero)], axis=0)
        return jnp.concatenate([qq, q_extra], axis=1)

    order = sorted(range(n_tiles), key=lambda qi: -qi)
    chains = [[], []]
    for qi in order:
        min(chains, key=len).extend((qi, j) for j in range(qi + 1))
    state = {}
    for step in range(max(len(c) for c in chains)):
        live = [c[step] for c in chains if step < len(c)]
        for qi, j in live:
            if j == 0:
                state[qi] = dict(qq=queries(qi))
        s = {}
        for qi, j in live:
            k = jnp.concatenate([k_ref[0, j * t:(j + 1) * t, :], k_extra], axis=1)
            s[qi] = _dot_nt(k, state[qi]["qq"])
            if j == qi:
                s[qi] = jnp.where(causal, s[qi], NEG_INF)
        p = {}
        for qi, j in live:
            st = state[qi]
            shift = slope * float((j - qi) * t)
            m_tile = jnp.max(s[qi], axis=0, keepdims=True) + shift
            st["m_old"], st["m"] = st.get("m"), (m_tile if j == 0 else jnp.maximum(st["m"], m_tile))
            p[qi] = jnp.exp2(s[qi] - (st["m"] - shift)).astype(BF16)
        for qi, j in live:
            st = state[qi]
            acc_tile = _dot(vt_ref[j], p[qi])
            st["acc"] = acc_tile if j == 0 else jnp.exp2(st["m_old"] - st["m"]) * st["acc"] + acc_tile
        for qi, j in live:
            if j == qi:
                acc = state.pop(qi)["acc"]
                o = acc[:BLOCK] / acc[BLOCK:BLOCK + 1]
                o = (o[:, :t] - lam * o[:, t:]).T
                o_ref[0, qi * t:(qi + 1) * t, :] = (_rms_norm(o, g_ref[...]) * (1.0 - lam_init)).astype(o_ref.dtype)


def _diff_attention(proj, slopes, lq1, lk1, lq2, lk2, g, lam_init, heads, t):
    b, s, _ = proj.shape
    q_blk, k_blk, v_blk = 0, heads, 2 * heads
    vec = lambda n: pl.BlockSpec((1, n), lambda bi, hi: (0, 0))
    slab = lambda blk: pl.BlockSpec((1, s, BLOCK), lambda bi, hi: (bi, 0, blk + hi))
    return pl.pallas_call(
        functools.partial(_diff_attn_kernel, t=t, lam_init=lam_init),
        grid=(b, heads),
        in_specs=[pl.BlockSpec(memory_space=pltpu.SMEM), slab(q_blk), slab(k_blk), slab(v_blk),
                  vec(DIFF_HALF_DIM), vec(DIFF_HALF_DIM), vec(DIFF_HALF_DIM), vec(DIFF_HALF_DIM), vec(BLOCK)],
        out_specs=slab(0),
        out_shape=jax.ShapeDtypeStruct((b, s, heads * BLOCK), BF16),
        scratch_shapes=[pltpu.VMEM((s // t, BLOCK + BF16_SUBLANES, t), BF16)],
        compiler_params=_params(("arbitrary", "arbitrary")),
    )(slopes, proj, proj, proj, lq1, lk1, lq2, lk2, g)


def _split_slopes(slopes):
    x = np.asarray(slopes, np.float64) * math.log2(math.e)
    x = x.astype(np.float32)
    parts, rest = [], x.copy()
    for _ in range(SLOPE_PARTS):
        part = rest.astype(BF16).astype(np.float32)
        parts.append(part)
        rest = rest - part
    return jnp.asarray(np.stack(parts + [x], axis=1), F32)


def _dilated_attn_kernel(slope_ref, q_ref, k_ref, v_ref, g_ref, o_ref, qf, kf, vf, out_ref, lse_ref,
                         *, seq, tc):
    h = pl.program_id(1)
    slope = slope_ref[h]
    qf[...] = q_ref[0].astype(F32)
    kf[...] = k_ref[0].astype(F32)
    vf[...] = v_ref[0].astype(F32)

    qi = lax.broadcasted_iota(jnp.int32, (BLOCK, 2 * BLOCK), 0)
    kj = lax.broadcasted_iota(jnp.int32, (BLOCK, 2 * BLOCK), 1)
    rel = qi + BLOCK - kj

    def rows(start, size, dil):
        return pl.ds(start, size) if dil == 1 else pl.ds(start, size, stride=dil)

    def tiles(branch, dil, starts, bias):
        nkeys = bias.shape[1]
        q_rows = [rows(q_start, BLOCK, dil) for q_start, _ in starts]
        k_rows = [rows(k_start, nkeys, dil) for _, k_start in starts]
        ones = jnp.ones((nkeys, BLOCK), BF16)
        s = [_dot_nt(qf[qr, :].astype(BF16), kf[kr, :].astype(BF16)) + bias for qr, kr in zip(q_rows, k_rows)]
        m = [jnp.max(si, axis=-1, keepdims=True) for si in s]
        p = [jnp.exp2(si - mi).astype(BF16) for si, mi in zip(s, m)]
        v1 = [jnp.concatenate([vf[kr, :].astype(BF16), ones], axis=1) for kr in k_rows]
        r = [_dot(pi, vi) for pi, vi in zip(p, v1)]
        for qr, mi, ri in zip(q_rows, m, r):
            l = ri[:, BLOCK:]
            out_ref[branch, qr, :] = ri[:, :BLOCK] / l
            lse_ref[branch, qr, :] = mi + jnp.log2(l)

    for branch, (window, dil) in enumerate(DIL_CONFIGS):
        n_back = window // dil
        nb = seq // dil // BLOCK
        step = dil * BLOCK
        valid = (rel >= 0) & (rel <= n_back)
        bias = jnp.where(valid, (-slope * dil) * rel.astype(F32), NEG_INF)
        bias_first = bias[:, BLOCK:]

        def first_tiles(classes, branch=branch, dil=dil, bias_first=bias_first):
            tiles(branch, dil, [(c, c) for c in classes], bias_first)

        def later_tiles(c, blocks, branch=branch, dil=dil, step=step, bias=bias):
            tiles(branch, dil, [(c + n * step, c + (n - 1) * step) for n in blocks], bias)

        if nb == 1:
            def class_group(i, carry, first_tiles=first_tiles):
                first_tiles([i * TILES_PER_BODY + u for u in range(TILES_PER_BODY)])
                return carry
            lax.fori_loop(0, dil // TILES_PER_BODY, class_group, 0)
        elif nb <= TILES_PER_BODY:
            per_body = TILES_PER_BODY // nb
            def class_group(i, carry, first_tiles=first_tiles, later_tiles=later_tiles, nb=nb, per_body=per_body):
                classes = [i * per_body + u for u in range(per_body)]
                first_tiles(classes)
                for c in classes:
                    later_tiles(c, range(1, nb))
                return carry
            lax.fori_loop(0, dil // per_body, class_group, 0)
        else:
            group = 5
            assert dil == 1 and (nb - 1) % group == 0
            first_tiles([0])
            def block_group(i, carry, later_tiles=later_tiles, group=group):
                later_tiles(0, [1 + i * group + u for u in range(group)])
                return carry
            lax.fori_loop(0, (nb - 1) // group, block_group, 0)

    g = g_ref[...]
    for t in range(seq // tc):
        r = pl.ds(t * tc, tc)
        lses = [lse_ref[i, r, :] for i in range(len(DIL_CONFIGS))]
        top = functools.reduce(jnp.maximum, lses)
        ws = [jnp.exp2(lse - top) for lse in lses]
        num = sum(w * out_ref[i, r, :] for i, w in enumerate(ws))
        o_ref[0, r, :] = _rms_norm(num / sum(ws), g).astype(o_ref.dtype)


def _dilated_attention(proj, slopes, g, heads, first_blk):
    b, s, _ = proj.shape
    nbr = len(DIL_CONFIGS)
    spec = lambda blk: pl.BlockSpec((1, s, BLOCK), lambda bi, hi: (bi, 0, blk + hi))
    return pl.pallas_call(
        functools.partial(_dilated_attn_kernel, seq=s, tc=BLOCK),
        grid=(b, heads),
        in_specs=[pl.BlockSpec(memory_space=pltpu.SMEM),
                  spec(first_blk), spec(first_blk + heads), spec(first_blk + 2 * heads),
                  pl.BlockSpec((1, BLOCK), lambda bi, hi: (0, 0))],
        out_specs=pl.BlockSpec((1, s, BLOCK), lambda bi, hi: (bi, 0, hi)),
        out_shape=jax.ShapeDtypeStruct((b, s, heads * BLOCK), BF16),
        scratch_shapes=[pltpu.VMEM((s, BLOCK), F32), pltpu.VMEM((s, BLOCK), F32), pltpu.VMEM((s, BLOCK), F32),
                        pltpu.VMEM((nbr, s, BLOCK), F32), pltpu.VMEM((nbr, s, BLOCK), F32)],
        compiler_params=_params(("arbitrary", "arbitrary")),
    )(slopes, proj, proj, proj, g)


def _outproj_ln_kernel(*refs, n_in, alpha):
    o_refs, w_refs = refs[:n_in], refs[n_in:2 * n_in]
    h_ref, g_ref, b_ref, out_ref = refs[2 * n_in:]
    y = _dot(o_refs[0][...], w_refs[0][...])
    for o_ref, w_ref in zip(o_refs[1:], w_refs[1:]):
        y = y + _dot(o_ref[...], w_ref[...])
    out_ref[...] = _layer_norm(alpha * h_ref[...] + y, g_ref[...], b_ref[...])


def _outproj_ln(os, w, h, g, b, alpha, tm):
    m, d = h.shape
    n_in = len(os)
    kw = w.shape[0] // n_in
    return pl.pallas_call(
        functools.partial(_outproj_ln_kernel, n_in=n_in, alpha=alpha),
        grid=(m // tm,),
        in_specs=([pl.BlockSpec((tm, kw), lambda i: (i, 0)) for _ in os]
                  + [_resident((kw, d), functools.partial(lambda i, r: (r, 0), r=r)) for r in range(n_in)]
                  + [pl.BlockSpec((tm, d), lambda i: (i, 0)),
                     pl.BlockSpec((1, d), lambda i: (0, 0)), pl.BlockSpec((1, d), lambda i: (0, 0))]),
        out_specs=pl.BlockSpec((tm, d), lambda i: (i, 0)),
        out_shape=jax.ShapeDtypeStruct((m, d), F32),
        compiler_params=_params(("arbitrary",)),
    )(*os, *([w] * n_in), h, g, b)


def _mem_attn_kernel(h_ref, wq_ref, kv_ref, o_ref, *, heads):
    d = h_ref.shape[1]
    hd = d // heads
    q = _dot(h_ref[...].astype(BF16), wq_ref[...]).astype(BF16)
    scale = hd ** -0.5
    for i in range(heads):
        k = kv_ref[0, :, i * hd:(i + 1) * hd]
        v = kv_ref[0, :, d + i * hd:d + (i + 1) * hd]
        s = _dot_nt(q[:, i * hd:(i + 1) * hd], k) * scale
        e = jnp.exp(s - jnp.max(s, axis=-1, keepdims=True))
        p = e * (1.0 / jnp.sum(e, axis=-1, keepdims=True))
        o_ref[:, i * hd:(i + 1) * hd] = _dot(p.astype(BF16), v).astype(o_ref.dtype)


def _mem_attention(h, wq, kv, seq, tm):
    m, d = h.shape
    mem_len = kv.shape[1]
    per_seq = seq // tm
    return pl.pallas_call(
        functools.partial(_mem_attn_kernel, heads=MEM_HEADS),
        grid=(m // tm,),
        in_specs=[pl.BlockSpec((tm, d), lambda i: (i, 0)),
                  _resident((d, d), lambda i: (0, 0)),
                  pl.BlockSpec((1, mem_len, 2 * d), lambda i: (i // per_seq, 0, 0))],
        out_specs=pl.BlockSpec((tm, d), lambda i: (i, 0)),
        out_shape=jax.ShapeDtypeStruct((m, d), BF16),
        compiler_params=_params(("arbitrary",)),
    )(h, wq, kv)


HALO = 8
FFN_ROW_CHUNK = 64
V7X_MXU_COLUMNS = 256
BF16_SUBLANES = 16
SLOPE_PARTS = 3


def _ffn_kernel(x_ref, wg_ref, wu_ref, cwg_ref, cwu_ref, cbg_ref, cbu_ref, wd_ref, g_ref, b_ref, o_ref,
                xb_ref, hg0_ref, hu0_ref, hg1_ref, hu1_ref, act0_ref, act1_ref, carry_g_ref, carry_u_ref, acc_ref,
                *, tm, tiles_per_seq, alpha):
    i = pl.program_id(0)
    f = pl.program_id(1)
    nf = pl.num_programs(1) - 2
    seq_start = (i % tiles_per_seq) == 0
    h_refs = ((hg0_ref, hu0_ref), (hg1_ref, hu1_ref))
    act_refs = (act0_ref, act1_ref)

    tf = wg_ref.shape[2]
    d = wd_ref.shape[1]

    def up_project(slot):
        def part(which, c0):
            w_ref, hs_ref, carry_ref = ((wg_ref, wu_ref)[which], h_refs[slot][which],
                                        (carry_g_ref, carry_u_ref)[which])
            cols = slice(c0, c0 + V7X_MXU_COLUMNS)
            hs_ref[HALO:, cols] = _dot(xb_ref[...], w_ref[0, :, cols])
            hs_ref[:HALO, cols] = jnp.where(seq_start, 0.0, carry_ref[f, :, cols])
            carry_ref[f, :, cols] = hs_ref[tm:, cols]
        return [functools.partial(part, which, c0) for which in range(2) for c0 in range(0, tf, V7X_MXU_COLUMNS)]

    def conv(hs_ref, cw_ref, cb_ref, row0, cols):
        out = cb_ref[:, cols]
        for tap in range(CONV_WIDTH):
            off = row0 + HALO - (CONV_WIDTH - 1) + tap
            out = out + hs_ref[off:off + FFN_ROW_CHUNK, cols] * cw_ref[tap:tap + 1, cols]
        return out

    def activate(slot):
        def part(c0, row0):
            cols = slice(c0, c0 + BLOCK)
            gate = conv(h_refs[slot][0], cwg_ref, cbg_ref, row0, cols)
            up = conv(h_refs[slot][1], cwu_ref, cbu_ref, row0, cols)
            act = gate * (1.0 / (1.0 + jnp.exp(-gate))) * up
            act_refs[slot][row0:row0 + FFN_ROW_CHUNK, cols] = act.astype(BF16)
        return [functools.partial(part, c0, row0) for c0 in range(0, tf, BLOCK)
                for row0 in range(0, tm, FFN_ROW_CHUNK)]

    def down_project(slot):
        def part(c0):
            cols = slice(c0, c0 + V7X_MXU_COLUMNS)
            acc_ref[:, cols] += _dot(act_refs[slot][...], wd_ref[:, cols])
        return [functools.partial(part, c0) for c0 in range(0, d, V7X_MXU_COLUMNS)]

    def run(*stages):
        order = sorted(((k + 0.5) / len(parts), s, k) for s, parts in enumerate(stages) for k in range(len(parts)))
        for _, s, k in order:
            stages[s][k]()

    @pl.when(f == 0)
    def _():
        @pl.when(i == 0)
        def _():
            carry_g_ref[...] = jnp.zeros(carry_g_ref.shape, F32)
            carry_u_ref[...] = jnp.zeros(carry_u_ref.shape, F32)

        xb_ref[...] = x_ref[...].astype(BF16)
        acc_ref[...] = jnp.zeros(acc_ref.shape, F32)
        run(up_project(0))

    @pl.when(f == 1)
    def _():
        run(up_project(1), activate(0))

    for parity in range(2):
        @pl.when((f > 1) & (f < nf) & (f % 2 == parity))
        def _(parity=parity):
            run(up_project(parity), activate(1 - parity), down_project(parity))

    @pl.when(f == nf)
    def _():
        run(down_project((nf - 2) % 2), activate((nf - 1) % 2))

    @pl.when(f == nf + 1)
    def _():
        run(down_project((nf - 1) % 2))
        o_ref[...] = _layer_norm(alpha * x_ref[...] + acc_ref[...], g_ref[...], b_ref[...])


def _conv_ffn_ln(x, w_up, conv_w, conv_b, w_down, g, b, alpha, seq, tm, tf):
    m, d = x.shape
    d_ff = w_down.shape[0]
    nf = d_ff // tf
    w_up = w_up.reshape(d, 2 * nf, tf).transpose(1, 0, 2)
    row = lambda i, f: (i, 0)
    fixed = lambda i, f: (0, 0)
    chunk = lambda f, lag: jnp.clip(f - lag, 0, nf - 1)
    h_buf = pltpu.VMEM((tm + HALO, tf), F32)
    act_buf = pltpu.VMEM((tm, tf), BF16)
    return pl.pallas_call(
        functools.partial(_ffn_kernel, tm=tm, tiles_per_seq=seq // tm, alpha=alpha),
        grid=(m // tm, nf + 2),
        in_specs=[pl.BlockSpec((tm, d), row),
                  pl.BlockSpec((1, d, tf), lambda i, f: (chunk(f, 0), 0, 0)),
                  pl.BlockSpec((1, d, tf), lambda i, f: (nf + chunk(f, 0), 0, 0)),
                  pl.BlockSpec((CONV_WIDTH, tf), lambda i, f: (0, chunk(f, 1))),
                  pl.BlockSpec((CONV_WIDTH, tf), lambda i, f: (0, nf + chunk(f, 1))),
                  pl.BlockSpec((1, tf), lambda i, f: (0, chunk(f, 1))),
                  pl.BlockSpec((1, tf), lambda i, f: (0, nf + chunk(f, 1))),
                  pl.BlockSpec((tf, d), lambda i, f: (chunk(f, 2), 0)),
                  pl.BlockSpec((1, d), fixed), pl.BlockSpec((1, d), fixed)],
        out_specs=pl.BlockSpec((tm, d), row),
        out_shape=jax.ShapeDtypeStruct((m, d), F32),
        scratch_shapes=[pltpu.VMEM((tm, d), BF16), h_buf, h_buf, h_buf, h_buf, act_buf, act_buf,
                        pltpu.VMEM((nf, HALO, tf), F32), pltpu.VMEM((nf, HALO, tf), F32),
                        pltpu.VMEM((tm, d), F32)],
        compiler_params=_params(("arbitrary", "arbitrary")),
    )(x, w_up, w_up, conv_w, conv_w, conv_b, conv_b, w_down, g, b)


def _alibi_slopes(n):
    return (2.0 ** (-8.0 * np.arange(1, n + 1) / n)).astype(np.float32)


def kernel(x, mem, w_in, w_mix_out, lambda_q1, lambda_k1, lambda_q2, lambda_k2, g_diff, g_dil, ln1_g, ln1_b,
           w_mem_q, w_mem_kv, w_mem_o, ln2_g, ln2_b, w_up, conv_w, conv_b, w_down, ln3_g, ln3_b):
    batch, seq, d = x.shape
    depth = w_in.shape[0]
    mem_len = mem.shape[1]
    diff_heads = dil_heads = d // 256
    alpha = (2 * depth) ** 0.25
    slopes = _alibi_slopes(diff_heads + dil_heads)
    log2e = math.log2(math.e)
    slopes_diff = _split_slopes(slopes[0::2])
    slopes_dil = jnp.asarray(slopes[1::2].astype(np.float64) * log2e, F32)
    row = lambda a: a.reshape(1, -1)

    group = diff_heads * BLOCK
    col_scale = np.ones((1, 6 * group), np.float32)
    col_scale[:, :group] = DIFF_HALF_DIM ** -0.5 * log2e
    col_scale[:, 3 * group:4 * group] = BLOCK ** -0.5 * log2e
    col_scale = jnp.asarray(col_scale)
    kv_scale = jnp.ones((1, 2 * d), F32)

    h = x.reshape(batch * seq, d)
    mem2 = mem.reshape(batch * mem_len, d)
    for l in range(depth):
        lam_init = 0.8 - 0.6 * math.exp(-0.3 * l)
        proj = _project(h, w_in[l].astype(BF16), col_scale, tm=1024, tn=1024).reshape(batch, seq, -1)
        o_diff = _diff_attention(proj, slopes_diff, row(lambda_q1[l]), row(lambda_k1[l]), row(lambda_q2[l]),
                                 row(lambda_k2[l]), row(g_diff[l]), lam_init, diff_heads, t=512)
        o_dil = _dilated_attention(proj, slopes_dil, row(g_dil[l]), dil_heads, first_blk=3 * diff_heads)
        h = _outproj_ln([o_diff.reshape(batch * seq, -1), o_dil.reshape(batch * seq, -1)],
                        w_mix_out[l].astype(BF16), h, row(ln1_g[l]), row(ln1_b[l]), alpha, tm=512)

        kv = _project(mem2, w_mem_kv[l].astype(BF16), kv_scale, tm=1024, tn=1024).reshape(batch, mem_len, 2 * d)
        o_mem = _mem_attention(h, w_mem_q[l].astype(BF16), kv, seq, tm=512)
        h = _outproj_ln([o_mem], w_mem_o[l].astype(BF16), h, row(ln2_g[l]), row(ln2_b[l]), alpha, tm=512)

        h = _conv_ffn_ln(h, w_up[l].astype(BF16), conv_w[l], row(conv_b[l]), w_down[l].astype(BF16),
                         row(ln3_g[l]), row(ln3_b[l]), alpha, seq, tm=512, tf=512)
    return h.reshape(batch, seq, d)
```

```python
import functools
import math

import numpy as np
import jax
import jax.numpy as jnp
from jax import lax
from jax.experimental import pallas as pl
from jax.experimental.pallas import tpu as pltpu

F32 = jnp.float32
BF16 = jnp.bfloat16

LN_EPS = 1e-5
RMS_EPS = 1e-5
BLOCK = 128
DIFF_HALF_DIM = 64
DIL_CONFIGS = ((128, 1), (512, 4), (2048, 16))
TILES_PER_BODY = 8
MEM_HEADS = 4
CONV_WIDTH = 3
NEG_INF = float("-inf")

V7X_VMEM_LIMIT_BYTES = 56 * 1024 * 1024
V7X_VMEM_LIMIT_LARGE_BYTES = 60 * 1024 * 1024


def _params(semantics, vmem_limit_bytes=V7X_VMEM_LIMIT_BYTES):
    return pltpu.CompilerParams(dimension_semantics=semantics, vmem_limit_bytes=vmem_limit_bytes)


def _resident(shape, index_map):
    return pl.BlockSpec(shape, index_map, pipeline_mode=pl.Buffered(1))


def _layer_norm(z, g, b):
    mu = jnp.mean(z, axis=-1, keepdims=True)
    zc = z - mu
    var = jnp.mean(zc * zc, axis=-1, keepdims=True)
    return zc * lax.rsqrt(var + LN_EPS) * g + b


def _rms_norm(o, g):
    return o * lax.rsqrt(jnp.mean(o * o, axis=-1, keepdims=True) + RMS_EPS) * g


def _dot(a, b):
    return jnp.dot(a, b, preferred_element_type=F32)


def _dot_nt(a, b):
    return lax.dot_general(a, b, (((1,), (1,)), ((), ())), preferred_element_type=F32)


def _proj_kernel(x_ref, w_ref, cs_ref, o_ref, xb_ref):
    @pl.when(pl.program_id(1) == 0)
    def _():
        xb_ref[...] = x_ref[...].astype(BF16)

    o_ref[...] = (_dot(xb_ref[...], w_ref[...]) * cs_ref[...]).astype(o_ref.dtype)


def _project(x, w, col_scale, tm, tn):
    m, k = x.shape
    n = w.shape[1]
    tm = min(tm, m)
    return pl.pallas_call(
        _proj_kernel,
        grid=(m // tm, n // tn),
        in_specs=[pl.BlockSpec((tm, k), lambda i, j: (i, 0)),
                  pl.BlockSpec((k, tn), lambda i, j: (0, j)),
                  pl.BlockSpec((1, tn), lambda i, j: (0, j))],
        out_specs=pl.BlockSpec((tm, tn), lambda i, j: (i, j)),
        out_shape=jax.ShapeDtypeStruct((m, n), BF16),
        scratch_shapes=[pltpu.VMEM((tm, k), BF16)],
        compiler_params=_params(("arbitrary", "arbitrary")),
    )(x, w, col_scale)


def _diff_attn_kernel(slope_ref, q_ref, k_ref, v_ref, lq1_ref, lk1_ref, lq2_ref, lk2_ref, g_ref, o_ref,
                      vt_ref, *, t, lam_init):
    h = pl.program_id(1)
    slope = slope_ref[h, SLOPE_PARTS]
    n_tiles = vt_ref.shape[0]
    for j in range(n_tiles):
        vt_ref[j, :BLOCK, :] = v_ref[0, j * t:(j + 1) * t, :].astype(F32).T.astype(BF16)
        vt_ref[j, BLOCK:, :] = jnp.ones((vt_ref.shape[1] - BLOCK, t), BF16)

    lam = (jnp.exp(jnp.sum(lq1_ref[...] * lk1_ref[...], axis=-1, keepdims=True))
           - jnp.exp(jnp.sum(lq2_ref[...] * lk2_ref[...], axis=-1, keepdims=True)) + lam_init)
    lane = lax.broadcasted_iota(jnp.int32, (t, BLOCK), 1)
    key_index = lax.broadcasted_iota(jnp.int32, (t, BLOCK), 0)
    low, high = key_index & (V7X_MXU_COLUMNS - 1), key_index >> 8
    assert t <= 2 * V7X_MXU_COLUMNS and V7X_MXU_COLUMNS == 1 << 8
    k_extra = jnp.where(lane < SLOPE_PARTS, low, jnp.where(lane < 2 * SLOPE_PARTS, high, 0)).astype(F32).astype(BF16)
    q_extra = jnp.zeros((2 * t, BLOCK), F32)
    lane2 = lax.broadcasted_iota(jnp.int32, (2 * t, BLOCK), 1)
    for i in range(SLOPE_PARTS):
        q_extra = jnp.where(lane2 == i, slope_ref[h, i], q_extra)
        q_extra = jnp.where(lane2 == SLOPE_PARTS + i, slope_ref[h, i] * float(V7X_MXU_COLUMNS), q_extra)
    q_extra = q_extra.astype(BF16)
    key = lax.broadcasted_iota(jnp.int32, (t, 2 * t), 0)
    query = lax.broadcasted_iota(jnp.int32, (t, 2 * t), 1)
    causal = key <= jnp.where(query >= t, query - t, query)

    def queries(qi):
        q = q_ref[0, qi * t:(qi + 1) * t, :]
        zero = jnp.zeros_like(q)
        qq = jnp.concatenate([jnp.where(lane < DIFF_HALF_DIM, q, zero),
                              jnp.where(lane >= DIFF_HALF_DIM, q, zero)], axis=0)
        return jnp.concatenate([qq, q_extra], axis=1)

    order = sorted(range(n_tiles), key=lambda qi: -qi)
    chains = [[], []]
    for qi in order:
        min(chains, key=len).extend((qi, j) for j in range(qi + 1))
    state = {}
    for step in range(max(len(c) for c in chains)):
        live = [c[step] for c in chains if step < len(c)]
        for qi, j in live:
            if j == 0:
                state[qi] = dict(qq=queries(qi))
        s = {}
        for qi, j in live:
            k = jnp.concatenate([k_ref[0, j * t:(j + 1) * t, :], k_extra], axis=1)
            s[qi] = _dot_nt(k, state[qi]["qq"])
            if j == qi:
                s[qi] = jnp.where(causal, s[qi], NEG_INF)
        p = {}
        for qi, j in live:
            st = state[qi]
            shift = slope * float((j - qi) * t)
            m_tile = jnp.max(s[qi], axis=0, keepdims=True) + shift
            st["m_old"], st["m"] = st.get("m"), (m_tile if j == 0 else jnp.maximum(st["m"], m_tile))
            p[qi] = jnp.exp2(s[qi] - (st["m"] - shift)).astype(BF16)
        for qi, j in live:
            st = state[qi]
            acc_tile = _dot(vt_ref[j], p[qi])
            st["acc"] = acc_tile if j == 0 else jnp.exp2(st["m_old"] - st["m"]) * st["acc"] + acc_tile
        for qi, j in live:
            if j == qi:
                acc = state.pop(qi)["acc"]
                o = acc[:BLOCK] / acc[BLOCK:BLOCK + 1]
                o = (o[:, :t] - lam * o[:, t:]).T
                o_ref[0, qi * t:(qi + 1) * t, :] = (_rms_norm(o, g_ref[...]) * (1.0 - lam_init)).astype(o_ref.dtype)


def _diff_attention(proj, slopes, lq1, lk1, lq2, lk2, g, lam_init, heads, t):
    b, s, _ = proj.shape
    q_blk, k_blk, v_blk = 0, heads, 2 * heads
    vec = lambda n: pl.BlockSpec((1, n), lambda bi, hi: (0, 0))
    slab = lambda blk: pl.BlockSpec((1, s, BLOCK), lambda bi, hi: (bi, 0, blk + hi))
    return pl.pallas_call(
        functools.partial(_diff_attn_kernel, t=t, lam_init=lam_init),
        grid=(b, heads),
        in_specs=[pl.BlockSpec(memory_space=pltpu.SMEM), slab(q_blk), slab(k_blk), slab(v_blk),
                  vec(DIFF_HALF_DIM), vec(DIFF_HALF_DIM), vec(DIFF_HALF_DIM), vec(DIFF_HALF_DIM), vec(BLOCK)],
        out_specs=slab(0),
        out_shape=jax.ShapeDtypeStruct((b, s, heads * BLOCK), BF16),
        scratch_shapes=[pltpu.VMEM((s // t, BLOCK + BF16_SUBLANES, t), BF16)],
        compiler_params=_params(("arbitrary", "arbitrary")),
    )(slopes, proj, proj, proj, lq1, lk1, lq2, lk2, g)


def _split_slopes(slopes):
    x = np.asarray(slopes, np.float64) * math.log2(math.e)
    x = x.astype(np.float32)
    parts, rest = [], x.copy()
    for _ in range(SLOPE_PARTS):
        part = rest.astype(BF16).astype(np.float32)
        parts.append(part)
        rest = rest - part
    return jnp.asarray(np.stack(parts + [x], axis=1), F32)


def _dilated_attn_kernel(slope_ref, q_ref, k_ref, v_ref, g_ref, o_ref, qf, kf, vf, out_ref, lse_ref,
                         *, seq, tc):
    h = pl.program_id(1)
    slope = slope_ref[h]
    qf[...] = q_ref[0].astype(F32)
    kf[...] = k_ref[0].astype(F32)
    vf[...] = v_ref[0].astype(F32)

    qi = lax.broadcasted_iota(jnp.int32, (BLOCK, 2 * BLOCK), 0)
    kj = lax.broadcasted_iota(jnp.int32, (BLOCK, 2 * BLOCK), 1)
    rel = qi + BLOCK - kj

    def rows(start, size, dil):
        return pl.ds(start, size) if dil == 1 else pl.ds(start, size, stride=dil)

    def tiles(branch, dil, starts, bias):
        nkeys = bias.shape[1]
        q_rows = [rows(q_start, BLOCK, dil) for q_start, _ in starts]
        k_rows = [rows(k_start, nkeys, dil) for _, k_start in starts]
        ones = jnp.ones((nkeys, BLOCK), BF16)
        s = [_dot_nt(qf[qr, :].astype(BF16), kf[kr, :].astype(BF16)) + bias for qr, kr in zip(q_rows, k_rows)]
        m = [jnp.max(si, axis=-1, keepdims=True) for si in s]
        p = [jnp.exp2(si - mi).astype(BF16) for si, mi in zip(s, m)]
        v1 = [jnp.concatenate([vf[kr, :].astype(BF16), ones], axis=1) for kr in k_rows]
        r = [_dot(pi, vi) for pi, vi in zip(p, v1)]
        for qr, mi, ri in zip(q_rows, m, r):
            l = ri[:, BLOCK:]
            out_ref[branch, qr, :] = ri[:, :BLOCK] / l
            lse_ref[branch, qr, :] = mi + jnp.log2(l)

    for branch, (window, dil) in enumerate(DIL_CONFIGS):
        n_back = window // dil
        nb = seq // dil // BLOCK
        step = dil * BLOCK
        valid = (rel >= 0) & (rel <= n_back)
        bias = jnp.where(valid, (-slope * dil) * rel.astype(F32), NEG_INF)
        bias_first = bias[:, BLOCK:]

        def first_tiles(classes, branch=branch, dil=dil, bias_first=bias_first):
            tiles(branch, dil, [(c, c) for c in classes], bias_first)

        def later_tiles(c, blocks, branch=branch, dil=dil, step=step, bias=bias):
            tiles(branch, dil, [(c + n * step, c + (n - 1) * step) for n in blocks], bias)

        if nb == 1:
            def class_group(i, carry, first_tiles=first_tiles):
                first_tiles([i * TILES_PER_BODY + u for u in range(TILES_PER_BODY)])
                return carry
            lax.fori_loop(0, dil // TILES_PER_BODY, class_group, 0)
        elif nb <= TILES_PER_BODY:
            per_body = TILES_PER_BODY // nb
            def class_group(i, carry, first_tiles=first_tiles, later_tiles=later_tiles, nb=nb, per_body=per_body):
                classes = [i * per_body + u for u in range(per_body)]
                first_tiles(classes)
                for c in classes:
                    later_tiles(c, range(1, nb))
                return carry
            lax.fori_loop(0, dil // per_body, class_group, 0)
        else:
            group = 5
            assert dil == 1 and (nb - 1) % group == 0
            first_tiles([0])
            def block_group(i, carry, later_tiles=later_tiles, group=group):
                later_tiles(0, [1 + i * group + u for u in range(group)])
                return carry
            lax.fori_loop(0, (nb - 1) // group, block_group, 0)

    g = g_ref[...]
    for t in range(seq // tc):
        r = pl.ds(t * tc, tc)
        lses = [lse_ref[i, r, :] for i in range(len(DIL_CONFIGS))]
        top = functools.reduce(jnp.maximum, lses)
        ws = [jnp.exp2(lse - top) for lse in lses]
        num = sum(w * out_ref[i, r, :] for i, w in enumerate(ws))
        o_ref[0, r, :] = _rms_norm(num / sum(ws), g).astype(o_ref.dtype)


def _dilated_attention(proj, slopes, g, heads, first_blk):
    b, s, _ = proj.shape
    nbr = len(DIL_CONFIGS)
    spec = lambda blk: pl.BlockSpec((1, s, BLOCK), lambda bi, hi: (bi, 0, blk + hi))
    return pl.pallas_call(
        functools.partial(_dilated_attn_kernel, seq=s, tc=BLOCK),
        grid=(b, heads),
        in_specs=[pl.BlockSpec(memory_space=pltpu.SMEM),
                  spec(first_blk), spec(first_blk + heads), spec(first_blk + 2 * heads),
                  pl.BlockSpec((1, BLOCK), lambda bi, hi: (0, 0))],
        out_specs=pl.BlockSpec((1, s, BLOCK), lambda bi, hi: (bi, 0, hi)),
        out_shape=jax.ShapeDtypeStruct((b, s, heads * BLOCK), BF16),
        scratch_shapes=[pltpu.VMEM((s, BLOCK), F32), pltpu.VMEM((s, BLOCK), F32), pltpu.VMEM((s, BLOCK), F32),
                        pltpu.VMEM((nbr, s, BLOCK), F32), pltpu.VMEM((nbr, s, BLOCK), F32)],
        compiler_params=_params(("arbitrary", "arbitrary")),
    )(slopes, proj, proj, proj, g)


def _outproj_ln_kernel(*refs, n_in, alpha):
    o_refs, w_refs = refs[:n_in], refs[n_in:2 * n_in]
    h_ref, g_ref, b_ref, out_ref = refs[2 * n_in:]
    y = _dot(o_refs[0][...], w_refs[0][...])
    for o_ref, w_ref in zip(o_refs[1:], w_refs[1:]):
        y = y + _dot(o_ref[...], w_ref[...])
    out_ref[...] = _layer_norm(alpha * h_ref[...] + y, g_ref[...], b_ref[...])


def _outproj_ln(os, w, h, g, b, alpha, tm):
    m, d = h.shape
    n_in = len(os)
    kw = w.shape[0] // n_in
    return pl.pallas_call(
        functools.partial(_outproj_ln_kernel, n_in=n_in, alpha=alpha),
        grid=(m // tm,),
        in_specs=([pl.BlockSpec((tm, kw), lambda i: (i, 0)) for _ in os]
                  + [_resident((kw, d), functools.partial(lambda i, r: (r, 0), r=r)) for r in range(n_in)]
                  + [pl.BlockSpec((tm, d), lambda i: (i, 0)),
                     pl.BlockSpec((1, d), lambda i: (0, 0)), pl.BlockSpec((1, d), lambda i: (0, 0))]),
        out_specs=pl.BlockSpec((tm, d), lambda i: (i, 0)),
        out_shape=jax.ShapeDtypeStruct((m, d), F32),
        compiler_params=_params(("arbitrary",)),
    )(*os, *([w] * n_in), h, g, b)


def _mem_attn_kernel(h_ref, wq_ref, kv_ref, o_ref, *, heads):
    d = h_ref.shape[1]
    hd = d // heads
    q = _dot(h_ref[...].astype(BF16), wq_ref[...]).astype(BF16)
    scale = hd ** -0.5
    for i in range(heads):
        k = kv_ref[0, :, i * hd:(i + 1) * hd]
        v = kv_ref[0, :, d + i * hd:d + (i + 1) * hd]
        s = _dot_nt(q[:, i * hd:(i + 1) * hd], k) * scale
        e = jnp.exp(s - jnp.max(s, axis=-1, keepdims=True))
        p = e * (1.0 / jnp.sum(e, axis=-1, keepdims=True))
        o_ref[:, i * hd:(i + 1) * hd] = _dot(p.astype(BF16), v).astype(o_ref.dtype)


def _mem_attention(h, wq, kv, seq, tm):
    m, d = h.shape
    mem_len = kv.shape[1]
    per_seq = seq // tm
    return pl.pallas_call(
        functools.partial(_mem_attn_kernel, heads=MEM_HEADS),
        grid=(m // tm,),
        in_specs=[pl.BlockSpec((tm, d), lambda i: (i, 0)),
                  _resident((d, d), lambda i: (0, 0)),
                  pl.BlockSpec((1, mem_len, 2 * d), lambda i: (i // per_seq, 0, 0))],
        out_specs=pl.BlockSpec((tm, d), lambda i: (i, 0)),
        out_shape=jax.ShapeDtypeStruct((m, d), BF16),
        compiler_params=_params(("arbitrary",)),
    )(h, wq, kv)


HALO = 8
FFN_ROW_CHUNK = 64
V7X_MXU_COLUMNS = 256
BF16_SUBLANES = 16
SLOPE_PARTS = 3


def _ffn_kernel(x_ref, wg_ref, wu_ref, cwg_ref, cwu_ref, cbg_ref, cbu_ref, wd_ref, g_ref, b_ref, o_ref,
                xb_ref, hg0_ref, hu0_ref, hg1_ref, hu1_ref, act0_ref, act1_ref, carry_g_ref, carry_u_ref,
                *, tm, tiles_per_seq, alpha):
    i = pl.program_id(0)
    f = pl.program_id(1)
    nf = pl.num_programs(1) - 2
    seq_start = (i % tiles_per_seq) == 0
    h_refs = ((hg0_ref, hu0_ref), (hg1_ref, hu1_ref))
    act_refs = (act0_ref, act1_ref)
    acc_ref = o_ref

    tf = wg_ref.shape[1]
    d = wd_ref.shape[1]

    def up_project(slot):
        def part(which, c0):
            w_ref, hs_ref, carry_ref = ((wg_ref, wu_ref)[which], h_refs[slot][which],
                                        (carry_g_ref, carry_u_ref)[which])
            cols = slice(c0, c0 + V7X_MXU_COLUMNS)
            hs_ref[HALO:, cols] = _dot(xb_ref[...], w_ref[:, cols])
            hs_ref[:HALO, cols] = jnp.where(seq_start, 0.0, carry_ref[f, :, cols])
            carry_ref[f, :, cols] = hs_ref[tm:, cols]
        return [functools.partial(part, which, c0) for which in range(2) for c0 in range(0, tf, V7X_MXU_COLUMNS)]

    def conv(hs_ref, cw_ref, cb_ref, row0, cols):
        out = cb_ref[:, cols]
        for tap in range(CONV_WIDTH):
            off = row0 + HALO - (CONV_WIDTH - 1) + tap
            out = out + hs_ref[off:off + FFN_ROW_CHUNK, cols] * cw_ref[tap:tap + 1, cols]
        return out

    def activate(slot):
        def part(c0, row0):
            cols = slice(c0, c0 + BLOCK)
            gate = conv(h_refs[slot][0], cwg_ref, cbg_ref, row0, cols)
            up = conv(h_refs[slot][1], cwu_ref, cbu_ref, row0, cols)
            act = gate * (1.0 / (1.0 + jnp.exp(-gate))) * up
            act_refs[slot][row0:row0 + FFN_ROW_CHUNK, cols] = act.astype(BF16)
        return [functools.partial(part, c0, row0) for c0 in range(0, tf, BLOCK)
                for row0 in range(0, tm, FFN_ROW_CHUNK)]

    def down_project(slot):
        def part(c0):
            cols = slice(c0, c0 + V7X_MXU_COLUMNS)
            acc_ref[:, cols] += _dot(act_refs[slot][...], wd_ref[:, cols])
        return [functools.partial(part, c0) for c0 in range(0, d, V7X_MXU_COLUMNS)]

    def run(*stages):
        order = sorted(((k + 0.5) / len(parts), s, k) for s, parts in enumerate(stages) for k in range(len(parts)))
        for _, s, k in order:
            stages[s][k]()

    @pl.when(f == 0)
    def _():
        @pl.when(i == 0)
        def _():
            carry_g_ref[...] = jnp.zeros(carry_g_ref.shape, F32)
            carry_u_ref[...] = jnp.zeros(carry_u_ref.shape, F32)

        xb_ref[...] = x_ref[...].astype(BF16)
        acc_ref[...] = jnp.zeros(acc_ref.shape, F32)
        run(up_project(0))

    @pl.when(f == 1)
    def _():
        run(up_project(1), activate(0))

    for parity in range(2):
        @pl.when((f > 1) & (f < nf) & (f % 2 == parity))
        def _(parity=parity):
            run(up_project(parity), activate(1 - parity), down_project(parity))

    @pl.when(f == nf)
    def _():
        run(down_project((nf - 2) % 2), activate((nf - 1) % 2))

    @pl.when(f == nf + 1)
    def _():
        run(down_project((nf - 1) % 2))
        o_ref[...] = _layer_norm(alpha * x_ref[...] + acc_ref[...], g_ref[...], b_ref[...])


def _conv_ffn_ln(x, w_up, conv_w, conv_b, w_down, g, b, alpha, seq, tm, tf):
    m, d = x.shape
    d_ff = w_down.shape[0]
    nf = d_ff // tf
    row = lambda i, f: (i, 0)
    fixed = lambda i, f: (0, 0)
    chunk = lambda f, lag: jnp.clip(f - lag, 0, nf - 1)
    h_buf = pltpu.VMEM((tm + HALO, tf), F32)
    act_buf = pltpu.VMEM((tm, tf), BF16)
    return pl.pallas_call(
        functools.partial(_ffn_kernel, tm=tm, tiles_per_seq=seq // tm, alpha=alpha),
        grid=(m // tm, nf + 2),
        in_specs=[pl.BlockSpec((tm, d), row, pipeline_mode=pl.Buffered(1)),
                  pl.BlockSpec((d, tf), lambda i, f: (0, chunk(f, 0))),
                  pl.BlockSpec((d, tf), lambda i, f: (0, nf + chunk(f, 0))),
                  pl.BlockSpec((CONV_WIDTH, tf), lambda i, f: (0, chunk(f, 1))),
                  pl.BlockSpec((CONV_WIDTH, tf), lambda i, f: (0, nf + chunk(f, 1))),
                  pl.BlockSpec((1, tf), lambda i, f: (0, chunk(f, 1))),
                  pl.BlockSpec((1, tf), lambda i, f: (0, nf + chunk(f, 1))),
                  pl.BlockSpec((tf, d), lambda i, f: (chunk(f, 2), 0)),
                  pl.BlockSpec((1, d), fixed), pl.BlockSpec((1, d), fixed)],
        out_specs=pl.BlockSpec((tm, d), row),
        out_shape=jax.ShapeDtypeStruct((m, d), F32),
        scratch_shapes=[pltpu.VMEM((tm, d), BF16), h_buf, h_buf, h_buf, h_buf, act_buf, act_buf,
                        pltpu.VMEM((nf, HALO, tf), F32), pltpu.VMEM((nf, HALO, tf), F32)],
        compiler_params=_params(("arbitrary", "arbitrary"), V7X_VMEM_LIMIT_LARGE_BYTES),
    )(x, w_up, w_up, conv_w, conv_w, conv_b, conv_b, w_down, g, b)


def _alibi_slopes(n):
    return (2.0 ** (-8.0 * np.arange(1, n + 1) / n)).astype(np.float32)


def kernel(x, mem, w_in, w_mix_out, lambda_q1, lambda_k1, lambda_q2, lambda_k2, g_diff, g_dil, ln1_g, ln1_b,
           w_mem_q, w_mem_kv, w_mem_o, ln2_g, ln2_b, w_up, conv_w, conv_b, w_down, ln3_g, ln3_b):
    batch, seq, d = x.shape
    depth = w_in.shape[0]
    mem_len = mem.shape[1]
    diff_heads = dil_heads = d // 256
    alpha = (2 * depth) ** 0.25
    slopes = _alibi_slopes(diff_heads + dil_heads)
    log2e = math.log2(math.e)
    slopes_diff = _split_slopes(slopes[0::2])
    slopes_dil = jnp.asarray(slopes[1::2].astype(np.float64) * log2e, F32)
    row = lambda a: a.reshape(1, -1)

    group = diff_heads * BLOCK
    col_scale = np.ones((1, 6 * group), np.float32)
    col_scale[:, :group] = DIFF_HALF_DIM ** -0.5 * log2e
    col_scale[:, 3 * group:4 * group] = BLOCK ** -0.5 * log2e
    col_scale = jnp.asarray(col_scale)
    kv_scale = jnp.ones((1, 2 * d), F32)

    h = x.reshape(batch * seq, d)
    mem2 = mem.reshape(batch * mem_len, d)
    for l in range(depth):
        lam_init = 0.8 - 0.6 * math.exp(-0.3 * l)
        proj = _project(h, w_in[l].astype(BF16), col_scale, tm=1024, tn=1024).reshape(batch, seq, -1)
        o_diff = _diff_attention(proj, slopes_diff, row(lambda_q1[l]), row(lambda_k1[l]), row(lambda_q2[l]),
                                 row(lambda_k2[l]), row(g_diff[l]), lam_init, diff_heads, t=512)
        o_dil = _dilated_attention(proj, slopes_dil, row(g_dil[l]), dil_heads, first_blk=3 * diff_heads)
        h = _outproj_ln([o_diff.reshape(batch * seq, -1), o_dil.reshape(batch * seq, -1)],
                        w_mix_out[l].astype(BF16), h, row(ln1_g[l]), row(ln1_b[l]), alpha, tm=512)

        kv = _project(mem2, w_mem_kv[l].astype(BF16), kv_scale, tm=1024, tn=1024).reshape(batch, mem_len, 2 * d)
        o_mem = _mem_attention(h, w_mem_q[l].astype(BF16), kv, seq, tm=512)
        h = _outproj_ln([o_mem], w_mem_o[l].astype(BF16), h, row(ln2_g[l]), row(ln2_b[l]), alpha, tm=512)

        h = _conv_ffn_ln(h, w_up[l].astype(BF16), conv_w[l], row(conv_b[l]), w_down[l].astype(BF16),
                         row(ln3_g[l]), row(ln3_b[l]), alpha, seq, tm=1024, tf=512)
    return h.reshape(batch, seq, d)
```

```python
import functools
import math

import numpy as np
import jax
import jax.numpy as jnp
from jax import lax
from jax.experimental import pallas as pl
from jax.experimental.pallas import tpu as pltpu

F32 = jnp.float32
BF16 = jnp.bfloat16

LN_EPS = 1e-5
RMS_EPS = 1e-5
BLOCK = 128
DIFF_HALF_DIM = 64
DIL_CONFIGS = ((128, 1), (512, 4), (2048, 16))
TILES_PER_BODY = 16
MEM_HEADS = 4
CONV_WIDTH = 3
NEG_INF = float("-inf")

V7X_VMEM_LIMIT_BYTES = 56 * 1024 * 1024


def _params(semantics, vmem_limit_bytes=V7X_VMEM_LIMIT_BYTES):
    return pltpu.CompilerParams(dimension_semantics=semantics, vmem_limit_bytes=vmem_limit_bytes)


def _resident(shape, index_map):
    return pl.BlockSpec(shape, index_map, pipeline_mode=pl.Buffered(1))


def _layer_norm(z, g, b):
    mu = jnp.mean(z, axis=-1, keepdims=True)
    zc = z - mu
    var = jnp.mean(zc * zc, axis=-1, keepdims=True)
    return zc * lax.rsqrt(var + LN_EPS) * g + b


def _rms_norm(o, g):
    return o * lax.rsqrt(jnp.mean(o * o, axis=-1, keepdims=True) + RMS_EPS) * g


def _dot(a, b):
    return jnp.dot(a, b, preferred_element_type=F32)


def _dot_nt(a, b):
    return lax.dot_general(a, b, (((1,), (1,)), ((), ())), preferred_element_type=F32)


def _proj_kernel(x_ref, w_ref, cs_ref, o_ref, xb_ref):
    @pl.when(pl.program_id(1) == 0)
    def _():
        xb_ref[...] = x_ref[...].astype(BF16)

    o_ref[...] = (_dot(xb_ref[...], w_ref[...]) * cs_ref[...]).astype(o_ref.dtype)


def _project(x, w, col_scale, tm, tn):
    m, k = x.shape
    n = w.shape[1]
    tm = min(tm, m)
    return pl.pallas_call(
        _proj_kernel,
        grid=(m // tm, n // tn),
        in_specs=[pl.BlockSpec((tm, k), lambda i, j: (i, 0)),
                  pl.BlockSpec((k, tn), lambda i, j: (0, j)),
                  pl.BlockSpec((1, tn), lambda i, j: (0, j))],
        out_specs=pl.BlockSpec((tm, tn), lambda i, j: (i, j)),
        out_shape=jax.ShapeDtypeStruct((m, n), BF16),
        scratch_shapes=[pltpu.VMEM((tm, k), BF16)],
        compiler_params=_params(("arbitrary", "arbitrary")),
    )(x, w, col_scale)


def _diff_attn_kernel(slope_ref, q_ref, k_ref, v_ref, lq1_ref, lk1_ref, lq2_ref, lk2_ref, g_ref, o_ref,
                      vt_ref, *, t, lam_init):
    h = pl.program_id(1)
    slope = slope_ref[h, SLOPE_PARTS]
    n_tiles = vt_ref.shape[0]
    for j in range(n_tiles):
        vt_ref[j, :BLOCK, :] = v_ref[0, j * t:(j + 1) * t, :].astype(F32).T.astype(BF16)
        vt_ref[j, BLOCK:, :] = jnp.ones((vt_ref.shape[1] - BLOCK, t), BF16)

    lam = (jnp.exp(jnp.sum(lq1_ref[...] * lk1_ref[...], axis=-1, keepdims=True))
           - jnp.exp(jnp.sum(lq2_ref[...] * lk2_ref[...], axis=-1, keepdims=True)) + lam_init)
    lane = lax.broadcasted_iota(jnp.int32, (t, BLOCK), 1)
    key_index = lax.broadcasted_iota(jnp.int32, (t, BLOCK), 0)
    low, high = key_index & (V7X_MXU_COLUMNS - 1), key_index >> 8
    assert t <= 2 * V7X_MXU_COLUMNS and V7X_MXU_COLUMNS == 1 << 8
    k_extra = jnp.where(lane < SLOPE_PARTS, low, jnp.where(lane < 2 * SLOPE_PARTS, high, 0)).astype(F32).astype(BF16)
    q_extra = jnp.zeros((2 * t, BLOCK), F32)
    lane2 = lax.broadcasted_iota(jnp.int32, (2 * t, BLOCK), 1)
    for i in range(SLOPE_PARTS):
        q_extra = jnp.where(lane2 == i, slope_ref[h, i], q_extra)
        q_extra = jnp.where(lane2 == SLOPE_PARTS + i, slope_ref[h, i] * float(V7X_MXU_COLUMNS), q_extra)
    q_extra = q_extra.astype(BF16)
    key = lax.broadcasted_iota(jnp.int32, (t, 2 * t), 0)
    query = lax.broadcasted_iota(jnp.int32, (t, 2 * t), 1)
    causal = key <= jnp.where(query >= t, query - t, query)

    def queries(qi):
        q = q_ref[0, qi * t:(qi + 1) * t, :]
        zero = jnp.zeros_like(q)
        qq = jnp.concatenate([jnp.where(lane < DIFF_HALF_DIM, q, zero),
                              jnp.where(lane >= DIFF_HALF_DIM, q, zero)], axis=0)
        return jnp.concatenate([qq, q_extra], axis=1)

    order = sorted(range(n_tiles), key=lambda qi: -qi)
    chains = [[], []]
    for qi in order:
        min(chains, key=len).extend((qi, j) for j in range(qi + 1))
    state = {}
    for step in range(max(len(c) for c in chains)):
        live = [c[step] for c in chains if step < len(c)]
        for qi, j in live:
            if j == 0:
                state[qi] = dict(qq=queries(qi))
        s = {}
        for qi, j in live:
            k = jnp.concatenate([k_ref[0, j * t:(j + 1) * t, :], k_extra], axis=1)
            s[qi] = _dot_nt(k, state[qi]["qq"])
            if j == qi:
                s[qi] = jnp.where(causal, s[qi], NEG_INF)
        p = {}
        for qi, j in live:
            st = state[qi]
            shift = slope * float((j - qi) * t)
            m_tile = jnp.max(s[qi], axis=0, keepdims=True) + shift
            st["m_old"], st["m"] = st.get("m"), (m_tile if j == 0 else jnp.maximum(st["m"], m_tile))
            p[qi] = jnp.exp2(s[qi] - (st["m"] - shift)).astype(BF16)
        for qi, j in live:
            st = state[qi]
            acc_tile = _dot(vt_ref[j], p[qi])
            st["acc"] = acc_tile if j == 0 else jnp.exp2(st["m_old"] - st["m"]) * st["acc"] + acc_tile
        for qi, j in live:
            if j == qi:
                acc = state.pop(qi)["acc"]
                o = acc[:BLOCK] / acc[BLOCK:BLOCK + 1]
                o = (o[:, :t] - lam * o[:, t:]).T
                o_ref[0, qi * t:(qi + 1) * t, :] = (_rms_norm(o, g_ref[...]) * (1.0 - lam_init)).astype(o_ref.dtype)


def _diff_attention(proj, slopes, lq1, lk1, lq2, lk2, g, lam_init, heads, t):
    b, s, _ = proj.shape
    q_blk, k_blk, v_blk = 0, heads, 2 * heads
    vec = lambda n: pl.BlockSpec((1, n), lambda bi, hi: (0, 0))
    slab = lambda blk: pl.BlockSpec((1, s, BLOCK), lambda bi, hi: (bi, 0, blk + hi))
    return pl.pallas_call(
        functools.partial(_diff_attn_kernel, t=t, lam_init=lam_init),
        grid=(b, heads),
        in_specs=[pl.BlockSpec(memory_space=pltpu.SMEM), slab(q_blk), slab(k_blk), slab(v_blk),
                  vec(DIFF_HALF_DIM), vec(DIFF_HALF_DIM), vec(DIFF_HALF_DIM), vec(DIFF_HALF_DIM), vec(BLOCK)],
        out_specs=slab(0),
        out_shape=jax.ShapeDtypeStruct((b, s, heads * BLOCK), BF16),
        scratch_shapes=[pltpu.VMEM((s // t, BLOCK + BF16_SUBLANES, t), BF16)],
        compiler_params=_params(("arbitrary", "arbitrary")),
    )(slopes, proj, proj, proj, lq1, lk1, lq2, lk2, g)


def _split_slopes(slopes):
    x = np.asarray(slopes, np.float64) * math.log2(math.e)
    x = x.astype(np.float32)
    parts, rest = [], x.copy()
    for _ in range(SLOPE_PARTS):
        part = rest.astype(BF16).astype(np.float32)
        parts.append(part)
        rest = rest - part
    return jnp.asarray(np.stack(parts + [x], axis=1), F32)


def _dilated_attn_kernel(slope_ref, q_ref, k_ref, v_ref, g_ref, o_ref, qf, kf, vf, out_ref, lse_ref,
                         *, seq, tc):
    h = pl.program_id(1)
    slope = slope_ref[h]
    qf[...] = q_ref[0].astype(F32)
    kf[...] = k_ref[0].astype(F32)
    vf[...] = v_ref[0].astype(F32)

    qi = lax.broadcasted_iota(jnp.int32, (BLOCK, 2 * BLOCK), 0)
    kj = lax.broadcasted_iota(jnp.int32, (BLOCK, 2 * BLOCK), 1)
    rel = qi + BLOCK - kj

    def rows(start, size, dil):
        return pl.ds(start, size) if dil == 1 else pl.ds(start, size, stride=dil)

    def tiles(branch, dil, starts, bias):
        nkeys = bias.shape[1]
        q_rows = [rows(q_start, BLOCK, dil) for q_start, _ in starts]
        k_rows = [rows(k_start, nkeys, dil) for _, k_start in starts]
        ones = jnp.ones((nkeys, BLOCK), BF16)
        s = [_dot_nt(qf[qr, :].astype(BF16), kf[kr, :].astype(BF16)) + bias for qr, kr in zip(q_rows, k_rows)]
        m = [jnp.max(si, axis=-1, keepdims=True) for si in s]
        p = [jnp.exp2(si - mi).astype(BF16) for si, mi in zip(s, m)]
        v1 = [jnp.concatenate([vf[kr, :].astype(BF16), ones], axis=1) for kr in k_rows]
        r = [_dot(pi, vi) for pi, vi in zip(p, v1)]
        for qr, mi, ri in zip(q_rows, m, r):
            l = ri[:, BLOCK:]
            out_ref[branch, qr, :] = ri[:, :BLOCK] / l
            lse_ref[branch, qr, :] = mi + jnp.log2(l)

    for branch, (window, dil) in enumerate(DIL_CONFIGS):
        n_back = window // dil
        nb = seq // dil // BLOCK
        step = dil * BLOCK
        valid = (rel >= 0) & (rel <= n_back)
        bias = jnp.where(valid, (-slope * dil) * rel.astype(F32), NEG_INF)
        bias_first = bias[:, BLOCK:]

        def first_tiles(classes, branch=branch, dil=dil, bias_first=bias_first):
            tiles(branch, dil, [(c, c) for c in classes], bias_first)

        def later_tiles(c, blocks, branch=branch, dil=dil, step=step, bias=bias):
            tiles(branch, dil, [(c + n * step, c + (n - 1) * step) for n in blocks], bias)

        if nb == 1:
            def class_group(i, carry, first_tiles=first_tiles):
                first_tiles([i * TILES_PER_BODY + u for u in range(TILES_PER_BODY)])
                return carry
            lax.fori_loop(0, dil // TILES_PER_BODY, class_group, 0)
        elif nb <= TILES_PER_BODY:
            per_body = TILES_PER_BODY // nb
            def class_group(i, carry, first_tiles=first_tiles, later_tiles=later_tiles, nb=nb, per_body=per_body):
                classes = [i * per_body + u for u in range(per_body)]
                first_tiles(classes)
                for c in classes:
                    later_tiles(c, range(1, nb))
                return carry
            lax.fori_loop(0, dil // per_body, class_group, 0)
        else:
            group = TILES_PER_BODY - 1
            assert dil == 1 and (nb - 1) % group == 0
            first_tiles([0])
            def block_group(i, carry, later_tiles=later_tiles, group=group):
                later_tiles(0, [1 + i * group + u for u in range(group)])
                return carry
            lax.fori_loop(0, (nb - 1) // group, block_group, 0)

    g = g_ref[...]
    for t in range(seq // tc):
        r = pl.ds(t * tc, tc)
        lses = [lse_ref[i, r, :] for i in range(len(DIL_CONFIGS))]
        top = functools.reduce(jnp.maximum, lses)
        ws = [jnp.exp2(lse - top) for lse in lses]
        num = sum(w * out_ref[i, r, :] for i, w in enumerate(ws))
        o_ref[0, r, :] = _rms_norm(num / sum(ws), g).astype(o_ref.dtype)


def _dilated_attention(proj, slopes, g, heads, first_blk):
    b, s, _ = proj.shape
    nbr = len(DIL_CONFIGS)
    spec = lambda blk: pl.BlockSpec((1, s, BLOCK), lambda bi, hi: (bi, 0, blk + hi))
    return pl.pallas_call(
        functools.partial(_dilated_attn_kernel, seq=s, tc=BLOCK),
        grid=(b, heads),
        in_specs=[pl.BlockSpec(memory_space=pltpu.SMEM),
                  spec(first_blk), spec(first_blk + heads), spec(first_blk + 2 * heads),
                  pl.BlockSpec((1, BLOCK), lambda bi, hi: (0, 0))],
        out_specs=pl.BlockSpec((1, s, BLOCK), lambda bi, hi: (bi, 0, hi)),
        out_shape=jax.ShapeDtypeStruct((b, s, heads * BLOCK), BF16),
        scratch_shapes=[pltpu.VMEM((s, BLOCK), F32), pltpu.VMEM((s, BLOCK), F32), pltpu.VMEM((s, BLOCK), F32),
                        pltpu.VMEM((nbr, s, BLOCK), F32), pltpu.VMEM((nbr, s, BLOCK), F32)],
        compiler_params=_params(("arbitrary", "arbitrary")),
    )(slopes, proj, proj, proj, g)


def _outproj_ln_kernel(*refs, n_in, alpha):
    o_refs, w_refs = refs[:n_in], refs[n_in:2 * n_in]
    h_ref, g_ref, b_ref, out_ref = refs[2 * n_in:]
    y = _dot(o_refs[0][...], w_refs[0][...])
    for o_ref, w_ref in zip(o_refs[1:], w_refs[1:]):
        y = y + _dot(o_ref[...], w_ref[...])
    out_ref[...] = _layer_norm(alpha * h_ref[...] + y, g_ref[...], b_ref[...])


def _outproj_ln(os, w, h, g, b, alpha, tm):
    m, d = h.shape
    n_in = len(os)
    kw = w.shape[0] // n_in
    return pl.pallas_call(
        functools.partial(_outproj_ln_kernel, n_in=n_in, alpha=alpha),
        grid=(m // tm,),
        in_specs=([pl.BlockSpec((tm, kw), lambda i: (i, 0)) for _ in os]
                  + [_resident((kw, d), functools.partial(lambda i, r: (r, 0), r=r)) for r in range(n_in)]
                  + [pl.BlockSpec((tm, d), lambda i: (i, 0)),
                     pl.BlockSpec((1, d), lambda i: (0, 0)), pl.BlockSpec((1, d), lambda i: (0, 0))]),
        out_specs=pl.BlockSpec((tm, d), lambda i: (i, 0)),
        out_shape=jax.ShapeDtypeStruct((m, d), F32),
        compiler_params=_params(("arbitrary",)),
    )(*os, *([w] * n_in), h, g, b)


def _mem_attn_kernel(h_ref, wq_ref, kv_ref, o_ref, *, heads):
    d = h_ref.shape[1]
    hd = d // heads
    q = _dot(h_ref[...].astype(BF16), wq_ref[...]).astype(BF16)
    scale = hd ** -0.5
    for i in range(heads):
        k = kv_ref[0, :, i * hd:(i + 1) * hd]
        v = kv_ref[0, :, d + i * hd:d + (i + 1) * hd]
        s = _dot_nt(q[:, i * hd:(i + 1) * hd], k) * scale
        e = jnp.exp(s - jnp.max(s, axis=-1, keepdims=True))
        p = e * (1.0 / jnp.sum(e, axis=-1, keepdims=True))
        o_ref[:, i * hd:(i + 1) * hd] = _dot(p.astype(BF16), v).astype(o_ref.dtype)


def _mem_attention(h, wq, kv, seq, tm):
    m, d = h.shape
    mem_len = kv.shape[1]
    per_seq = seq // tm
    return pl.pallas_call(
        functools.partial(_mem_attn_kernel, heads=MEM_HEADS),
        grid=(m // tm,),
        in_specs=[pl.BlockSpec((tm, d), lambda i: (i, 0)),
                  _resident((d, d), lambda i: (0, 0)),
                  pl.BlockSpec((1, mem_len, 2 * d), lambda i: (i // per_seq, 0, 0))],
        out_specs=pl.BlockSpec((tm, d), lambda i: (i, 0)),
        out_shape=jax.ShapeDtypeStruct((m, d), BF16),
        compiler_params=_params(("arbitrary",)),
    )(h, wq, kv)


HALO = 8
FFN_ROW_CHUNK = 64
V7X_MXU_COLUMNS = 256
BF16_SUBLANES = 16
SLOPE_PARTS = 3


def _ffn_kernel(x_ref, wg_ref, wu_ref, cwg_ref, cwu_ref, cbg_ref, cbu_ref, wd_ref, g_ref, b_ref, o_ref,
                xb_ref, hg0_ref, hu0_ref, hg1_ref, hu1_ref, act0_ref, act1_ref, carry_g_ref, carry_u_ref,
                *, tm, tiles_per_seq, alpha):
    i = pl.program_id(0)
    f = pl.program_id(1)
    nf = pl.num_programs(1) - 2
    seq_start = (i % tiles_per_seq) == 0
    h_refs = ((hg0_ref, hu0_ref), (hg1_ref, hu1_ref))
    act_refs = (act0_ref, act1_ref)
    acc_ref = o_ref

    tf = wg_ref.shape[1]
    d = wd_ref.shape[1]

    def up_project(slot):
        def part(which, c0):
            w_ref, hs_ref, carry_ref = ((wg_ref, wu_ref)[which], h_refs[slot][which],
                                        (carry_g_ref, carry_u_ref)[which])
            cols = slice(c0, c0 + V7X_MXU_COLUMNS)
            hs_ref[HALO:, cols] = _dot(xb_ref[...], w_ref[:, cols])
            hs_ref[:HALO, cols] = jnp.where(seq_start, 0.0, carry_ref[f, :, cols])
            carry_ref[f, :, cols] = hs_ref[tm:, cols]
        return [functools.partial(part, which, c0) for which in range(2) for c0 in range(0, tf, V7X_MXU_COLUMNS)]

    def conv(hs_ref, cw_ref, cb_ref, row0, cols):
        out = cb_ref[:, cols]
        for tap in range(CONV_WIDTH):
            off = row0 + HALO - (CONV_WIDTH - 1) + tap
            out = out + hs_ref[off:off + FFN_ROW_CHUNK, cols] * cw_ref[tap:tap + 1, cols]
        return out

    def activate(slot):
        def part(c0, row0):
            cols = slice(c0, c0 + BLOCK)
            gate = conv(h_refs[slot][0], cwg_ref, cbg_ref, row0, cols)
            up = conv(h_refs[slot][1], cwu_ref, cbu_ref, row0, cols)
            act = gate * (1.0 / (1.0 + jnp.exp(-gate))) * up
            act_refs[slot][row0:row0 + FFN_ROW_CHUNK, cols] = act.astype(BF16)
        return [functools.partial(part, c0, row0) for c0 in range(0, tf, BLOCK)
                for row0 in range(0, tm, FFN_ROW_CHUNK)]

    def down_project(slot):
        def part(c0):
            cols = slice(c0, c0 + V7X_MXU_COLUMNS)
            acc_ref[:, cols] += _dot(act_refs[slot][...], wd_ref[:, cols])
        return [functools.partial(part, c0) for c0 in range(0, d, V7X_MXU_COLUMNS)]

    def run(*stages):
        order = sorted(((k + 0.5) / len(parts), s, k) for s, parts in enumerate(stages) for k in range(len(parts)))
        for _, s, k in order:
            stages[s][k]()

    @pl.when(f == 0)
    def _():
        @pl.when(i == 0)
        def _():
            carry_g_ref[...] = jnp.zeros(carry_g_ref.shape, F32)
            carry_u_ref[...] = jnp.zeros(carry_u_ref.shape, F32)

        xb_ref[...] = x_ref[...].astype(BF16)
        acc_ref[...] = jnp.zeros(acc_ref.shape, F32)
        run(up_project(0))

    @pl.when(f == 1)
    def _():
        run(up_project(1), activate(0))

    for parity in range(2):
        @pl.when((f > 1) & (f < nf) & (f % 2 == parity))
        def _(parity=parity):
            run(up_project(parity), activate(1 - parity), down_project(parity))

    @pl.when(f == nf)
    def _():
        run(down_project((nf - 2) % 2), activate((nf - 1) % 2))

    @pl.when(f == nf + 1)
    def _():
        run(down_project((nf - 1) % 2))
        o_ref[...] = _layer_norm(alpha * x_ref[...] + acc_ref[...], g_ref[...], b_ref[...])


def _conv_ffn_ln(x, w_up, conv_w, conv_b, w_down, g, b, alpha, seq, tm, tf):
    m, d = x.shape
    d_ff = w_down.shape[0]
    nf = d_ff // tf
    row = lambda i, f: (i, 0)
    fixed = lambda i, f: (0, 0)
    chunk = lambda f, lag: jnp.clip(f - lag, 0, nf - 1)
    h_buf = pltpu.VMEM((tm + HALO, tf), F32)
    act_buf = pltpu.VMEM((tm, tf), BF16)
    return pl.pallas_call(
        functools.partial(_ffn_kernel, tm=tm, tiles_per_seq=seq // tm, alpha=alpha),
        grid=(m // tm, nf + 2),
        in_specs=[pl.BlockSpec((tm, d), row),
                  pl.BlockSpec((d, tf), lambda i, f: (0, chunk(f, 0))),
                  pl.BlockSpec((d, tf), lambda i, f: (0, nf + chunk(f, 0))),
                  pl.BlockSpec((CONV_WIDTH, tf), lambda i, f: (0, chunk(f, 1))),
                  pl.BlockSpec((CONV_WIDTH, tf), lambda i, f: (0, nf + chunk(f, 1))),
                  pl.BlockSpec((1, tf), lambda i, f: (0, chunk(f, 1))),
                  pl.BlockSpec((1, tf), lambda i, f: (0, nf + chunk(f, 1))),
                  pl.BlockSpec((tf, d), lambda i, f: (chunk(f, 2), 0)),
                  pl.BlockSpec((1, d), fixed), pl.BlockSpec((1, d), fixed)],
        out_specs=pl.BlockSpec((tm, d), row),
        out_shape=jax.ShapeDtypeStruct((m, d), F32),
        scratch_shapes=[pltpu.VMEM((tm, d), BF16), h_buf, h_buf, h_buf, h_buf, act_buf, act_buf,
                        pltpu.VMEM((nf, HALO, tf), F32), pltpu.VMEM((nf, HALO, tf), F32)],
        compiler_params=_params(("arbitrary", "arbitrary")),
    )(x, w_up, w_up, conv_w, conv_w, conv_b, conv_b, w_down, g, b)


def _alibi_slopes(n):
    return (2.0 ** (-8.0 * np.arange(1, n + 1) / n)).astype(np.float32)


def kernel(x, mem, w_in, w_mix_out, lambda_q1, lambda_k1, lambda_q2, lambda_k2, g_diff, g_dil, ln1_g, ln1_b,
           w_mem_q, w_mem_kv, w_mem_o, ln2_g, ln2_b, w_up, conv_w, conv_b, w_down, ln3_g, ln3_b):
    batch, seq, d = x.shape
    depth = w_in.shape[0]
    mem_len = mem.shape[1]
    diff_heads = dil_heads = d // 256
    alpha = (2 * depth) ** 0.25
    slopes = _alibi_slopes(diff_heads + dil_heads)
    log2e = math.log2(math.e)
    slopes_diff = _split_slopes(slopes[0::2])
    slopes_dil = jnp.asarray(slopes[1::2].astype(np.float64) * log2e, F32)
    row = lambda a: a.reshape(1, -1)

    group = diff_heads * BLOCK
    col_scale = np.ones((1, 6 * group), np.float32)
    col_scale[:, :group] = DIFF_HALF_DIM ** -0.5 * log2e
    col_scale[:, 3 * group:4 * group] = BLOCK ** -0.5 * log2e
    col_scale = jnp.asarray(col_scale)
    kv_scale = jnp.ones((1, 2 * d), F32)

    h = x.reshape(batch * seq, d)
    mem2 = mem.reshape(batch * mem_len, d)
    for l in range(depth):
        lam_init = 0.8 - 0.6 * math.exp(-0.3 * l)
        proj = _project(h, w_in[l].astype(BF16), col_scale, tm=1024, tn=1024).reshape(batch, seq, -1)
        o_diff = _diff_attention(proj, slopes_diff, row(lambda_q1[l]), row(lambda_k1[l]), row(lambda_q2[l]),
                                 row(lambda_k2[l]), row(g_diff[l]), lam_init, diff_heads, t=512)
        o_dil = _dilated_attention(proj, slopes_dil, row(g_dil[l]), dil_heads, first_blk=3 * diff_heads)
        h = _outproj_ln([o_diff.reshape(batch * seq, -1), o_dil.reshape(batch * seq, -1)],
                        w_mix_out[l].astype(BF16), h, row(ln1_g[l]), row(ln1_b[l]), alpha, tm=512)

        kv = _project(mem2, w_mem_kv[l].astype(BF16), kv_scale, tm=1024, tn=1024).reshape(batch, mem_len, 2 * d)
        o_mem = _mem_attention(h, w_mem_q[l].astype(BF16), kv, seq, tm=512)
        h = _outproj_ln([o_mem], w_mem_o[l].astype(BF16), h, row(ln2_g[l]), row(ln2_b[l]), alpha, tm=512)

        h = _conv_ffn_ln(h, w_up[l].astype(BF16), conv_w[l], row(conv_b[l]), w_down[l].astype(BF16),
                         row(ln3_g[l]), row(ln3_b[l]), alpha, seq, tm=512, tf=512)
    return h.reshape(batch, seq, d)
```

```python
import functools
import math

import numpy as np
import jax
import jax.numpy as jnp
from jax import lax
from jax.experimental import pallas as pl
from jax.experimental.pallas import tpu as pltpu

F32 = jnp.float32
BF16 = jnp.bfloat16

LN_EPS = 1e-5
RMS_EPS = 1e-5
BLOCK = 128
DIFF_HALF_DIM = 64
DIL_CONFIGS = ((128, 1), (512, 4), (2048, 16))
TILES_PER_BODY = 16
MEM_HEADS = 4
CONV_WIDTH = 3
NEG_INF = float("-inf")

V7X_VMEM_LIMIT_BYTES = 56 * 1024 * 1024


def _params(semantics, vmem_limit_bytes=V7X_VMEM_LIMIT_BYTES):
    return pltpu.CompilerParams(dimension_semantics=semantics, vmem_limit_bytes=vmem_limit_bytes)


def _resident(shape, index_map):
    return pl.BlockSpec(shape, index_map, pipeline_mode=pl.Buffered(1))


def _layer_norm(z, g, b):
    mu = jnp.mean(z, axis=-1, keepdims=True)
    zc = z - mu
    var = jnp.mean(zc * zc, axis=-1, keepdims=True)
    return zc * lax.rsqrt(var + LN_EPS) * g + b


def _rms_norm(o, g):
    return o * lax.rsqrt(jnp.mean(o * o, axis=-1, keepdims=True) + RMS_EPS) * g


def _dot(a, b):
    return jnp.dot(a, b, preferred_element_type=F32)


def _dot_nt(a, b):
    return lax.dot_general(a, b, (((1,), (1,)), ((), ())), preferred_element_type=F32)


def _proj_kernel(x_ref, w_ref, cs_ref, o_ref, xb_ref):
    @pl.when(pl.program_id(1) == 0)
    def _():
        xb_ref[...] = x_ref[...].astype(BF16)

    o_ref[...] = (_dot(xb_ref[...], w_ref[...]) * cs_ref[...]).astype(o_ref.dtype)


def _project(x, w, col_scale, tm, tn):
    m, k = x.shape
    n = w.shape[1]
    tm = min(tm, m)
    return pl.pallas_call(
        _proj_kernel,
        grid=(m // tm, n // tn),
        in_specs=[pl.BlockSpec((tm, k), lambda i, j: (i, 0)),
                  pl.BlockSpec((k, tn), lambda i, j: (0, j)),
                  pl.BlockSpec((1, tn), lambda i, j: (0, j))],
        out_specs=pl.BlockSpec((tm, tn), lambda i, j: (i, j)),
        out_shape=jax.ShapeDtypeStruct((m, n), BF16),
        scratch_shapes=[pltpu.VMEM((tm, k), BF16)],
        compiler_params=_params(("arbitrary", "arbitrary")),
    )(x, w, col_scale)


def _diff_attn_kernel(slope_ref, q_ref, k_ref, v_ref, lq1_ref, lk1_ref, lq2_ref, lk2_ref, g_ref, o_ref,
                      vt_ref, *, t, lam_init):
    h = pl.program_id(1)
    slope = slope_ref[h, SLOPE_PARTS]
    n_tiles = vt_ref.shape[0]
    for j in range(n_tiles):
        vt_ref[j, :BLOCK, :] = v_ref[0, j * t:(j + 1) * t, :].astype(F32).T.astype(BF16)
        vt_ref[j, BLOCK:, :] = jnp.ones((vt_ref.shape[1] - BLOCK, t), BF16)

    lam = (jnp.exp(jnp.sum(lq1_ref[...] * lk1_ref[...], axis=-1, keepdims=True))
           - jnp.exp(jnp.sum(lq2_ref[...] * lk2_ref[...], axis=-1, keepdims=True)) + lam_init)
    lane = lax.broadcasted_iota(jnp.int32, (t, BLOCK), 1)
    key_index = lax.broadcasted_iota(jnp.int32, (t, BLOCK), 0)
    low, high = key_index & (V7X_MXU_COLUMNS - 1), key_index >> 8
    assert t <= 2 * V7X_MXU_COLUMNS and V7X_MXU_COLUMNS == 1 << 8
    k_extra = jnp.where(lane < SLOPE_PARTS, low, jnp.where(lane < 2 * SLOPE_PARTS, high, 0)).astype(F32).astype(BF16)
    q_extra = jnp.zeros((2 * t, BLOCK), F32)
    lane2 = lax.broadcasted_iota(jnp.int32, (2 * t, BLOCK), 1)
    for i in range(SLOPE_PARTS):
        q_extra = jnp.where(lane2 == i, slope_ref[h, i], q_extra)
        q_extra = jnp.where(lane2 == SLOPE_PARTS + i, slope_ref[h, i] * float(V7X_MXU_COLUMNS), q_extra)
    q_extra = q_extra.astype(BF16)
    key = lax.broadcasted_iota(jnp.int32, (t, 2 * t), 0)
    query = lax.broadcasted_iota(jnp.int32, (t, 2 * t), 1)
    causal = key <= jnp.where(query >= t, query - t, query)

    def queries(qi):
        q = q_ref[0, qi * t:(qi + 1) * t, :]
        zero = jnp.zeros_like(q)
        qq = jnp.concatenate([jnp.where(lane < DIFF_HALF_DIM, q, zero),
                              jnp.where(lane >= DIFF_HALF_DIM, q, zero)], axis=0)
        return jnp.concatenate([qq, q_extra], axis=1)

    order = sorted(range(n_tiles), key=lambda qi: -qi)
    chains = [[], []]
    for qi in order:
        min(chains, key=len).extend((qi, j) for j in range(qi + 1))
    state = {}
    for step in range(max(len(c) for c in chains)):
        live = [c[step] for c in chains if step < len(c)]
        for qi, j in live:
            if j == 0:
                state[qi] = dict(qq=queries(qi))
        s = {}
        for qi, j in live:
            k = jnp.concatenate([k_ref[0, j * t:(j + 1) * t, :], k_extra], axis=1)
            s[qi] = _dot_nt(k, state[qi]["qq"])
            if j == qi:
                s[qi] = jnp.where(causal, s[qi], NEG_INF)
        p = {}
        for qi, j in live:
            st = state[qi]
            shift = slope * float((j - qi) * t)
            m_tile = jnp.max(s[qi], axis=0, keepdims=True) + shift
            st["m_old"], st["m"] = st.get("m"), (m_tile if j == 0 else jnp.maximum(st["m"], m_tile))
            p[qi] = jnp.exp2(s[qi] - (st["m"] - shift)).astype(BF16)
        for qi, j in live:
            st = state[qi]
            acc_tile = _dot(vt_ref[j], p[qi])
            st["acc"] = acc_tile if j == 0 else jnp.exp2(st["m_old"] - st["m"]) * st["acc"] + acc_tile
        for qi, j in live:
            if j == qi:
                acc = state.pop(qi)["acc"]
                o = acc[:BLOCK] / acc[BLOCK:BLOCK + 1]
                o = (o[:, :t] - lam * o[:, t:]).T
                o_ref[0, qi * t:(qi + 1) * t, :] = (_rms_norm(o, g_ref[...]) * (1.0 - lam_init)).astype(o_ref.dtype)


def _diff_attention(proj, slopes, lq1, lk1, lq2, lk2, g, lam_init, heads, t):
    b, s, _ = proj.shape
    q_blk, k_blk, v_blk = 0, heads, 2 * heads
    vec = lambda n: pl.BlockSpec((1, n), lambda bi, hi: (0, 0))
    slab = lambda blk: pl.BlockSpec((1, s, BLOCK), lambda bi, hi: (bi, 0, blk + hi))
    return pl.pallas_call(
        functools.partial(_diff_attn_kernel, t=t, lam_init=lam_init),
        grid=(b, heads),
        in_specs=[pl.BlockSpec(memory_space=pltpu.SMEM), slab(q_blk), slab(k_blk), slab(v_blk),
                  vec(DIFF_HALF_DIM), vec(DIFF_HALF_DIM), vec(DIFF_HALF_DIM), vec(DIFF_HALF_DIM), vec(BLOCK)],
        out_specs=slab(0),
        out_shape=jax.ShapeDtypeStruct((b, s, heads * BLOCK), BF16),
        scratch_shapes=[pltpu.VMEM((s // t, BLOCK + BF16_SUBLANES, t), BF16)],
        compiler_params=_params(("arbitrary", "arbitrary")),
    )(slopes, proj, proj, proj, lq1, lk1, lq2, lk2, g)


def _split_slopes(slopes):
    x = np.asarray(slopes, np.float64) * math.log2(math.e)
    x = x.astype(np.float32)
    parts, rest = [], x.copy()
    for _ in range(SLOPE_PARTS):
        part = rest.astype(BF16).astype(np.float32)
        parts.append(part)
        rest = rest - part
    return jnp.asarray(np.stack(parts + [x], axis=1), F32)


def _dilated_attn_kernel(slope_ref, q_ref, k_ref, v_ref, g_ref, o_ref, qf, kf, vf, out_ref, lse_ref,
                         *, seq, tc):
    h = pl.program_id(1)
    slope = slope_ref[h]
    qf[...] = q_ref[0].astype(F32)
    kf[...] = k_ref[0].astype(F32)
    vf[...] = v_ref[0].astype(F32)

    qi = lax.broadcasted_iota(jnp.int32, (BLOCK, 2 * BLOCK), 0)
    kj = lax.broadcasted_iota(jnp.int32, (BLOCK, 2 * BLOCK), 1)
    rel = qi + BLOCK - kj

    def rows(start, size, dil):
        return pl.ds(start, size) if dil == 1 else pl.ds(start, size, stride=dil)

    def tiles(branch, dil, starts, bias):
        nkeys = bias.shape[1]
        q_rows = [rows(q_start, BLOCK, dil) for q_start, _ in starts]
        k_rows = [rows(k_start, nkeys, dil) for _, k_start in starts]
        ones = jnp.ones((nkeys, BLOCK), BF16)
        s = [_dot_nt(qf[qr, :].astype(BF16), kf[kr, :].astype(BF16)) + bias for qr, kr in zip(q_rows, k_rows)]
        m = [jnp.max(si, axis=-1, keepdims=True) for si in s]
        p = [jnp.exp2(si - mi).astype(BF16) for si, mi in zip(s, m)]
        v1 = [jnp.concatenate([vf[kr, :].astype(BF16), ones], axis=1) for kr in k_rows]
        r = [_dot(pi, vi) for pi, vi in zip(p, v1)]
        for qr, mi, ri in zip(q_rows, m, r):
            l = ri[:, BLOCK:]
            out_ref[branch, qr, :] = ri[:, :BLOCK] / l
            lse_ref[branch, qr, :] = mi + jnp.log2(l)

    for branch, (window, dil) in enumerate(DIL_CONFIGS):
        n_back = window // dil
        nb = seq // dil // BLOCK
        step = dil * BLOCK
        valid = (rel >= 0) & (rel <= n_back)
        bias = jnp.where(valid, (-slope * dil) * rel.astype(F32), NEG_INF)
        bias_first = bias[:, BLOCK:]

        def first_tiles(classes, branch=branch, dil=dil, bias_first=bias_first):
            tiles(branch, dil, [(c, c) for c in classes], bias_first)

        def later_tiles(c, blocks, branch=branch, dil=dil, step=step, bias=bias):
            tiles(branch, dil, [(c + n * step, c + (n - 1) * step) for n in blocks], bias)

        if nb == 1:
            def class_group(i, carry, first_tiles=first_tiles):
                first_tiles([i * TILES_PER_BODY + u for u in range(TILES_PER_BODY)])
                return carry
            lax.fori_loop(0, dil // TILES_PER_BODY, class_group, 0)
        elif nb <= TILES_PER_BODY:
            per_body = TILES_PER_BODY // nb
            def class_group(i, carry, first_tiles=first_tiles, later_tiles=later_tiles, nb=nb, per_body=per_body):
                classes = [i * per_body + u for u in range(per_body)]
                first_tiles(classes)
                for c in classes:
                    later_tiles(c, range(1, nb))
                return carry
            lax.fori_loop(0, dil // per_body, class_group, 0)
        else:
            group = TILES_PER_BODY - 1
            assert dil == 1 and (nb - 1) % group == 0
            first_tiles([0])
            def block_group(i, carry, later_tiles=later_tiles, group=group):
                later_tiles(0, [1 + i * group + u for u in range(group)])
                return carry
            lax.fori_loop(0, (nb - 1) // group, block_group, 0)

    g = g_ref[...]
    for t in range(seq // tc):
        r = pl.ds(t * tc, tc)
        lses = [lse_ref[i, r, :] for i in range(len(DIL_CONFIGS))]
        top = functools.reduce(jnp.maximum, lses)
        ws = [jnp.exp2(lse - top) for lse in lses]
        num = sum(w * out_ref[i, r, :] for i, w in enumerate(ws))
        o_ref[0, r, :] = _rms_norm(num / sum(ws), g).astype(o_ref.dtype)


def _dilated_attention(proj, slopes, g, heads, first_blk):
    b, s, _ = proj.shape
    nbr = len(DIL_CONFIGS)
    spec = lambda blk: pl.BlockSpec((1, s, BLOCK), lambda bi, hi: (bi, 0, blk + hi))
    return pl.pallas_call(
        functools.partial(_dilated_attn_kernel, seq=s, tc=BLOCK),
        grid=(b, heads),
        in_specs=[pl.BlockSpec(memory_space=pltpu.SMEM),
                  spec(first_blk), spec(first_blk + heads), spec(first_blk + 2 * heads),
                  pl.BlockSpec((1, BLOCK), lambda bi, hi: (0, 0))],
        out_specs=pl.BlockSpec((1, s, BLOCK), lambda bi, hi: (bi, 0, hi)),
        out_shape=jax.ShapeDtypeStruct((b, s, heads * BLOCK), BF16),
        scratch_shapes=[pltpu.VMEM((s, BLOCK), F32), pltpu.VMEM((s, BLOCK), F32), pltpu.VMEM((s, BLOCK), F32),
                        pltpu.VMEM((nbr, s, BLOCK), F32), pltpu.VMEM((nbr, s, BLOCK), F32)],
        compiler_params=_params(("arbitrary", "arbitrary")),
    )(slopes, proj, proj, proj, g)


def _outproj_ln_kernel(*refs, n_in, alpha):
    o_refs, w_refs = refs[:n_in], refs[n_in:2 * n_in]
    h_ref, g_ref, b_ref, out_ref = refs[2 * n_in:]
    y = _dot(o_refs[0][...], w_refs[0][...])
    for o_ref, w_ref in zip(o_refs[1:], w_refs[1:]):
        y = y + _dot(o_ref[...], w_ref[...])
    out_ref[...] = _layer_norm(alpha * h_ref[...] + y, g_ref[...], b_ref[...])


def _outproj_ln(os, w, h, g, b, alpha, tm):
    m, d = h.shape
    n_in = len(os)
    kw = w.shape[0] // n_in
    return pl.pallas_call(
        functools.partial(_outproj_ln_kernel, n_in=n_in, alpha=alpha),
        grid=(m // tm,),
        in_specs=([pl.BlockSpec((tm, kw), lambda i: (i, 0)) for _ in os]
                  + [_resident((kw, d), functools.partial(lambda i, r: (r, 0), r=r)) for r in range(n_in)]
                  + [pl.BlockSpec((tm, d), lambda i: (i, 0)),
                     pl.BlockSpec((1, d), lambda i: (0, 0)), pl.BlockSpec((1, d), lambda i: (0, 0))]),
        out_specs=pl.BlockSpec((tm, d), lambda i: (i, 0)),
        out_shape=jax.ShapeDtypeStruct((m, d), F32),
        compiler_params=_params(("arbitrary",)),
    )(*os, *([w] * n_in), h, g, b)


def _mem_attn_kernel(h_ref, wq_ref, kv_ref, o_ref, *, heads):
    d = h_ref.shape[1]
    hd = d // heads
    q = _dot(h_ref[...].astype(BF16), wq_ref[...]).astype(BF16)
    scale = hd ** -0.5
    for i in range(heads):
        k = kv_ref[0, :, i * hd:(i + 1) * hd]
        v = kv_ref[0, :, d + i * hd:d + (i + 1) * hd]
        s = _dot_nt(q[:, i * hd:(i + 1) * hd], k) * scale
        e = jnp.exp(s - jnp.max(s, axis=-1, keepdims=True))
        p = e * (1.0 / jnp.sum(e, axis=-1, keepdims=True))
        o_ref[:, i * hd:(i + 1) * hd] = _dot(p.astype(BF16), v).astype(o_ref.dtype)


def _mem_attention(h, wq, kv, seq, tm):
    m, d = h.shape
    mem_len = kv.shape[1]
    per_seq = seq // tm
    return pl.pallas_call(
        functools.partial(_mem_attn_kernel, heads=MEM_HEADS),
        grid=(m // tm,),
        in_specs=[pl.BlockSpec((tm, d), lambda i: (i, 0)),
                  _resident((d, d), lambda i: (0, 0)),
                  pl.BlockSpec((1, mem_len, 2 * d), lambda i: (i // per_seq, 0, 0))],
        out_specs=pl.BlockSpec((tm, d), lambda i: (i, 0)),
        out_shape=jax.ShapeDtypeStruct((m, d), BF16),
        compiler_params=_params(("arbitrary",)),
    )(h, wq, kv)


HALO = 8
FFN_ROW_CHUNK = 64
V7X_MXU_COLUMNS = 256
FFN_MATMUL_PART_COLUMNS = 2 * V7X_MXU_COLUMNS
BF16_SUBLANES = 16
SLOPE_PARTS = 3


def _ffn_kernel(x_ref, wg_ref, wu_ref, cwg_ref, cwu_ref, cbg_ref, cbu_ref, wd_ref, g_ref, b_ref, o_ref,
                xb_ref, hg0_ref, hu0_ref, hg1_ref, hu1_ref, act0_ref, act1_ref, carry_g_ref, carry_u_ref,
                *, tm, tiles_per_seq, alpha):
    i = pl.program_id(0)
    f = pl.program_id(1)
    nf = pl.num_programs(1) - 2
    seq_start = (i % tiles_per_seq) == 0
    h_refs = ((hg0_ref, hu0_ref), (hg1_ref, hu1_ref))
    act_refs = (act0_ref, act1_ref)
    acc_ref = o_ref

    tf = wg_ref.shape[1]
    d = wd_ref.shape[1]

    def up_project(slot):
        def part(which, c0):
            w_ref, hs_ref, carry_ref = ((wg_ref, wu_ref)[which], h_refs[slot][which],
                                        (carry_g_ref, carry_u_ref)[which])
            cols = slice(c0, c0 + FFN_MATMUL_PART_COLUMNS)
            hs_ref[HALO:, cols] = _dot(xb_ref[...], w_ref[:, cols])
            hs_ref[:HALO, cols] = jnp.where(seq_start, 0.0, carry_ref[f, :, cols])
            carry_ref[f, :, cols] = hs_ref[tm:, cols]
        return [functools.partial(part, which, c0) for which in range(2)
                for c0 in range(0, tf, FFN_MATMUL_PART_COLUMNS)]

    def conv(hs_ref, cw_ref, cb_ref, row0, cols):
        out = cb_ref[:, cols]
        for tap in range(CONV_WIDTH):
            off = row0 + HALO - (CONV_WIDTH - 1) + tap
            out = out + hs_ref[off:off + FFN_ROW_CHUNK, cols] * cw_ref[tap:tap + 1, cols]
        return out

    def activate(slot):
        def part(c0, row0):
            cols = slice(c0, c0 + BLOCK)
            gate = conv(h_refs[slot][0], cwg_ref, cbg_ref, row0, cols)
            up = conv(h_refs[slot][1], cwu_ref, cbu_ref, row0, cols)
            act = gate * (1.0 / (1.0 + jnp.exp(-gate))) * up
            act_refs[slot][row0:row0 + FFN_ROW_CHUNK, cols] = act.astype(BF16)
        return [functools.partial(part, c0, row0) for c0 in range(0, tf, BLOCK)
                for row0 in range(0, tm, FFN_ROW_CHUNK)]

    def down_project(slot):
        def part(c0):
            cols = slice(c0, c0 + FFN_MATMUL_PART_COLUMNS)
            acc_ref[:, cols] += _dot(act_refs[slot][...], wd_ref[:, cols])
        return [functools.partial(part, c0) for c0 in range(0, d, FFN_MATMUL_PART_COLUMNS)]

    def run(*stages):
        order = sorted(((k + 0.5) / len(parts), s, k) for s, parts in enumerate(stages) for k in range(len(parts)))
        for _, s, k in order:
            stages[s][k]()

    @pl.when(f == 0)
    def _():
        @pl.when(i == 0)
        def _():
            carry_g_ref[...] = jnp.zeros(carry_g_ref.shape, F32)
            carry_u_ref[...] = jnp.zeros(carry_u_ref.shape, F32)

        xb_ref[...] = x_ref[...].astype(BF16)
        acc_ref[...] = jnp.zeros(acc_ref.shape, F32)
        run(up_project(0))

    @pl.when(f == 1)
    def _():
        run(up_project(1), activate(0))

    for parity in range(2):
        @pl.when((f > 1) & (f < nf) & (f % 2 == parity))
        def _(parity=parity):
            run(up_project(parity), activate(1 - parity), down_project(parity))

    @pl.when(f == nf)
    def _():
        run(down_project((nf - 2) % 2), activate((nf - 1) % 2))

    @pl.when(f == nf + 1)
    def _():
        run(down_project((nf - 1) % 2))
        o_ref[...] = _layer_norm(alpha * x_ref[...] + acc_ref[...], g_ref[...], b_ref[...])


def _conv_ffn_ln(x, w_up, conv_w, conv_b, w_down, g, b, alpha, seq, tm, tf):
    m, d = x.shape
    d_ff = w_down.shape[0]
    nf = d_ff // tf
    row = lambda i, f: (i, 0)
    fixed = lambda i, f: (0, 0)
    chunk = lambda f, lag: jnp.clip(f - lag, 0, nf - 1)
    h_buf = pltpu.VMEM((tm + HALO, tf), F32)
    act_buf = pltpu.VMEM((tm, tf), BF16)
    return pl.pallas_call(
        functools.partial(_ffn_kernel, tm=tm, tiles_per_seq=seq // tm, alpha=alpha),
        grid=(m // tm, nf + 2),
        in_specs=[pl.BlockSpec((tm, d), row),
                  pl.BlockSpec((d, tf), lambda i, f: (0, chunk(f, 0))),
                  pl.BlockSpec((d, tf), lambda i, f: (0, nf + chunk(f, 0))),
                  pl.BlockSpec((CONV_WIDTH, tf), lambda i, f: (0, chunk(f, 1))),
                  pl.BlockSpec((CONV_WIDTH, tf), lambda i, f: (0, nf + chunk(f, 1))),
                  pl.BlockSpec((1, tf), lambda i, f: (0, chunk(f, 1))),
                  pl.BlockSpec((1, tf), lambda i, f: (0, nf + chunk(f, 1))),
                  pl.BlockSpec((tf, d), lambda i, f: (chunk(f, 2), 0)),
                  pl.BlockSpec((1, d), fixed), pl.BlockSpec((1, d), fixed)],
        out_specs=pl.BlockSpec((tm, d), row),
        out_shape=jax.ShapeDtypeStruct((m, d), F32),
        scratch_shapes=[pltpu.VMEM((tm, d), BF16), h_buf, h_buf, h_buf, h_buf, act_buf, act_buf,
                        pltpu.VMEM((nf, HALO, tf), F32), pltpu.VMEM((nf, HALO, tf), F32)],
        compiler_params=_params(("arbitrary", "arbitrary")),
    )(x, w_up, w_up, conv_w, conv_w, conv_b, conv_b, w_down, g, b)


def _alibi_slopes(n):
    return (2.0 ** (-8.0 * np.arange(1, n + 1) / n)).astype(np.float32)


def kernel(x, mem, w_in, w_mix_out, lambda_q1, lambda_k1, lambda_q2, lambda_k2, g_diff, g_dil, ln1_g, ln1_b,
           w_mem_q, w_mem_kv, w_mem_o, ln2_g, ln2_b, w_up, conv_w, conv_b, w_down, ln3_g, ln3_b):
    batch, seq, d = x.shape
    depth = w_in.shape[0]
    mem_len = mem.shape[1]
    diff_heads = dil_heads = d // 256
    alpha = (2 * depth) ** 0.25
    slopes = _alibi_slopes(diff_heads + dil_heads)
    log2e = math.log2(math.e)
    slopes_diff = _split_slopes(slopes[0::2])
    slopes_dil = jnp.asarray(slopes[1::2].astype(np.float64) * log2e, F32)
    row = lambda a: a.reshape(1, -1)

    group = diff_heads * BLOCK
    col_scale = np.ones((1, 6 * group), np.float32)
    col_scale[:, :group] = DIFF_HALF_DIM ** -0.5 * log2e
    col_scale[:, 3 * group:4 * group] = BLOCK ** -0.5 * log2e
    col_scale = jnp.asarray(col_scale)
    kv_scale = jnp.ones((1, 2 * d), F32)

    h = x.reshape(batch * seq, d)
    mem2 = mem.reshape(batch * mem_len, d)
    for l in range(depth):
        lam_init = 0.8 - 0.6 * math.exp(-0.3 * l)
        proj = _project(h, w_in[l].astype(BF16), col_scale, tm=1024, tn=1024).reshape(batch, seq, -1)
        o_diff = _diff_attention(proj, slopes_diff, row(lambda_q1[l]), row(lambda_k1[l]), row(lambda_q2[l]),
                                 row(lambda_k2[l]), row(g_diff[l]), lam_init, diff_heads, t=512)
        o_dil = _dilated_attention(proj, slopes_dil, row(g_dil[l]), dil_heads, first_blk=3 * diff_heads)
        h = _outproj_ln([o_diff.reshape(batch * seq, -1), o_dil.reshape(batch * seq, -1)],
                        w_mix_out[l].astype(BF16), h, row(ln1_g[l]), row(ln1_b[l]), alpha, tm=512)

        kv = _project(mem2, w_mem_kv[l].astype(BF16), kv_scale, tm=1024, tn=1024).reshape(batch, mem_len, 2 * d)
        o_mem = _mem_attention(h, w_mem_q[l].astype(BF16), kv, seq, tm=512)
        h = _outproj_ln([o_mem], w_mem_o[l].astype(BF16), h, row(ln2_g[l]), row(ln2_b[l]), alpha, tm=512)

        h = _conv_ffn_ln(h, w_up[l].astype(BF16), conv_w[l], row(conv_b[l]), w_down[l].astype(BF16),
                         row(ln3_g[l]), row(ln3_b[l]), alpha, seq, tm=512, tf=512)
    return h.reshape(batch, seq, d)
```

```python
import functools
import math

import numpy as np
import jax
import jax.numpy as jnp
from jax import lax
from jax.experimental import pallas as pl
from jax.experimental.pallas import tpu as pltpu

F32 = jnp.float32
BF16 = jnp.bfloat16

LN_EPS = 1e-5
RMS_EPS = 1e-5
BLOCK = 128
DIFF_HALF_DIM = 64
DIL_CONFIGS = ((128, 1), (512, 4), (2048, 16))
TILES_PER_BODY = 16
DIFF_CHAINS = 2
MEM_HEADS = 4
CONV_WIDTH = 3
NEG_INF = float("-inf")

V7X_VMEM_LIMIT_BYTES = 56 * 1024 * 1024


def _params(semantics, vmem_limit_bytes=V7X_VMEM_LIMIT_BYTES):
    return pltpu.CompilerParams(dimension_semantics=semantics, vmem_limit_bytes=vmem_limit_bytes)


def _resident(shape, index_map):
    return pl.BlockSpec(shape, index_map, pipeline_mode=pl.Buffered(1))


def _layer_norm(z, g, b):
    mu = jnp.mean(z, axis=-1, keepdims=True)
    zc = z - mu
    var = jnp.mean(zc * zc, axis=-1, keepdims=True)
    return zc * lax.rsqrt(var + LN_EPS) * g + b


def _rms_norm(o, g):
    return o * lax.rsqrt(jnp.mean(o * o, axis=-1, keepdims=True) + RMS_EPS) * g


def _interleave(*stages):
    order = sorted(((k + 0.5) / len(parts), s, k) for s, parts in enumerate(stages) for k in range(len(parts)))
    for _, s, k in order:
        stages[s][k]()


def _dot(a, b):
    return jnp.dot(a, b, preferred_element_type=F32)


def _dot_nt(a, b):
    return lax.dot_general(a, b, (((1,), (1,)), ((), ())), preferred_element_type=F32)


def _proj_kernel(x_ref, w_ref, cs_ref, o_ref, xb_ref):
    @pl.when(pl.program_id(1) == 0)
    def _():
        xb_ref[...] = x_ref[...].astype(BF16)

    o_ref[...] = (_dot(xb_ref[...], w_ref[...]) * cs_ref[...]).astype(o_ref.dtype)


def _project(x, w, col_scale, tm, tn):
    m, k = x.shape
    n = w.shape[1]
    tm = min(tm, m)
    return pl.pallas_call(
        _proj_kernel,
        grid=(m // tm, n // tn),
        in_specs=[pl.BlockSpec((tm, k), lambda i, j: (i, 0)),
                  pl.BlockSpec((k, tn), lambda i, j: (0, j)),
                  pl.BlockSpec((1, tn), lambda i, j: (0, j))],
        out_specs=pl.BlockSpec((tm, tn), lambda i, j: (i, j)),
        out_shape=jax.ShapeDtypeStruct((m, n), BF16),
        scratch_shapes=[pltpu.VMEM((tm, k), BF16)],
        compiler_params=_params(("arbitrary", "arbitrary")),
    )(x, w, col_scale)


def _diff_attn_kernel(slope_ref, q_ref, k_ref, v_ref, lq1_ref, lk1_ref, lq2_ref, lk2_ref, g_ref, o_ref,
                      vt_ref, *, t, lam_init):
    h = pl.program_id(1)
    slope = slope_ref[h, SLOPE_PARTS]
    n_tiles = vt_ref.shape[0]
    for j in range(n_tiles):
        vt_ref[j, :BLOCK, :] = v_ref[0, j * t:(j + 1) * t, :].astype(F32).T.astype(BF16)
        vt_ref[j, BLOCK:, :] = jnp.ones((vt_ref.shape[1] - BLOCK, t), BF16)

    lam = (jnp.exp(jnp.sum(lq1_ref[...] * lk1_ref[...], axis=-1, keepdims=True))
           - jnp.exp(jnp.sum(lq2_ref[...] * lk2_ref[...], axis=-1, keepdims=True)) + lam_init)
    lane = lax.broadcasted_iota(jnp.int32, (t, BLOCK), 1)
    key_index = lax.broadcasted_iota(jnp.int32, (t, BLOCK), 0)
    low, high = key_index & (V7X_MXU_COLUMNS - 1), key_index >> 8
    assert t <= 2 * V7X_MXU_COLUMNS and V7X_MXU_COLUMNS == 1 << 8
    k_extra = jnp.where(lane < SLOPE_PARTS, low, jnp.where(lane < 2 * SLOPE_PARTS, high, 0)).astype(F32).astype(BF16)
    q_extra = jnp.zeros((2 * t, BLOCK), F32)
    lane2 = lax.broadcasted_iota(jnp.int32, (2 * t, BLOCK), 1)
    for i in range(SLOPE_PARTS):
        q_extra = jnp.where(lane2 == i, slope_ref[h, i], q_extra)
        q_extra = jnp.where(lane2 == SLOPE_PARTS + i, slope_ref[h, i] * float(V7X_MXU_COLUMNS), q_extra)
    q_extra = q_extra.astype(BF16)
    key = lax.broadcasted_iota(jnp.int32, (t, 2 * t), 0)
    query = lax.broadcasted_iota(jnp.int32, (t, 2 * t), 1)
    causal = key <= jnp.where(query >= t, query - t, query)

    def queries(qi):
        q = q_ref[0, qi * t:(qi + 1) * t, :]
        zero = jnp.zeros_like(q)
        qq = jnp.concatenate([jnp.where(lane < DIFF_HALF_DIM, q, zero),
                              jnp.where(lane >= DIFF_HALF_DIM, q, zero)], axis=0)
        return jnp.concatenate([qq, q_extra], axis=1)

    order = sorted(range(n_tiles), key=lambda qi: -qi)
    chains = [[] for _ in range(DIFF_CHAINS)]
    for qi in order:
        min(chains, key=len).extend((qi, j) for j in range(qi + 1))
    state = {}
    for step in range(max(len(c) for c in chains)):
        live = [c[step] for c in chains if step < len(c)]
        for qi, j in live:
            if j == 0:
                state[qi] = dict(qq=queries(qi))
        s = {}
        for qi, j in live:
            k = jnp.concatenate([k_ref[0, j * t:(j + 1) * t, :], k_extra], axis=1)
            s[qi] = _dot_nt(k, state[qi]["qq"])
            if j == qi:
                s[qi] = jnp.where(causal, s[qi], NEG_INF)
        p = {}
        for qi, j in live:
            st = state[qi]
            shift = slope * float((j - qi) * t)
            m_tile = jnp.max(s[qi], axis=0, keepdims=True) + shift
            st["m_old"], st["m"] = st.get("m"), (m_tile if j == 0 else jnp.maximum(st["m"], m_tile))
            p[qi] = jnp.exp2(s[qi] - (st["m"] - shift)).astype(BF16)
        for qi, j in live:
            st = state[qi]
            acc_tile = _dot(vt_ref[j], p[qi])
            st["acc"] = acc_tile if j == 0 else jnp.exp2(st["m_old"] - st["m"]) * st["acc"] + acc_tile
        for qi, j in live:
            if j == qi:
                acc = state.pop(qi)["acc"]
                o = acc[:BLOCK] / acc[BLOCK:BLOCK + 1]
                o = (o[:, :t] - lam * o[:, t:]).T
                o_ref[0, qi * t:(qi + 1) * t, :] = (_rms_norm(o, g_ref[...]) * (1.0 - lam_init)).astype(o_ref.dtype)


def _diff_attention(proj, slopes, lq1, lk1, lq2, lk2, g, lam_init, heads, t):
    b, s, _ = proj.shape
    q_blk, k_blk, v_blk = 0, heads, 2 * heads
    vec = lambda n: pl.BlockSpec((1, n), lambda bi, hi: (0, 0))
    slab = lambda blk: pl.BlockSpec((1, s, BLOCK), lambda bi, hi: (bi, 0, blk + hi))
    return pl.pallas_call(
        functools.partial(_diff_attn_kernel, t=t, lam_init=lam_init),
        grid=(b, heads),
        in_specs=[pl.BlockSpec(memory_space=pltpu.SMEM), slab(q_blk), slab(k_blk), slab(v_blk),
                  vec(DIFF_HALF_DIM), vec(DIFF_HALF_DIM), vec(DIFF_HALF_DIM), vec(DIFF_HALF_DIM), vec(BLOCK)],
        out_specs=slab(0),
        out_shape=jax.ShapeDtypeStruct((b, s, heads * BLOCK), BF16),
        scratch_shapes=[pltpu.VMEM((s // t, BLOCK + BF16_SUBLANES, t), BF16)],
        compiler_params=_params(("arbitrary", "arbitrary")),
    )(slopes, proj, proj, proj, lq1, lk1, lq2, lk2, g)


def _split_slopes(slopes):
    x = np.asarray(slopes, np.float64) * math.log2(math.e)
    x = x.astype(np.float32)
    parts, rest = [], x.copy()
    for _ in range(SLOPE_PARTS):
        part = rest.astype(BF16).astype(np.float32)
        parts.append(part)
        rest = rest - part
    return jnp.asarray(np.stack(parts + [x], axis=1), F32)


def _dilated_attn_kernel(slope_ref, q_ref, k_ref, v_ref, g_ref, o_ref, qf, kf, vf, out_ref, lse_ref,
                         *, seq, tc):
    h = pl.program_id(1)
    slope = slope_ref[h]
    qf[...] = q_ref[0].astype(F32)
    kf[...] = k_ref[0].astype(F32)
    vf[...] = v_ref[0].astype(F32)

    qi = lax.broadcasted_iota(jnp.int32, (BLOCK, 2 * BLOCK), 0)
    kj = lax.broadcasted_iota(jnp.int32, (BLOCK, 2 * BLOCK), 1)
    rel = qi + BLOCK - kj

    def rows(start, size, dil):
        return pl.ds(start, size) if dil == 1 else pl.ds(start, size, stride=dil)

    def tiles(branch, dil, starts, bias):
        nkeys = bias.shape[1]
        q_rows = [rows(q_start, BLOCK, dil) for q_start, _ in starts]
        k_rows = [rows(k_start, nkeys, dil) for _, k_start in starts]
        ones = jnp.ones((nkeys, BLOCK), BF16)
        s = [_dot_nt(qf[qr, :].astype(BF16), kf[kr, :].astype(BF16)) + bias for qr, kr in zip(q_rows, k_rows)]
        m = [jnp.max(si, axis=-1, keepdims=True) for si in s]
        p = [jnp.exp2(si - mi).astype(BF16) for si, mi in zip(s, m)]
        v1 = [jnp.concatenate([vf[kr, :].astype(BF16), ones], axis=1) for kr in k_rows]
        r = [_dot(pi, vi) for pi, vi in zip(p, v1)]
        for qr, mi, ri in zip(q_rows, m, r):
            l = ri[:, BLOCK:]
            out_ref[branch, qr, :] = ri[:, :BLOCK] / l
            lse_ref[branch, qr, :] = mi + jnp.log2(l)

    for branch, (window, dil) in enumerate(DIL_CONFIGS):
        n_back = window // dil
        nb = seq // dil // BLOCK
        step = dil * BLOCK
        valid = (rel >= 0) & (rel <= n_back)
        bias = jnp.where(valid, (-slope * dil) * rel.astype(F32), NEG_INF)
        bias_first = bias[:, BLOCK:]

        def first_tiles(classes, branch=branch, dil=dil, bias_first=bias_first):
            tiles(branch, dil, [(c, c) for c in classes], bias_first)

        def later_tiles(c, blocks, branch=branch, dil=dil, step=step, bias=bias):
            tiles(branch, dil, [(c + n * step, c + (n - 1) * step) for n in blocks], bias)

        if nb == 1:
            def class_group(i, carry, first_tiles=first_tiles):
                first_tiles([i * TILES_PER_BODY + u for u in range(TILES_PER_BODY)])
                return carry
            lax.fori_loop(0, dil // TILES_PER_BODY, class_group, 0)
        elif nb <= TILES_PER_BODY:
            per_body = TILES_PER_BODY // nb
            def class_group(i, carry, first_tiles=first_tiles, later_tiles=later_tiles, nb=nb, per_body=per_body):
                classes = [i * per_body + u for u in range(per_body)]
                first_tiles(classes)
                for c in classes:
                    later_tiles(c, range(1, nb))
                return carry
            lax.fori_loop(0, dil // per_body, class_group, 0)
        else:
            group = TILES_PER_BODY - 1
            assert dil == 1 and (nb - 1) % group == 0
            first_tiles([0])
            def block_group(i, carry, later_tiles=later_tiles, group=group):
                later_tiles(0, [1 + i * group + u for u in range(group)])
                return carry
            lax.fori_loop(0, (nb - 1) // group, block_group, 0)

    g = g_ref[...]
    for t in range(seq // tc):
        r = pl.ds(t * tc, tc)
        lses = [lse_ref[i, r, :] for i in range(len(DIL_CONFIGS))]
        top = functools.reduce(jnp.maximum, lses)
        ws = [jnp.exp2(lse - top) for lse in lses]
        num = sum(w * out_ref[i, r, :] for i, w in enumerate(ws))
        o_ref[0, r, :] = _rms_norm(num / sum(ws), g).astype(o_ref.dtype)


def _dilated_attention(proj, slopes, g, heads, first_blk):
    b, s, _ = proj.shape
    nbr = len(DIL_CONFIGS)
    spec = lambda blk: pl.BlockSpec((1, s, BLOCK), lambda bi, hi: (bi, 0, blk + hi))
    return pl.pallas_call(
        functools.partial(_dilated_attn_kernel, seq=s, tc=BLOCK),
        grid=(b, heads),
        in_specs=[pl.BlockSpec(memory_space=pltpu.SMEM),
                  spec(first_blk), spec(first_blk + heads), spec(first_blk + 2 * heads),
                  pl.BlockSpec((1, BLOCK), lambda bi, hi: (0, 0))],
        out_specs=pl.BlockSpec((1, s, BLOCK), lambda bi, hi: (bi, 0, hi)),
        out_shape=jax.ShapeDtypeStruct((b, s, heads * BLOCK), BF16),
        scratch_shapes=[pltpu.VMEM((s, BLOCK), F32), pltpu.VMEM((s, BLOCK), F32), pltpu.VMEM((s, BLOCK), F32),
                        pltpu.VMEM((nbr, s, BLOCK), F32), pltpu.VMEM((nbr, s, BLOCK), F32)],
        compiler_params=_params(("arbitrary", "arbitrary")),
    )(slopes, proj, proj, proj, g)


def _outproj_ln_kernel(*refs, n_in, alpha, tr):
    o_refs, w_refs = refs[:n_in], refs[n_in:2 * n_in]
    h_ref, g_ref, b_ref, out_ref = refs[2 * n_in:]
    tm, d = out_ref.shape

    def matmul_parts(r0):
        def part(c0):
            rows, cols = slice(r0, r0 + tr), slice(c0, c0 + V7X_MXU_COLUMNS)
            y = alpha * h_ref[rows, cols]
            for o_ref, w_ref in zip(o_refs, w_refs):
                y = y + _dot(o_ref[rows, :], w_ref[:, cols])
            out_ref[rows, cols] = y
        return [functools.partial(part, c0) for c0 in range(0, d, V7X_MXU_COLUMNS)]

    def norm_parts(r0):
        def part(r):
            rows = slice(r, r + FFN_ROW_CHUNK)
            out_ref[rows, :] = _layer_norm(out_ref[rows, :], g_ref[...], b_ref[...])
        return [functools.partial(part, r) for r in range(r0, r0 + tr, FFN_ROW_CHUNK)]

    pending = []
    for r0 in range(0, tm, tr):
        _interleave(matmul_parts(r0), pending)
        pending = norm_parts(r0)
    _interleave(pending)


def _outproj_ln(os, w, h, g, b, alpha, tm, tr):
    m, d = h.shape
    n_in = len(os)
    kw = w.shape[0] // n_in
    return pl.pallas_call(
        functools.partial(_outproj_ln_kernel, n_in=n_in, alpha=alpha, tr=tr),
        grid=(m // tm,),
        in_specs=([pl.BlockSpec((tm, kw), lambda i: (i, 0)) for _ in os]
                  + [_resident((kw, d), functools.partial(lambda i, r: (r, 0), r=r)) for r in range(n_in)]
                  + [pl.BlockSpec((tm, d), lambda i: (i, 0)),
                     pl.BlockSpec((1, d), lambda i: (0, 0)), pl.BlockSpec((1, d), lambda i: (0, 0))]),
        out_specs=pl.BlockSpec((tm, d), lambda i: (i, 0)),
        out_shape=jax.ShapeDtypeStruct((m, d), F32),
        compiler_params=_params(("arbitrary",)),
    )(*os, *([w] * n_in), h, g, b)


def _mem_attn_kernel(h_ref, wq_ref, kv_ref, o_ref, *, heads):
    d = h_ref.shape[1]
    hd = d // heads
    q = _dot(h_ref[...].astype(BF16), wq_ref[...]).astype(BF16)
    scale = hd ** -0.5
    for i in range(heads):
        k = kv_ref[0, :, i * hd:(i + 1) * hd]
        v = kv_ref[0, :, d + i * hd:d + (i + 1) * hd]
        s = _dot_nt(q[:, i * hd:(i + 1) * hd], k) * scale
        e = jnp.exp(s - jnp.max(s, axis=-1, keepdims=True))
        p = e * (1.0 / jnp.sum(e, axis=-1, keepdims=True))
        o_ref[:, i * hd:(i + 1) * hd] = _dot(p.astype(BF16), v).astype(o_ref.dtype)


def _mem_attention(h, wq, kv, seq, tm):
    m, d = h.shape
    mem_len = kv.shape[1]
    per_seq = seq // tm
    return pl.pallas_call(
        functools.partial(_mem_attn_kernel, heads=MEM_HEADS),
        grid=(m // tm,),
        in_specs=[pl.BlockSpec((tm, d), lambda i: (i, 0)),
                  _resident((d, d), lambda i: (0, 0)),
                  pl.BlockSpec((1, mem_len, 2 * d), lambda i: (i // per_seq, 0, 0))],
        out_specs=pl.BlockSpec((tm, d), lambda i: (i, 0)),
        out_shape=jax.ShapeDtypeStruct((m, d), BF16),
        compiler_params=_params(("arbitrary",)),
    )(h, wq, kv)


HALO = 8
FFN_ROW_CHUNK = 64
V7X_MXU_COLUMNS = 256
FFN_MATMUL_PART_COLUMNS = V7X_MXU_COLUMNS
BF16_SUBLANES = 16
SLOPE_PARTS = 3


def _ffn_kernel(x_ref, wg_ref, wu_ref, cwg_ref, cwu_ref, cbg_ref, cbu_ref, wd_ref, g_ref, b_ref, o_ref,
                xb_ref, hg0_ref, hu0_ref, hg1_ref, hu1_ref, act0_ref, act1_ref, carry_g_ref, carry_u_ref,
                *, tm, tiles_per_seq, alpha):
    i = pl.program_id(0)
    f = pl.program_id(1)
    nf = pl.num_programs(1) - 2
    seq_start = (i % tiles_per_seq) == 0
    h_refs = ((hg0_ref, hu0_ref), (hg1_ref, hu1_ref))
    act_refs = (act0_ref, act1_ref)
    acc_ref = o_ref

    tf = wg_ref.shape[1]
    d = wd_ref.shape[1]

    def up_project(slot):
        def part(which, c0):
            w_ref, hs_ref, carry_ref = ((wg_ref, wu_ref)[which], h_refs[slot][which],
                                        (carry_g_ref, carry_u_ref)[which])
            cols = slice(c0, c0 + FFN_MATMUL_PART_COLUMNS)
            hs_ref[HALO:, cols] = _dot(xb_ref[...], w_ref[:, cols])
            hs_ref[:HALO, cols] = jnp.where(seq_start, 0.0, carry_ref[f, :, cols])
            carry_ref[f, :, cols] = hs_ref[tm:, cols]
        return [functools.partial(part, which, c0) for which in range(2)
                for c0 in range(0, tf, FFN_MATMUL_PART_COLUMNS)]

    def conv(hs_ref, cw_ref, cb_ref, row0, cols):
        out = cb_ref[:, cols]
        for tap in range(CONV_WIDTH):
            off = row0 + HALO - (CONV_WIDTH - 1) + tap
            out = out + hs_ref[off:off + FFN_ROW_CHUNK, cols] * cw_ref[tap:tap + 1, cols]
        return out

    def activate(slot):
        def part(c0, row0):
            cols = slice(c0, c0 + BLOCK)
            gate = conv(h_refs[slot][0], cwg_ref, cbg_ref, row0, cols)
            up = conv(h_refs[slot][1], cwu_ref, cbu_ref, row0, cols)
            act = gate * (1.0 / (1.0 + jnp.exp2(gate * -math.log2(math.e)))) * up
            act_refs[slot][row0:row0 + FFN_ROW_CHUNK, cols] = act.astype(BF16)
        return [functools.partial(part, c0, row0) for c0 in range(0, tf, BLOCK)
                for row0 in range(0, tm, FFN_ROW_CHUNK)]

    def down_project(slot):
        def part(c0):
            cols = slice(c0, c0 + FFN_MATMUL_PART_COLUMNS)
            acc_ref[:, cols] += _dot(act_refs[slot][...], wd_ref[:, cols])
        return [functools.partial(part, c0) for c0 in range(0, d, FFN_MATMUL_PART_COLUMNS)]

    run = _interleave

    @pl.when(f == 0)
    def _():
        @pl.when(i == 0)
        def _():
            carry_g_ref[...] = jnp.zeros(carry_g_ref.shape, F32)
            carry_u_ref[...] = jnp.zeros(carry_u_ref.shape, F32)

        xb_ref[...] = x_ref[...].astype(BF16)
        acc_ref[...] = jnp.zeros(acc_ref.shape, F32)
        run(up_project(0))

    @pl.when(f == 1)
    def _():
        run(up_project(1), activate(0))

    for parity in range(2):
        @pl.when((f > 1) & (f < nf) & (f % 2 == parity))
        def _(parity=parity):
            run(up_project(parity), activate(1 - parity), down_project(parity))

    @pl.when(f == nf)
    def _():
        run(down_project((nf - 2) % 2), activate((nf - 1) % 2))

    @pl.when(f == nf + 1)
    def _():
        run(down_project((nf - 1) % 2))
        o_ref[...] = _layer_norm(alpha * x_ref[...] + acc_ref[...], g_ref[...], b_ref[...])


def _conv_ffn_ln(x, w_up, conv_w, conv_b, w_down, g, b, alpha, seq, tm, tf):
    m, d = x.shape
    d_ff = w_down.shape[0]
    nf = d_ff // tf
    row = lambda i, f: (i, 0)
    fixed = lambda i, f: (0, 0)
    chunk = lambda f, lag: jnp.clip(f - lag, 0, nf - 1)
    h_buf = pltpu.VMEM((tm + HALO, tf), F32)
    act_buf = pltpu.VMEM((tm, tf), BF16)
    return pl.pallas_call(
        functools.partial(_ffn_kernel, tm=tm, tiles_per_seq=seq // tm, alpha=alpha),
        grid=(m // tm, nf + 2),
        in_specs=[pl.BlockSpec((tm, d), row),
                  pl.BlockSpec((d, tf), lambda i, f: (0, chunk(f, 0))),
                  pl.BlockSpec((d, tf), lambda i, f: (0, nf + chunk(f, 0))),
                  pl.BlockSpec((CONV_WIDTH, tf), lambda i, f: (0, chunk(f, 1))),
                  pl.BlockSpec((CONV_WIDTH, tf), lambda i, f: (0, nf + chunk(f, 1))),
                  pl.BlockSpec((1, tf), lambda i, f: (0, chunk(f, 1))),
                  pl.BlockSpec((1, tf), lambda i, f: (0, nf + chunk(f, 1))),
                  pl.BlockSpec((tf, d), lambda i, f: (chunk(f, 2), 0)),
                  pl.BlockSpec((1, d), fixed), pl.BlockSpec((1, d), fixed)],
        out_specs=pl.BlockSpec((tm, d), row),
        out_shape=jax.ShapeDtypeStruct((m, d), F32),
        scratch_shapes=[pltpu.VMEM((tm, d), BF16), h_buf, h_buf, h_buf, h_buf, act_buf, act_buf,
                        pltpu.VMEM((nf, HALO, tf), F32), pltpu.VMEM((nf, HALO, tf), F32)],
        compiler_params=_params(("arbitrary", "arbitrary")),
    )(x, w_up, w_up, conv_w, conv_w, conv_b, conv_b, w_down, g, b)


def _alibi_slopes(n):
    return (2.0 ** (-8.0 * np.arange(1, n + 1) / n)).astype(np.float32)


def kernel(x, mem, w_in, w_mix_out, lambda_q1, lambda_k1, lambda_q2, lambda_k2, g_diff, g_dil, ln1_g, ln1_b,
           w_mem_q, w_mem_kv, w_mem_o, ln2_g, ln2_b, w_up, conv_w, conv_b, w_down, ln3_g, ln3_b):
    batch, seq, d = x.shape
    depth = w_in.shape[0]
    mem_len = mem.shape[1]
    diff_heads = dil_heads = d // 256
    alpha = (2 * depth) ** 0.25
    slopes = _alibi_slopes(diff_heads + dil_heads)
    log2e = math.log2(math.e)
    slopes_diff = _split_slopes(slopes[0::2])
    slopes_dil = jnp.asarray(slopes[1::2].astype(np.float64) * log2e, F32)
    row = lambda a: a.reshape(1, -1)

    group = diff_heads * BLOCK
    col_scale = np.ones((1, 6 * group), np.float32)
    col_scale[:, :group] = DIFF_HALF_DIM ** -0.5 * log2e
    col_scale[:, 3 * group:4 * group] = BLOCK ** -0.5 * log2e
    col_scale = jnp.asarray(col_scale)
    kv_scale = jnp.ones((1, 2 * d), F32)

    h = x.reshape(batch * seq, d)
    mem2 = mem.reshape(batch * mem_len, d)
    for l in range(depth):
        lam_init = 0.8 - 0.6 * math.exp(-0.3 * l)
        proj = _project(h, w_in[l].astype(BF16), col_scale, tm=1024, tn=1024).reshape(batch, seq, -1)
        o_diff = _diff_attention(proj, slopes_diff, row(lambda_q1[l]), row(lambda_k1[l]), row(lambda_q2[l]),
                                 row(lambda_k2[l]), row(g_diff[l]), lam_init, diff_heads, t=512)
        o_dil = _dilated_attention(proj, slopes_dil, row(g_dil[l]), dil_heads, first_blk=3 * diff_heads)
        h = _outproj_ln([o_diff.reshape(batch * seq, -1), o_dil.reshape(batch * seq, -1)],
                        w_mix_out[l].astype(BF16), h, row(ln1_g[l]), row(ln1_b[l]), alpha, tm=1024, tr=512)

        kv = _project(mem2, w_mem_kv[l].astype(BF16), kv_scale, tm=1024, tn=1024).reshape(batch, mem_len, 2 * d)
        o_mem = _mem_attention(h, w_mem_q[l].astype(BF16), kv, seq, tm=512)
        h = _outproj_ln([o_mem], w_mem_o[l].astype(BF16), h, row(ln2_g[l]), row(ln2_b[l]), alpha, tm=1024, tr=512)

        h = _conv_ffn_ln(h, w_up[l].astype(BF16), conv_w[l], row(conv_b[l]), w_down[l].astype(BF16),
                         row(ln3_g[l]), row(ln3_b[l]), alpha, seq, tm=512, tf=512)
    return h.reshape(batch, seq, d)
```

```python
import functools
import math

import numpy as np
import jax
import jax.numpy as jnp
from jax import lax
from jax.experimental import pallas as pl
from jax.experimental.pallas import tpu as pltpu

F32 = jnp.float32
BF16 = jnp.bfloat16

LN_EPS = 1e-5
RMS_EPS = 1e-5
BLOCK = 128
DIFF_HALF_DIM = 64
DIL_CONFIGS = ((128, 1), (512, 4), (2048, 16))
TILES_PER_BODY = 16
DIFF_CHAINS = 2
MEM_HEADS = 4
CONV_WIDTH = 3
NEG_INF = float("-inf")

V7X_VMEM_LIMIT_BYTES = 56 * 1024 * 1024


def _params(semantics, vmem_limit_bytes=V7X_VMEM_LIMIT_BYTES):
    return pltpu.CompilerParams(dimension_semantics=semantics, vmem_limit_bytes=vmem_limit_bytes)


def _resident(shape, index_map):
    return pl.BlockSpec(shape, index_map, pipeline_mode=pl.Buffered(1))


def _layer_norm(z, g, b):
    mu = jnp.mean(z, axis=-1, keepdims=True)
    zc = z - mu
    var = jnp.mean(zc * zc, axis=-1, keepdims=True)
    return zc * lax.rsqrt(var + LN_EPS) * g + b


def _rms_norm(o, g):
    return o * lax.rsqrt(jnp.mean(o * o, axis=-1, keepdims=True) + RMS_EPS) * g


def _interleave(*stages):
    order = sorted(((k + 0.5) / len(parts), s, k) for s, parts in enumerate(stages) for k in range(len(parts)))
    for _, s, k in order:
        stages[s][k]()


def _dot(a, b):
    return jnp.dot(a, b, preferred_element_type=F32)


def _dot_nt(a, b):
    return lax.dot_general(a, b, (((1,), (1,)), ((), ())), preferred_element_type=F32)


def _proj_kernel(x_ref, w_ref, cs_ref, o_ref, xb_ref):
    @pl.when(pl.program_id(1) == 0)
    def _():
        xb_ref[...] = x_ref[...].astype(BF16)

    o_ref[...] = (_dot(xb_ref[...], w_ref[...]) * cs_ref[...]).astype(o_ref.dtype)


def _project(x, w, col_scale, tm, tn):
    m, k = x.shape
    n = w.shape[1]
    tm = min(tm, m)
    return pl.pallas_call(
        _proj_kernel,
        grid=(m // tm, n // tn),
        in_specs=[pl.BlockSpec((tm, k), lambda i, j: (i, 0)),
                  pl.BlockSpec((k, tn), lambda i, j: (0, j)),
                  pl.BlockSpec((1, tn), lambda i, j: (0, j))],
        out_specs=pl.BlockSpec((tm, tn), lambda i, j: (i, j)),
        out_shape=jax.ShapeDtypeStruct((m, n), BF16),
        scratch_shapes=[pltpu.VMEM((tm, k), BF16)],
        compiler_params=_params(("arbitrary", "arbitrary")),
    )(x, w, col_scale)


def _diff_attn_kernel(slope_ref, q_ref, k_ref, v_ref, lq1_ref, lk1_ref, lq2_ref, lk2_ref, g_ref, o_ref,
                      vt_ref, *, t, lam_init):
    h = pl.program_id(1)
    slope = slope_ref[h, SLOPE_PARTS]
    n_tiles = vt_ref.shape[0]
    for j in range(n_tiles):
        vt_ref[j, :BLOCK, :] = v_ref[0, j * t:(j + 1) * t, :].astype(F32).T.astype(BF16)
        vt_ref[j, BLOCK:, :] = jnp.ones((vt_ref.shape[1] - BLOCK, t), BF16)

    lam = (jnp.exp(jnp.sum(lq1_ref[...] * lk1_ref[...], axis=-1, keepdims=True))
           - jnp.exp(jnp.sum(lq2_ref[...] * lk2_ref[...], axis=-1, keepdims=True)) + lam_init)
    lane = lax.broadcasted_iota(jnp.int32, (t, BLOCK), 1)
    key_index = lax.broadcasted_iota(jnp.int32, (t, BLOCK), 0)
    low, high = key_index & (V7X_MXU_COLUMNS - 1), key_index >> 8
    assert t <= 2 * V7X_MXU_COLUMNS and V7X_MXU_COLUMNS == 1 << 8
    k_extra = jnp.where(lane < SLOPE_PARTS, low, jnp.where(lane < 2 * SLOPE_PARTS, high, 0)).astype(F32).astype(BF16)
    q_extra = jnp.zeros((2 * t, BLOCK), F32)
    lane2 = lax.broadcasted_iota(jnp.int32, (2 * t, BLOCK), 1)
    for i in range(SLOPE_PARTS):
        q_extra = jnp.where(lane2 == i, slope_ref[h, i], q_extra)
        q_extra = jnp.where(lane2 == SLOPE_PARTS + i, slope_ref[h, i] * float(V7X_MXU_COLUMNS), q_extra)
    q_extra = q_extra.astype(BF16)
    key = lax.broadcasted_iota(jnp.int32, (t, 2 * t), 0)
    query = lax.broadcasted_iota(jnp.int32, (t, 2 * t), 1)
    causal = key <= jnp.where(query >= t, query - t, query)

    def queries(qi):
        q = q_ref[0, qi * t:(qi + 1) * t, :]
        zero = jnp.zeros_like(q)
        qq = jnp.concatenate([jnp.where(lane < DIFF_HALF_DIM, q, zero),
                              jnp.where(lane >= DIFF_HALF_DIM, q, zero)], axis=0)
        return jnp.concatenate([qq, q_extra], axis=1)

    order = sorted(range(n_tiles), key=lambda qi: -qi)
    chains = [[] for _ in range(DIFF_CHAINS)]
    for qi in order:
        min(chains, key=len).extend((qi, j) for j in range(qi + 1))
    state = {}
    for step in range(max(len(c) for c in chains)):
        live = [c[step] for c in chains if step < len(c)]
        for qi, j in live:
            if j == 0:
                state[qi] = dict(qq=queries(qi))
        s = {}
        for qi, j in live:
            k = jnp.concatenate([k_ref[0, j * t:(j + 1) * t, :], k_extra], axis=1)
            s[qi] = _dot_nt(k, state[qi]["qq"])
            if j == qi:
                s[qi] = jnp.where(causal, s[qi], NEG_INF)
        p = {}
        for qi, j in live:
            st = state[qi]
            shift = slope * float((j - qi) * t)
            m_tile = jnp.max(s[qi], axis=0, keepdims=True) + shift
            st["m_old"], st["m"] = st.get("m"), (m_tile if j == 0 else jnp.maximum(st["m"], m_tile))
            p[qi] = jnp.exp2(s[qi] - (st["m"] - shift)).astype(BF16)
        for qi, j in live:
            st = state[qi]
            acc_tile = _dot(vt_ref[j], p[qi])
            st["acc"] = acc_tile if j == 0 else jnp.exp2(st["m_old"] - st["m"]) * st["acc"] + acc_tile
        for qi, j in live:
            if j == qi:
                acc = state.pop(qi)["acc"]
                o = acc[:BLOCK] / acc[BLOCK:BLOCK + 1]
                o = (o[:, :t] - lam * o[:, t:]).T
                o_ref[0, qi * t:(qi + 1) * t, :] = (_rms_norm(o, g_ref[...]) * (1.0 - lam_init)).astype(o_ref.dtype)


def _diff_attention(proj, slopes, lq1, lk1, lq2, lk2, g, lam_init, heads, t):
    b, s, _ = proj.shape
    q_blk, k_blk, v_blk = 0, heads, 2 * heads
    vec = lambda n: pl.BlockSpec((1, n), lambda bi, hi: (0, 0))
    slab = lambda blk: pl.BlockSpec((1, s, BLOCK), lambda bi, hi: (bi, 0, blk + hi))
    return pl.pallas_call(
        functools.partial(_diff_attn_kernel, t=t, lam_init=lam_init),
        grid=(b, heads),
        in_specs=[pl.BlockSpec(memory_space=pltpu.SMEM), slab(q_blk), slab(k_blk), slab(v_blk),
                  vec(DIFF_HALF_DIM), vec(DIFF_HALF_DIM), vec(DIFF_HALF_DIM), vec(DIFF_HALF_DIM), vec(BLOCK)],
        out_specs=slab(0),
        out_shape=jax.ShapeDtypeStruct((b, s, heads * BLOCK), BF16),
        scratch_shapes=[pltpu.VMEM((s // t, BLOCK + BF16_SUBLANES, t), BF16)],
        compiler_params=_params(("arbitrary", "arbitrary")),
    )(slopes, proj, proj, proj, lq1, lk1, lq2, lk2, g)


def _split_slopes(slopes):
    x = np.asarray(slopes, np.float64) * math.log2(math.e)
    x = x.astype(np.float32)
    parts, rest = [], x.copy()
    for _ in range(SLOPE_PARTS):
        part = rest.astype(BF16).astype(np.float32)
        parts.append(part)
        rest = rest - part
    return jnp.asarray(np.stack(parts + [x], axis=1), F32)


def _dilated_attn_kernel(slope_ref, q_ref, k_ref, v_ref, g_ref, o_ref, qf, kf, vf, out_ref, lse_ref,
                         *, seq, tc):
    h = pl.program_id(1)
    slope = slope_ref[h]
    qf[...] = q_ref[0].astype(F32)
    kf[...] = k_ref[0].astype(F32)
    vf[...] = v_ref[0].astype(F32)

    qi = lax.broadcasted_iota(jnp.int32, (BLOCK, 2 * BLOCK), 0)
    kj = lax.broadcasted_iota(jnp.int32, (BLOCK, 2 * BLOCK), 1)
    rel = qi + BLOCK - kj

    def rows(start, size, dil):
        return pl.ds(start, size) if dil == 1 else pl.ds(start, size, stride=dil)

    def tiles(branch, dil, starts, bias):
        nkeys = bias.shape[1]
        q_rows = [rows(q_start, BLOCK, dil) for q_start, _ in starts]
        k_rows = [rows(k_start, nkeys, dil) for _, k_start in starts]
        ones = jnp.ones((nkeys, BLOCK), BF16)
        s = [_dot_nt(qf[qr, :].astype(BF16), kf[kr, :].astype(BF16)) + bias for qr, kr in zip(q_rows, k_rows)]
        m = [jnp.max(si, axis=-1, keepdims=True) for si in s]
        p = [jnp.exp2(si - mi).astype(BF16) for si, mi in zip(s, m)]
        v1 = [jnp.concatenate([vf[kr, :].astype(BF16), ones], axis=1) for kr in k_rows]
        r = [_dot(pi, vi) for pi, vi in zip(p, v1)]
        for qr, mi, ri in zip(q_rows, m, r):
            l = ri[:, BLOCK:]
            out_ref[branch, qr, :] = ri[:, :BLOCK] / l
            lse_ref[branch, qr, :] = mi + jnp.log2(l)

    for branch, (window, dil) in enumerate(DIL_CONFIGS):
        n_back = window // dil
        nb = seq // dil // BLOCK
        step = dil * BLOCK
        valid = (rel >= 0) & (rel <= n_back)
        bias = jnp.where(valid, (-slope * dil) * rel.astype(F32), NEG_INF)
        bias_first = bias[:, BLOCK:]

        def first_tiles(classes, branch=branch, dil=dil, bias_first=bias_first):
            tiles(branch, dil, [(c, c) for c in classes], bias_first)

        def later_tiles(c, blocks, branch=branch, dil=dil, step=step, bias=bias):
            tiles(branch, dil, [(c + n * step, c + (n - 1) * step) for n in blocks], bias)

        if nb == 1:
            def class_group(i, carry, first_tiles=first_tiles):
                first_tiles([i * TILES_PER_BODY + u for u in range(TILES_PER_BODY)])
                return carry
            lax.fori_loop(0, dil // TILES_PER_BODY, class_group, 0)
        elif nb <= TILES_PER_BODY:
            per_body = TILES_PER_BODY // nb
            def class_group(i, carry, first_tiles=first_tiles, later_tiles=later_tiles, nb=nb, per_body=per_body):
                classes = [i * per_body + u for u in range(per_body)]
                first_tiles(classes)
                for c in classes:
                    later_tiles(c, range(1, nb))
                return carry
            lax.fori_loop(0, dil // per_body, class_group, 0)
        else:
            group = TILES_PER_BODY - 1
            assert dil == 1 and (nb - 1) % group == 0
            first_tiles([0])
            def block_group(i, carry, later_tiles=later_tiles, group=group):
                later_tiles(0, [1 + i * group + u for u in range(group)])
                return carry
            lax.fori_loop(0, (nb - 1) // group, block_group, 0)

    g = g_ref[...]
    for t in range(seq // tc):
        r = pl.ds(t * tc, tc)
        lses = [lse_ref[i, r, :] for i in range(len(DIL_CONFIGS))]
        top = functools.reduce(jnp.maximum, lses)
        ws = [jnp.exp2(lse - top) for lse in lses]
        num = sum(w * out_ref[i, r, :] for i, w in enumerate(ws))
        o_ref[0, r, :] = _rms_norm(num / sum(ws), g).astype(o_ref.dtype)


def _dilated_attention(proj, slopes, g, heads, first_blk):
    b, s, _ = proj.shape
    nbr = len(DIL_CONFIGS)
    spec = lambda blk: pl.BlockSpec((1, s, BLOCK), lambda bi, hi: (bi, 0, blk + hi))
    return pl.pallas_call(
        functools.partial(_dilated_attn_kernel, seq=s, tc=BLOCK),
        grid=(b, heads),
        in_specs=[pl.BlockSpec(memory_space=pltpu.SMEM),
                  spec(first_blk), spec(first_blk + heads), spec(first_blk + 2 * heads),
                  pl.BlockSpec((1, BLOCK), lambda bi, hi: (0, 0))],
        out_specs=pl.BlockSpec((1, s, BLOCK), lambda bi, hi: (bi, 0, hi)),
        out_shape=jax.ShapeDtypeStruct((b, s, heads * BLOCK), BF16),
        scratch_shapes=[pltpu.VMEM((s, BLOCK), F32), pltpu.VMEM((s, BLOCK), F32), pltpu.VMEM((s, BLOCK), F32),
                        pltpu.VMEM((nbr, s, BLOCK), F32), pltpu.VMEM((nbr, s, BLOCK), F32)],
        compiler_params=_params(("arbitrary", "arbitrary")),
    )(slopes, proj, proj, proj, g)


def _outproj_ln_kernel(*refs, n_in, alpha, tr):
    o_refs, w_refs = refs[:n_in], refs[n_in:2 * n_in]
    h_ref, g_ref, b_ref, out_ref = refs[2 * n_in:]
    tm, d = out_ref.shape

    def matmul_parts(r0):
        def part(c0):
            rows, cols = slice(r0, r0 + tr), slice(c0, c0 + V7X_MXU_COLUMNS)
            y = alpha * h_ref[rows, cols]
            for o_ref, w_ref in zip(o_refs, w_refs):
                y = y + _dot(o_ref[rows, :], w_ref[:, cols])
            out_ref[rows, cols] = y
        return [functools.partial(part, c0) for c0 in range(0, d, V7X_MXU_COLUMNS)]

    def norm_parts(r0):
        def part(r):
            rows = slice(r, r + FFN_ROW_CHUNK)
            out_ref[rows, :] = _layer_norm(out_ref[rows, :], g_ref[...], b_ref[...])
        return [functools.partial(part, r) for r in range(r0, r0 + tr, FFN_ROW_CHUNK)]

    pending = []
    for r0 in range(0, tm, tr):
        _interleave(matmul_parts(r0), pending)
        pending = norm_parts(r0)
    _interleave(pending)


def _outproj_ln(os, w, h, g, b, alpha, tm, tr):
    m, d = h.shape
    n_in = len(os)
    kw = w.shape[0] // n_in
    return pl.pallas_call(
        functools.partial(_outproj_ln_kernel, n_in=n_in, alpha=alpha, tr=tr),
        grid=(m // tm,),
        in_specs=([pl.BlockSpec((tm, kw), lambda i: (i, 0)) for _ in os]
                  + [_resident((kw, d), functools.partial(lambda i, r: (r, 0), r=r)) for r in range(n_in)]
                  + [pl.BlockSpec((tm, d), lambda i: (i, 0)),
                     pl.BlockSpec((1, d), lambda i: (0, 0)), pl.BlockSpec((1, d), lambda i: (0, 0))]),
        out_specs=pl.BlockSpec((tm, d), lambda i: (i, 0)),
        out_shape=jax.ShapeDtypeStruct((m, d), F32),
        compiler_params=_params(("arbitrary",)),
    )(*os, *([w] * n_in), h, g, b)


def _mem_attn_kernel(h_ref, wq_ref, kv_ref, o_ref, *, heads):
    d = h_ref.shape[1]
    hd = d // heads
    q = _dot(h_ref[...].astype(BF16), wq_ref[...]).astype(BF16)
    scale = hd ** -0.5
    for i in range(heads):
        k = kv_ref[0, :, i * hd:(i + 1) * hd]
        v = kv_ref[0, :, d + i * hd:d + (i + 1) * hd]
        s = _dot_nt(q[:, i * hd:(i + 1) * hd], k) * scale
        e = jnp.exp(s - jnp.max(s, axis=-1, keepdims=True))
        p = e * (1.0 / jnp.sum(e, axis=-1, keepdims=True))
        o_ref[:, i * hd:(i + 1) * hd] = _dot(p.astype(BF16), v).astype(o_ref.dtype)


def _mem_attention(h, wq, kv, seq, tm):
    m, d = h.shape
    mem_len = kv.shape[1]
    per_seq = seq // tm
    return pl.pallas_call(
        functools.partial(_mem_attn_kernel, heads=MEM_HEADS),
        grid=(m // tm,),
        in_specs=[pl.BlockSpec((tm, d), lambda i: (i, 0)),
                  _resident((d, d), lambda i: (0, 0)),
                  pl.BlockSpec((1, mem_len, 2 * d), lambda i: (i // per_seq, 0, 0))],
        out_specs=pl.BlockSpec((tm, d), lambda i: (i, 0)),
        out_shape=jax.ShapeDtypeStruct((m, d), BF16),
        compiler_params=_params(("arbitrary",)),
    )(h, wq, kv)


HALO = 8
FFN_ROW_CHUNK = 64
V7X_MXU_COLUMNS = 256
BF16_SUBLANES = 16
SLOPE_PARTS = 3


def _ffn_kernel(x_ref, wg_ref, wu_ref, cwg_ref, cwu_ref, cbg_ref, cbu_ref, wd_ref, g_ref, b_ref, o_ref,
                xb_ref, hg0_ref, hu0_ref, hg1_ref, hu1_ref, act0_ref, act1_ref, carry_g_ref, carry_u_ref,
                *, tm, tiles_per_seq, alpha):
    i = pl.program_id(0)
    f = pl.program_id(1)
    nf = pl.num_programs(1) - 2
    seq_start = (i % tiles_per_seq) == 0
    h_refs = ((hg0_ref, hu0_ref), (hg1_ref, hu1_ref))
    act_refs = (act0_ref, act1_ref)
    acc_ref = o_ref

    tf = wg_ref.shape[1]
    d = wd_ref.shape[1]

    def up_project(slot):
        def part(which, c0):
            w_ref, hs_ref, carry_ref = ((wg_ref, wu_ref)[which], h_refs[slot][which],
                                        (carry_g_ref, carry_u_ref)[which])
            cols = slice(c0, c0 + V7X_MXU_COLUMNS)
            hs_ref[HALO:, cols] = _dot(xb_ref[...], w_ref[:, cols])
            hs_ref[:HALO, cols] = jnp.where(seq_start, 0.0, carry_ref[f, :, cols])
            carry_ref[f, :, cols] = hs_ref[tm:, cols]
        return [functools.partial(part, which, c0) for which in range(2)
                for c0 in range(0, tf, V7X_MXU_COLUMNS)]

    def conv(hs_ref, cw_ref, cb_ref, row0, cols):
        out = cb_ref[:, cols]
        for tap in range(CONV_WIDTH):
            off = row0 + HALO - (CONV_WIDTH - 1) + tap
            out = out + hs_ref[off:off + FFN_ROW_CHUNK, cols] * cw_ref[tap:tap + 1, cols]
        return out

    def activate(slot):
        def part(c0, row0):
            cols = slice(c0, c0 + BLOCK)
            gate = conv(h_refs[slot][0], cwg_ref, cbg_ref, row0, cols)
            up = conv(h_refs[slot][1], cwu_ref, cbu_ref, row0, cols)
            act = gate * (1.0 / (1.0 + jnp.exp2(gate * -math.log2(math.e)))) * up
            act_refs[slot][row0:row0 + FFN_ROW_CHUNK, cols] = act.astype(BF16)
        return [functools.partial(part, c0, row0) for c0 in range(0, tf, BLOCK)
                for row0 in range(0, tm, FFN_ROW_CHUNK)]

    def down_project(slot):
        def part(c0):
            cols = slice(c0, c0 + V7X_MXU_COLUMNS)
            acc_ref[:, cols] += _dot(act_refs[slot][...], wd_ref[:, cols])
        return [functools.partial(part, c0) for c0 in range(0, d, V7X_MXU_COLUMNS)]

    run = _interleave

    @pl.when(f == 0)
    def _():
        @pl.when(i == 0)
        def _():
            carry_g_ref[...] = jnp.zeros(carry_g_ref.shape, F32)
            carry_u_ref[...] = jnp.zeros(carry_u_ref.shape, F32)

        xb_ref[...] = x_ref[...].astype(BF16)
        acc_ref[...] = jnp.zeros(acc_ref.shape, F32)
        run(up_project(0))

    @pl.when(f == 1)
    def _():
        run(up_project(1), activate(0))

    for parity in range(2):
        @pl.when((f > 1) & (f < nf) & (f % 2 == parity))
        def _(parity=parity):
            run(up_project(parity), activate(1 - parity), down_project(parity))

    @pl.when(f == nf)
    def _():
        run(down_project((nf - 2) % 2), activate((nf - 1) % 2))

    @pl.when(f == nf + 1)
    def _():
        run(down_project((nf - 1) % 2))
        o_ref[...] = _layer_norm(alpha * x_ref[...] + acc_ref[...], g_ref[...], b_ref[...])


def _conv_ffn_ln(x, w_up, conv_w, conv_b, w_down, g, b, alpha, seq, tm, tf):
    m, d = x.shape
    d_ff = w_down.shape[0]
    nf = d_ff // tf
    row = lambda i, f: (i, 0)
    fixed = lambda i, f: (0, 0)
    chunk = lambda f, lag: jnp.clip(f - lag, 0, nf - 1)
    h_buf = pltpu.VMEM((tm + HALO, tf), F32)
    act_buf = pltpu.VMEM((tm, tf), BF16)
    return pl.pallas_call(
        functools.partial(_ffn_kernel, tm=tm, tiles_per_seq=seq // tm, alpha=alpha),
        grid=(m // tm, nf + 2),
        in_specs=[pl.BlockSpec((tm, d), row),
                  pl.BlockSpec((d, tf), lambda i, f: (0, chunk(f, 0))),
                  pl.BlockSpec((d, tf), lambda i, f: (0, nf + chunk(f, 0))),
                  pl.BlockSpec((CONV_WIDTH, tf), lambda i, f: (0, chunk(f, 1))),
                  pl.BlockSpec((CONV_WIDTH, tf), lambda i, f: (0, nf + chunk(f, 1))),
                  pl.BlockSpec((1, tf), lambda i, f: (0, chunk(f, 1))),
                  pl.BlockSpec((1, tf), lambda i, f: (0, nf + chunk(f, 1))),
                  pl.BlockSpec((tf, d), lambda i, f: (chunk(f, 2), 0)),
                  pl.BlockSpec((1, d), fixed), pl.BlockSpec((1, d), fixed)],
        out_specs=pl.BlockSpec((tm, d), row),
        out_shape=jax.ShapeDtypeStruct((m, d), F32),
        scratch_shapes=[pltpu.VMEM((tm, d), BF16), h_buf, h_buf, h_buf, h_buf, act_buf, act_buf,
                        pltpu.VMEM((nf, HALO, tf), F32), pltpu.VMEM((nf, HALO, tf), F32)],
        compiler_params=_params(("arbitrary", "arbitrary")),
    )(x, w_up, w_up, conv_w, conv_w, conv_b, conv_b, w_down, g, b)


def _alibi_slopes(n):
    return (2.0 ** (-8.0 * np.arange(1, n + 1) / n)).astype(np.float32)


def kernel(x, mem, w_in, w_mix_out, lambda_q1, lambda_k1, lambda_q2, lambda_k2, g_diff, g_dil, ln1_g, ln1_b,
           w_mem_q, w_mem_kv, w_mem_o, ln2_g, ln2_b, w_up, conv_w, conv_b, w_down, ln3_g, ln3_b):
    batch, seq, d = x.shape
    depth = w_in.shape[0]
    mem_len = mem.shape[1]
    diff_heads = dil_heads = d // 256
    alpha = (2 * depth) ** 0.25
    slopes = _alibi_slopes(diff_heads + dil_heads)
    log2e = math.log2(math.e)
    slopes_diff = _split_slopes(slopes[0::2])
    slopes_dil = jnp.asarray(slopes[1::2].astype(np.float64) * log2e, F32)
    row = lambda a: a.reshape(1, -1)

    group = diff_heads * BLOCK
    col_scale = np.ones((1, 6 * group), np.float32)
    col_scale[:, :group] = DIFF_HALF_DIM ** -0.5 * log2e
    col_scale[:, 3 * group:4 * group] = BLOCK ** -0.5 * log2e
    col_scale = jnp.asarray(col_scale)
    kv_scale = jnp.ones((1, 2 * d), F32)

    h = x.reshape(batch * seq, d)
    mem2 = mem.reshape(batch * mem_len, d)
    for l in range(depth):
        lam_init = 0.8 - 0.6 * math.exp(-0.3 * l)
        proj = _project(h, w_in[l].astype(BF16), col_scale, tm=1024, tn=1024).reshape(batch, seq, -1)
        o_diff = _diff_attention(proj, slopes_diff, row(lambda_q1[l]), row(lambda_k1[l]), row(lambda_q2[l]),
                                 row(lambda_k2[l]), row(g_diff[l]), lam_init, diff_heads, t=512)
        o_dil = _dilated_attention(proj, slopes_dil, row(g_dil[l]), dil_heads, first_blk=3 * diff_heads)
        h = _outproj_ln([o_diff.reshape(batch * seq, -1), o_dil.reshape(batch * seq, -1)],
                        w_mix_out[l].astype(BF16), h, row(ln1_g[l]), row(ln1_b[l]), alpha, tm=1024, tr=512)

        kv = _project(mem2, w_mem_kv[l].astype(BF16), kv_scale, tm=1024, tn=1024).reshape(batch, mem_len, 2 * d)
        o_mem = _mem_attention(h, w_mem_q[l].astype(BF16), kv, seq, tm=512)
        h = _outproj_ln([o_mem], w_mem_o[l].astype(BF16), h, row(ln2_g[l]), row(ln2_b[l]), alpha, tm=1024, tr=512)

        h = _conv_ffn_ln(h, w_up[l].astype(BF16), conv_w[l], row(conv_b[l]), w_down[l].astype(BF16),
                         row(ln3_g[l]), row(ln3_b[l]), alpha, seq, tm=512, tf=512)
    return h.reshape(batch, seq, d)
```

```python
import functools
import math

import numpy as np
import jax
import jax.numpy as jnp
from jax import lax
from jax.experimental import pallas as pl
from jax.experimental.pallas import tpu as pltpu

F32 = jnp.float32
BF16 = jnp.bfloat16

LN_EPS = 1e-5
RMS_EPS = 1e-5
BLOCK = 128
DIFF_HALF_DIM = 64
DIL_CONFIGS = ((128, 1), (512, 4), (2048, 16))
TILES_PER_BODY = 16
DIFF_CHAINS = 2
MEM_HEADS = 4
CONV_WIDTH = 3
NEG_INF = float("-inf")

V7X_VMEM_LIMIT_BYTES = 56 * 1024 * 1024


def _params(semantics, vmem_limit_bytes=V7X_VMEM_LIMIT_BYTES):
    return pltpu.CompilerParams(dimension_semantics=semantics, vmem_limit_bytes=vmem_limit_bytes)


def _resident(shape, index_map):
    return pl.BlockSpec(shape, index_map, pipeline_mode=pl.Buffered(1))


def _layer_norm(z, g, b):
    mu = jnp.mean(z, axis=-1, keepdims=True)
    zc = z - mu
    var = jnp.mean(zc * zc, axis=-1, keepdims=True)
    return zc * lax.rsqrt(var + LN_EPS) * g + b


def _rms_norm(o, g):
    return o * lax.rsqrt(jnp.mean(o * o, axis=-1, keepdims=True) + RMS_EPS) * g


def _interleave(*stages):
    order = sorted(((k + 0.5) / len(parts), s, k) for s, parts in enumerate(stages) for k in range(len(parts)))
    for _, s, k in order:
        stages[s][k]()


def _dot(a, b):
    return jnp.dot(a, b, preferred_element_type=F32)


def _dot_nt(a, b):
    return lax.dot_general(a, b, (((1,), (1,)), ((), ())), preferred_element_type=F32)


def _proj_kernel(x_ref, w_ref, cs_ref, o_ref, xb_ref):
    @pl.when(pl.program_id(1) == 0)
    def _():
        xb_ref[...] = x_ref[...].astype(BF16)

    o_ref[...] = (_dot(xb_ref[...], w_ref[...]) * cs_ref[...]).astype(o_ref.dtype)


def _project(x, w, col_scale, tm, tn):
    m, k = x.shape
    n = w.shape[1]
    tm = min(tm, m)
    return pl.pallas_call(
        _proj_kernel,
        grid=(m // tm, n // tn),
        in_specs=[pl.BlockSpec((tm, k), lambda i, j: (i, 0)),
                  pl.BlockSpec((k, tn), lambda i, j: (0, j)),
                  pl.BlockSpec((1, tn), lambda i, j: (0, j))],
        out_specs=pl.BlockSpec((tm, tn), lambda i, j: (i, j)),
        out_shape=jax.ShapeDtypeStruct((m, n), BF16),
        scratch_shapes=[pltpu.VMEM((tm, k), BF16)],
        compiler_params=_params(("arbitrary", "arbitrary")),
    )(x, w, col_scale)


def _diff_attn_kernel(slope_ref, q_ref, k_ref, v_ref, lq1_ref, lk1_ref, lq2_ref, lk2_ref, g_ref, o_ref,
                      vt_ref, *, t, lam_init):
    h = pl.program_id(1)
    slope = slope_ref[h, SLOPE_PARTS]
    n_tiles = vt_ref.shape[0]
    for j in range(n_tiles):
        vt_ref[j, :BLOCK, :] = v_ref[0, j * t:(j + 1) * t, :].astype(F32).T.astype(BF16)
        vt_ref[j, BLOCK:, :] = jnp.ones((vt_ref.shape[1] - BLOCK, t), BF16)

    lam = (jnp.exp(jnp.sum(lq1_ref[...] * lk1_ref[...], axis=-1, keepdims=True))
           - jnp.exp(jnp.sum(lq2_ref[...] * lk2_ref[...], axis=-1, keepdims=True)) + lam_init)
    lane = lax.broadcasted_iota(jnp.int32, (t, BLOCK), 1)
    key_index = lax.broadcasted_iota(jnp.int32, (t, BLOCK), 0)
    low, high = key_index & (V7X_MXU_COLUMNS - 1), key_index >> 8
    assert t <= 2 * V7X_MXU_COLUMNS and V7X_MXU_COLUMNS == 1 << 8
    k_extra = jnp.where(lane < SLOPE_PARTS, low, jnp.where(lane < 2 * SLOPE_PARTS, high, 0)).astype(F32).astype(BF16)
    q_extra = jnp.zeros((2 * t, BLOCK), F32)
    lane2 = lax.broadcasted_iota(jnp.int32, (2 * t, BLOCK), 1)
    for i in range(SLOPE_PARTS):
        q_extra = jnp.where(lane2 == i, slope_ref[h, i], q_extra)
        q_extra = jnp.where(lane2 == SLOPE_PARTS + i, slope_ref[h, i] * float(V7X_MXU_COLUMNS), q_extra)
    q_extra = q_extra.astype(BF16)
    key = lax.broadcasted_iota(jnp.int32, (t, 2 * t), 0)
    query = lax.broadcasted_iota(jnp.int32, (t, 2 * t), 1)
    causal = key <= jnp.where(query >= t, query - t, query)

    def queries(qi):
        q = q_ref[0, qi * t:(qi + 1) * t, :]
        zero = jnp.zeros_like(q)
        qq = jnp.concatenate([jnp.where(lane < DIFF_HALF_DIM, q, zero),
                              jnp.where(lane >= DIFF_HALF_DIM, q, zero)], axis=0)
        return jnp.concatenate([qq, q_extra], axis=1)

    order = sorted(range(n_tiles), key=lambda qi: -qi)
    chains = [[] for _ in range(DIFF_CHAINS)]
    for qi in order:
        min(chains, key=len).extend((qi, j) for j in range(qi + 1))
    state = {}
    for step in range(max(len(c) for c in chains)):
        live = [c[step] for c in chains if step < len(c)]
        for qi, j in live:
            if j == 0:
                state[qi] = dict(qq=queries(qi))
        s = {}
        for qi, j in live:
            k = jnp.concatenate([k_ref[0, j * t:(j + 1) * t, :], k_extra], axis=1)
            s[qi] = _dot_nt(k, state[qi]["qq"])
            if j == qi:
                s[qi] = jnp.where(causal, s[qi], NEG_INF)
        p = {}
        for qi, j in live:
            st = state[qi]
            shift = slope * float((j - qi) * t)
            m_tile = jnp.max(s[qi], axis=0, keepdims=True) + shift
            st["m_old"], st["m"] = st.get("m"), (m_tile if j == 0 else jnp.maximum(st["m"], m_tile))
            p[qi] = jnp.exp2(s[qi] - (st["m"] - shift)).astype(BF16)
        for qi, j in live:
            st = state[qi]
            acc_tile = _dot(vt_ref[j], p[qi])
            st["acc"] = acc_tile if j == 0 else jnp.exp2(st["m_old"] - st["m"]) * st["acc"] + acc_tile
        for qi, j in live:
            if j == qi:
                acc = state.pop(qi)["acc"]
                o = acc[:BLOCK] / acc[BLOCK:BLOCK + 1]
                o = (o[:, :t] - lam * o[:, t:]).T
                o_ref[0, qi * t:(qi + 1) * t, :] = (_rms_norm(o, g_ref[...]) * (1.0 - lam_init)).astype(o_ref.dtype)


def _diff_attention(proj, slopes, lq1, lk1, lq2, lk2, g, lam_init, heads, t):
    b, s, _ = proj.shape
    q_blk, k_blk, v_blk = 0, heads, 2 * heads
    vec = lambda n: pl.BlockSpec((1, n), lambda bi, hi: (0, 0))
    slab = lambda blk: pl.BlockSpec((1, s, BLOCK), lambda bi, hi: (bi, 0, blk + hi))
    return pl.pallas_call(
        functools.partial(_diff_attn_kernel, t=t, lam_init=lam_init),
        grid=(b, heads),
        in_specs=[pl.BlockSpec(memory_space=pltpu.SMEM), slab(q_blk), slab(k_blk), slab(v_blk),
                  vec(DIFF_HALF_DIM), vec(DIFF_HALF_DIM), vec(DIFF_HALF_DIM), vec(DIFF_HALF_DIM), vec(BLOCK)],
        out_specs=slab(0),
        out_shape=jax.ShapeDtypeStruct((b, s, heads * BLOCK), BF16),
        scratch_shapes=[pltpu.VMEM((s // t, BLOCK + BF16_SUBLANES, t), BF16)],
        compiler_params=_params(("arbitrary", "arbitrary")),
    )(slopes, proj, proj, proj, lq1, lk1, lq2, lk2, g)


def _split_slopes(slopes):
    x = np.asarray(slopes, np.float64) * math.log2(math.e)
    x = x.astype(np.float32)
    parts, rest = [], x.copy()
    for _ in range(SLOPE_PARTS):
        part = rest.astype(BF16).astype(np.float32)
        parts.append(part)
        rest = rest - part
    return jnp.asarray(np.stack(parts + [x], axis=1), F32)


def _dilated_attn_kernel(slope_ref, q_ref, k_ref, v_ref, g_ref, o_ref, qf, kf, vf, out_ref, lse_ref,
                         *, seq, tc):
    h = pl.program_id(1)
    slope = slope_ref[h]
    qf[...] = q_ref[0].astype(F32)
    kf[...] = k_ref[0].astype(F32)
    vf[...] = v_ref[0].astype(F32)

    qi = lax.broadcasted_iota(jnp.int32, (BLOCK, 2 * BLOCK), 0)
    kj = lax.broadcasted_iota(jnp.int32, (BLOCK, 2 * BLOCK), 1)
    rel = qi + BLOCK - kj

    def rows(start, size, dil):
        return pl.ds(start, size) if dil == 1 else pl.ds(start, size, stride=dil)

    def tiles(branch, dil, starts, bias):
        nkeys = bias.shape[1]
        q_rows = [rows(q_start, BLOCK, dil) for q_start, _ in starts]
        k_rows = [rows(k_start, nkeys, dil) for _, k_start in starts]
        ones = jnp.ones((nkeys, BLOCK), BF16)
        s = [_dot_nt(qf[qr, :].astype(BF16), kf[kr, :].astype(BF16)) + bias for qr, kr in zip(q_rows, k_rows)]
        m = [jnp.max(si, axis=-1, keepdims=True) for si in s]
        p = [jnp.exp2(si - mi).astype(BF16) for si, mi in zip(s, m)]
        v1 = [jnp.concatenate([vf[kr, :].astype(BF16), ones], axis=1) for kr in k_rows]
        r = [_dot(pi, vi) for pi, vi in zip(p, v1)]
        for qr, mi, ri in zip(q_rows, m, r):
            l = ri[:, BLOCK:]
            out_ref[branch, qr, :] = ri[:, :BLOCK] / l
            lse_ref[branch, qr, :] = mi + jnp.log2(l)

    for branch, (window, dil) in enumerate(DIL_CONFIGS):
        n_back = window // dil
        nb = seq // dil // BLOCK
        step = dil * BLOCK
        valid = (rel >= 0) & (rel <= n_back)
        bias = jnp.where(valid, (-slope * dil) * rel.astype(F32), NEG_INF)
        bias_first = bias[:, BLOCK:]

        def first_tiles(classes, branch=branch, dil=dil, bias_first=bias_first):
            tiles(branch, dil, [(c, c) for c in classes], bias_first)

        def later_tiles(c, blocks, branch=branch, dil=dil, step=step, bias=bias):
            tiles(branch, dil, [(c + n * step, c + (n - 1) * step) for n in blocks], bias)

        if nb == 1:
            def class_group(i, carry, first_tiles=first_tiles):
                first_tiles([i * TILES_PER_BODY + u for u in range(TILES_PER_BODY)])
                return carry
            lax.fori_loop(0, dil // TILES_PER_BODY, class_group, 0)
        elif nb <= TILES_PER_BODY:
            per_body = TILES_PER_BODY // nb
            def class_group(i, carry, first_tiles=first_tiles, later_tiles=later_tiles, nb=nb, per_body=per_body):
                classes = [i * per_body + u for u in range(per_body)]
                first_tiles(classes)
                for c in classes:
                    later_tiles(c, range(1, nb))
                return carry
            lax.fori_loop(0, dil // per_body, class_group, 0)
        else:
            group = TILES_PER_BODY - 1
            assert dil == 1 and (nb - 1) % group == 0
            first_tiles([0])
            def block_group(i, carry, later_tiles=later_tiles, group=group):
                later_tiles(0, [1 + i * group + u for u in range(group)])
                return carry
            lax.fori_loop(0, (nb - 1) // group, block_group, 0)

    g = g_ref[...]
    for t in range(seq // tc):
        r = pl.ds(t * tc, tc)
        lses = [lse_ref[i, r, :] for i in range(len(DIL_CONFIGS))]
        top = functools.reduce(jnp.maximum, lses)
        ws = [jnp.exp2(lse - top) for lse in lses]
        num = sum(w * out_ref[i, r, :] for i, w in enumerate(ws))
        o_ref[0, r, :] = _rms_norm(num / sum(ws), g).astype(o_ref.dtype)


def _dilated_attention(proj, slopes, g, heads, first_blk):
    b, s, _ = proj.shape
    nbr = len(DIL_CONFIGS)
    spec = lambda blk: pl.BlockSpec((1, s, BLOCK), lambda bi, hi: (bi, 0, blk + hi))
    return pl.pallas_call(
        functools.partial(_dilated_attn_kernel, seq=s, tc=BLOCK),
        grid=(b, heads),
        in_specs=[pl.BlockSpec(memory_space=pltpu.SMEM),
                  spec(first_blk), spec(first_blk + heads), spec(first_blk + 2 * heads),
                  pl.BlockSpec((1, BLOCK), lambda bi, hi: (0, 0))],
        out_specs=pl.BlockSpec((1, s, BLOCK), lambda bi, hi: (bi, 0, hi)),
        out_shape=jax.ShapeDtypeStruct((b, s, heads * BLOCK), BF16),
        scratch_shapes=[pltpu.VMEM((s, BLOCK), F32), pltpu.VMEM((s, BLOCK), F32), pltpu.VMEM((s, BLOCK), F32),
                        pltpu.VMEM((nbr, s, BLOCK), F32), pltpu.VMEM((nbr, s, BLOCK), F32)],
        compiler_params=_params(("arbitrary", "arbitrary")),
    )(slopes, proj, proj, proj, g)


def _diff_parts(slope_ref, q_ref, k_ref, v_ref, lq1_ref, lk1_ref, lq2_ref, lk2_ref, g_ref, o_ref, vt_ref,
                *, t, lam_init):
    c = {}
    n_tiles = vt_ref.shape[0]

    def setup():
        h = pl.program_id(1)
        c["slope"] = slope_ref[h, SLOPE_PARTS]
        for j in range(n_tiles):
            vt_ref[j, :BLOCK, :] = v_ref[0, j * t:(j + 1) * t, :].astype(F32).T.astype(BF16)
            vt_ref[j, BLOCK:, :] = jnp.ones((vt_ref.shape[1] - BLOCK, t), BF16)
        c["lam"] = (jnp.exp(jnp.sum(lq1_ref[...] * lk1_ref[...], axis=-1, keepdims=True))
                    - jnp.exp(jnp.sum(lq2_ref[...] * lk2_ref[...], axis=-1, keepdims=True)) + lam_init)
        c["lane"] = lax.broadcasted_iota(jnp.int32, (t, BLOCK), 1)
        key_index = lax.broadcasted_iota(jnp.int32, (t, BLOCK), 0)
        low, high = key_index & (V7X_MXU_COLUMNS - 1), key_index >> 8
        c["k_extra"] = jnp.where(c["lane"] < SLOPE_PARTS, low,
                                 jnp.where(c["lane"] < 2 * SLOPE_PARTS, high, 0)).astype(F32).astype(BF16)
        q_extra = jnp.zeros((2 * t, BLOCK), F32)
        lane2 = lax.broadcasted_iota(jnp.int32, (2 * t, BLOCK), 1)
        for i in range(SLOPE_PARTS):
            q_extra = jnp.where(lane2 == i, slope_ref[h, i], q_extra)
            q_extra = jnp.where(lane2 == SLOPE_PARTS + i, slope_ref[h, i] * float(V7X_MXU_COLUMNS), q_extra)
        c["q_extra"] = q_extra.astype(BF16)
        key = lax.broadcasted_iota(jnp.int32, (t, 2 * t), 0)
        query = lax.broadcasted_iota(jnp.int32, (t, 2 * t), 1)
        c["causal"] = key <= jnp.where(query >= t, query - t, query)

    def queries(qi):
        q = q_ref[0, qi * t:(qi + 1) * t, :]
        zero = jnp.zeros_like(q)
        qq = jnp.concatenate([jnp.where(c["lane"] < DIFF_HALF_DIM, q, zero),
                              jnp.where(c["lane"] >= DIFF_HALF_DIM, q, zero)], axis=0)
        return jnp.concatenate([qq, c["q_extra"]], axis=1)

    order = sorted(range(n_tiles), key=lambda qi: -qi)
    chains = [[] for _ in range(DIFF_CHAINS)]
    for qi in order:
        min(chains, key=len).extend((qi, j) for j in range(qi + 1))
    state = {}
    parts = [setup]
    for step in range(max(len(ch) for ch in chains)):
        live = [ch[step] for ch in chains if step < len(ch)]
        s, p = {}, {}

        def score(live=live, s=s):
            for qi, j in live:
                if j == 0:
                    state[qi] = dict(qq=queries(qi))
                k = jnp.concatenate([k_ref[0, j * t:(j + 1) * t, :], c["k_extra"]], axis=1)
                s[qi] = _dot_nt(k, state[qi]["qq"])
                if j == qi:
                    s[qi] = jnp.where(c["causal"], s[qi], NEG_INF)

        def softmax(live=live, s=s, p=p):
            for qi, j in live:
                st = state[qi]
                shift = c["slope"] * float((j - qi) * t)
                m_tile = jnp.max(s[qi], axis=0, keepdims=True) + shift
                st["m_old"], st["m"] = st.get("m"), (m_tile if j == 0 else jnp.maximum(st["m"], m_tile))
                p[qi] = jnp.exp2(s[qi] - (st["m"] - shift)).astype(BF16)

        def values(live=live, p=p):
            for qi, j in live:
                st = state[qi]
                acc_tile = _dot(vt_ref[j], p[qi])
                st["acc"] = acc_tile if j == 0 else jnp.exp2(st["m_old"] - st["m"]) * st["acc"] + acc_tile

        def finish(live=live):
            for qi, j in live:
                if j == qi:
                    acc = state.pop(qi)["acc"]
                    o = acc[:BLOCK] / acc[BLOCK:BLOCK + 1]
                    o = (o[:, :t] - c["lam"] * o[:, t:]).T
                    o_ref[0, qi * t:(qi + 1) * t, :] = (_rms_norm(o, g_ref[...])
                                                        * (1.0 - lam_init)).astype(o_ref.dtype)

        parts += [score, softmax, values, finish]
    return parts


def _dilated_parts(slope_ref, q_ref, k_ref, v_ref, g_ref, o_ref, qf, kf, vf, out_ref, lse_ref, *, seq, tc):
    c = {}

    def setup():
        c["slope"] = slope_ref[pl.program_id(1)]
        qf[...] = q_ref[0].astype(F32)
        kf[...] = k_ref[0].astype(F32)
        vf[...] = v_ref[0].astype(F32)
        qi = lax.broadcasted_iota(jnp.int32, (BLOCK, 2 * BLOCK), 0)
        kj = lax.broadcasted_iota(jnp.int32, (BLOCK, 2 * BLOCK), 1)
        c["rel"] = qi + BLOCK - kj

    def rows(start, size, dil):
        return pl.ds(start, size) if dil == 1 else pl.ds(start, size, stride=dil)

    parts = [setup]
    for branch, (window, dil) in enumerate(DIL_CONFIGS):
        nb = seq // dil // BLOCK
        step = dil * BLOCK

        def biases(branch=branch, dil=dil, n_back=window // dil):
            valid = (c["rel"] >= 0) & (c["rel"] <= n_back)
            c["bias", branch] = jnp.where(valid, (-c["slope"] * dil) * c["rel"].astype(F32), NEG_INF)

        parts.append(biases)
        groups = [([(cls, cls) for cls in range(dil)], True)]
        if nb > 1:
            groups.append(([(cls + n * step, cls + (n - 1) * step) for cls in range(dil) for n in range(1, nb)],
                           False))
        for starts, first in groups:
            nkeys = BLOCK if first else 2 * BLOCK
            q_rows = [rows(q_start, BLOCK, dil) for q_start, _ in starts]
            k_rows = [rows(k_start, nkeys, dil) for _, k_start in starts]
            v = {}

            def score(branch=branch, first=first, q_rows=q_rows, k_rows=k_rows, v=v):
                bias = c["bias", branch][:, BLOCK:] if first else c["bias", branch]
                v["s"] = [_dot_nt(qf[qr, :].astype(BF16), kf[kr, :].astype(BF16)) + bias
                          for qr, kr in zip(q_rows, k_rows)]

            def peak(v=v):
                v["m"] = [jnp.max(si, axis=-1, keepdims=True) for si in v["s"]]

            def weights(v=v):
                v["p"] = [jnp.exp2(si - mi).astype(BF16) for si, mi in zip(v["s"], v["m"])]

            def values(k_rows=k_rows, nkeys=nkeys, v=v):
                ones = jnp.ones((nkeys, BLOCK), BF16)
                v["r"] = [_dot(pi, jnp.concatenate([vf[kr, :].astype(BF16), ones], axis=1))
                          for pi, kr in zip(v["p"], k_rows)]

            def store(branch=branch, q_rows=q_rows, v=v):
                for qr, mi, ri in zip(q_rows, v["m"], v["r"]):
                    l = ri[:, BLOCK:]
                    out_ref[branch, qr, :] = ri[:, :BLOCK] / l
                    lse_ref[branch, qr, :] = mi + jnp.log2(l)

            parts += [score, peak, weights, values, store]

    def combine(t0):
        r = pl.ds(t0, tc)
        lses = [lse_ref[i, r, :] for i in range(len(DIL_CONFIGS))]
        top = functools.reduce(jnp.maximum, lses)
        ws = [jnp.exp2(lse - top) for lse in lses]
        num = sum(w * out_ref[i, r, :] for i, w in enumerate(ws))
        o_ref[0, r, :] = _rms_norm(num / sum(ws), g_ref[...]).astype(o_ref.dtype)

    return parts + [functools.partial(combine, t0) for t0 in range(0, seq, tc)]


def _mixer_attn_kernel(dslope_ref, sslope_ref, dq_ref, dk_ref, dv_ref, lq1_ref, lk1_ref, lq2_ref, lk2_ref,
                       gd_ref, sq_ref, sk_ref, sv_ref, gs_ref, od_ref, os_ref,
                       vt_ref, qf, kf, vf, out_ref, lse_ref, *, t, lam_init, seq, tc):
    _interleave(
        _diff_parts(dslope_ref, dq_ref, dk_ref, dv_ref, lq1_ref, lk1_ref, lq2_ref, lk2_ref, gd_ref, od_ref, vt_ref,
                    t=t, lam_init=lam_init),
        _dilated_parts(sslope_ref, sq_ref, sk_ref, sv_ref, gs_ref, os_ref, qf, kf, vf, out_ref, lse_ref,
                       seq=seq, tc=tc))


def _mixer_attention(proj, slopes_diff, slopes_dil, lq1, lk1, lq2, lk2, g_diff, g_dil, lam_init, heads, t):
    b, s, _ = proj.shape
    nbr = len(DIL_CONFIGS)
    vec = lambda n: pl.BlockSpec((1, n), lambda bi, hi: (0, 0))
    slab = lambda blk: pl.BlockSpec((1, s, BLOCK), lambda bi, hi: (bi, 0, blk + hi))
    smem = pl.BlockSpec(memory_space=pltpu.SMEM)
    out = jax.ShapeDtypeStruct((b, s, heads * BLOCK), BF16)
    return pl.pallas_call(
        functools.partial(_mixer_attn_kernel, t=t, lam_init=lam_init, seq=s, tc=BLOCK),
        grid=(b, heads),
        in_specs=[smem, smem, slab(0), slab(heads), slab(2 * heads),
                  vec(DIFF_HALF_DIM), vec(DIFF_HALF_DIM), vec(DIFF_HALF_DIM), vec(DIFF_HALF_DIM), vec(BLOCK),
                  slab(3 * heads), slab(4 * heads), slab(5 * heads), vec(BLOCK)],
        out_specs=[slab(0), slab(0)],
        out_shape=[out, out],
        scratch_shapes=[pltpu.VMEM((s // t, BLOCK + BF16_SUBLANES, t), BF16),
                        pltpu.VMEM((s, BLOCK), F32), pltpu.VMEM((s, BLOCK), F32), pltpu.VMEM((s, BLOCK), F32),
                        pltpu.VMEM((nbr, s, BLOCK), F32), pltpu.VMEM((nbr, s, BLOCK), F32)],
        compiler_params=_params(("arbitrary", "arbitrary")),
    )(slopes_diff, slopes_dil, proj, proj, proj, lq1, lk1, lq2, lk2, g_diff, proj, proj, proj, g_dil)


def _outproj_ln_kernel(*refs, n_in, alpha, tr):
    o_refs, w_refs = refs[:n_in], refs[n_in:2 * n_in]
    h_ref, g_ref, b_ref, out_ref = refs[2 * n_in:]
    tm, d = out_ref.shape

    def matmul_parts(r0):
        def part(c0):
            rows, cols = slice(r0, r0 + tr), slice(c0, c0 + V7X_MXU_COLUMNS)
            y = alpha * h_ref[rows, cols]
            for o_ref, w_ref in zip(o_refs, w_refs):
                y = y + _dot(o_ref[rows, :], w_ref[:, cols])
            out_ref[rows, cols] = y
        return [functools.partial(part, c0) for c0 in range(0, d, V7X_MXU_COLUMNS)]

    def norm_parts(r0):
        def part(r):
            rows = slice(r, r + FFN_ROW_CHUNK)
            out_ref[rows, :] = _layer_norm(out_ref[rows, :], g_ref[...], b_ref[...])
        return [functools.partial(part, r) for r in range(r0, r0 + tr, FFN_ROW_CHUNK)]

    pending = []
    for r0 in range(0, tm, tr):
        _interleave(matmul_parts(r0), pending)
        pending = norm_parts(r0)
    _interleave(pending)


def _outproj_ln(os, w, h, g, b, alpha, tm, tr):
    m, d = h.shape
    n_in = len(os)
    kw = w.shape[0] // n_in
    return pl.pallas_call(
        functools.partial(_outproj_ln_kernel, n_in=n_in, alpha=alpha, tr=tr),
        grid=(m // tm,),
        in_specs=([pl.BlockSpec((tm, kw), lambda i: (i, 0)) for _ in os]
                  + [_resident((kw, d), functools.partial(lambda i, r: (r, 0), r=r)) for r in range(n_in)]
                  + [pl.BlockSpec((tm, d), lambda i: (i, 0)),
                     pl.BlockSpec((1, d), lambda i: (0, 0)), pl.BlockSpec((1, d), lambda i: (0, 0))]),
        out_specs=pl.BlockSpec((tm, d), lambda i: (i, 0)),
        out_shape=jax.ShapeDtypeStruct((m, d), F32),
        compiler_params=_params(("arbitrary",)),
    )(*os, *([w] * n_in), h, g, b)


def _mem_attn_kernel(h_ref, wq_ref, kv_ref, o_ref, *, heads):
    d = h_ref.shape[1]
    hd = d // heads
    q = _dot(h_ref[...].astype(BF16), wq_ref[...]).astype(BF16)
    scale = hd ** -0.5
    for i in range(heads):
        k = kv_ref[0, :, i * hd:(i + 1) * hd]
        v = kv_ref[0, :, d + i * hd:d + (i + 1) * hd]
        s = _dot_nt(q[:, i * hd:(i + 1) * hd], k) * scale
        e = jnp.exp(s - jnp.max(s, axis=-1, keepdims=True))
        p = e * (1.0 / jnp.sum(e, axis=-1, keepdims=True))
        o_ref[:, i * hd:(i + 1) * hd] = _dot(p.astype(BF16), v).astype(o_ref.dtype)


def _mem_attention(h, wq, kv, seq, tm):
    m, d = h.shape
    mem_len = kv.shape[1]
    per_seq = seq // tm
    return pl.pallas_call(
        functools.partial(_mem_attn_kernel, heads=MEM_HEADS),
        grid=(m // tm,),
        in_specs=[pl.BlockSpec((tm, d), lambda i: (i, 0)),
                  _resident((d, d), lambda i: (0, 0)),
                  pl.BlockSpec((1, mem_len, 2 * d), lambda i: (i // per_seq, 0, 0))],
        out_specs=pl.BlockSpec((tm, d), lambda i: (i, 0)),
        out_shape=jax.ShapeDtypeStruct((m, d), BF16),
        compiler_params=_params(("arbitrary",)),
    )(h, wq, kv)


HALO = 8
FFN_ROW_CHUNK = 64
V7X_MXU_COLUMNS = 256
BF16_SUBLANES = 16
SLOPE_PARTS = 3


def _ffn_kernel(x_ref, wg_ref, wu_ref, cwg_ref, cwu_ref, cbg_ref, cbu_ref, wd_ref, g_ref, b_ref, o_ref,
                xb_ref, hg0_ref, hu0_ref, hg1_ref, hu1_ref, act0_ref, act1_ref, carry_g_ref, carry_u_ref,
                *, tm, tiles_per_seq, alpha):
    i = pl.program_id(0)
    f = pl.program_id(1)
    nf = pl.num_programs(1) - 2
    seq_start = (i % tiles_per_seq) == 0
    h_refs = ((hg0_ref, hu0_ref), (hg1_ref, hu1_ref))
    act_refs = (act0_ref, act1_ref)
    acc_ref = o_ref

    tf = wg_ref.shape[1]
    d = wd_ref.shape[1]

    def up_project(slot):
        def part(which, c0):
            w_ref, hs_ref, carry_ref = ((wg_ref, wu_ref)[which], h_refs[slot][which],
                                        (carry_g_ref, carry_u_ref)[which])
            cols = slice(c0, c0 + V7X_MXU_COLUMNS)
            hs_ref[HALO:, cols] = _dot(xb_ref[...], w_ref[:, cols])
            hs_ref[:HALO, cols] = jnp.where(seq_start, 0.0, carry_ref[f, :, cols])
            carry_ref[f, :, cols] = hs_ref[tm:, cols]
        return [functools.partial(part, which, c0) for which in range(2)
                for c0 in range(0, tf, V7X_MXU_COLUMNS)]

    def conv(hs_ref, cw_ref, cb_ref, row0, cols):
        out = cb_ref[:, cols]
        for tap in range(CONV_WIDTH):
            off = row0 + HALO - (CONV_WIDTH - 1) + tap
            out = out + hs_ref[off:off + FFN_ROW_CHUNK, cols] * cw_ref[tap:tap + 1, cols]
        return out

    def activate(slot):
        def part(c0, row0):
            cols = slice(c0, c0 + BLOCK)
            gate = conv(h_refs[slot][0], cwg_ref, cbg_ref, row0, cols)
            up = conv(h_refs[slot][1], cwu_ref, cbu_ref, row0, cols)
            act = gate * (1.0 / (1.0 + jnp.exp2(gate * -math.log2(math.e)))) * up
            act_refs[slot][row0:row0 + FFN_ROW_CHUNK, cols] = act.astype(BF16)
        return [functools.partial(part, c0, row0) for c0 in range(0, tf, BLOCK)
                for row0 in range(0, tm, FFN_ROW_CHUNK)]

    def down_project(slot):
        def part(c0):
            cols = slice(c0, c0 + V7X_MXU_COLUMNS)
            acc_ref[:, cols] += _dot(act_refs[slot][...], wd_ref[:, cols])
        return [functools.partial(part, c0) for c0 in range(0, d, V7X_MXU_COLUMNS)]

    run = _interleave

    @pl.when(f == 0)
    def _():
        @pl.when(i == 0)
        def _():
            carry_g_ref[...] = jnp.zeros(carry_g_ref.shape, F32)
            carry_u_ref[...] = jnp.zeros(carry_u_ref.shape, F32)

        xb_ref[...] = x_ref[...].astype(BF16)
        acc_ref[...] = jnp.zeros(acc_ref.shape, F32)
        run(up_project(0))

    @pl.when(f == 1)
    def _():
        run(up_project(1), activate(0))

    for parity in range(2):
        @pl.when((f > 1) & (f < nf) & (f % 2 == parity))
        def _(parity=parity):
            run(up_project(parity), activate(1 - parity), down_project(parity))

    @pl.when(f == nf)
    def _():
        run(down_project((nf - 2) % 2), activate((nf - 1) % 2))

    @pl.when(f == nf + 1)
    def _():
        run(down_project((nf - 1) % 2))
        o_ref[...] = _layer_norm(alpha * x_ref[...] + acc_ref[...], g_ref[...], b_ref[...])


def _conv_ffn_ln(x, w_up, conv_w, conv_b, w_down, g, b, alpha, seq, tm, tf):
    m, d = x.shape
    d_ff = w_down.shape[0]
    nf = d_ff // tf
    row = lambda i, f: (i, 0)
    fixed = lambda i, f: (0, 0)
    chunk = lambda f, lag: jnp.clip(f - lag, 0, nf - 1)
    h_buf = pltpu.VMEM((tm + HALO, tf), F32)
    act_buf = pltpu.VMEM((tm, tf), BF16)
    return pl.pallas_call(
        functools.partial(_ffn_kernel, tm=tm, tiles_per_seq=seq // tm, alpha=alpha),
        grid=(m // tm, nf + 2),
        in_specs=[pl.BlockSpec((tm, d), row),
                  pl.BlockSpec((d, tf), lambda i, f: (0, chunk(f, 0))),
                  pl.BlockSpec((d, tf), lambda i, f: (0, nf + chunk(f, 0))),
                  pl.BlockSpec((CONV_WIDTH, tf), lambda i, f: (0, chunk(f, 1))),
                  pl.BlockSpec((CONV_WIDTH, tf), lambda i, f: (0, nf + chunk(f, 1))),
                  pl.BlockSpec((1, tf), lambda i, f: (0, chunk(f, 1))),
                  pl.BlockSpec((1, tf), lambda i, f: (0, nf + chunk(f, 1))),
                  pl.BlockSpec((tf, d), lambda i, f: (chunk(f, 2), 0)),
                  pl.BlockSpec((1, d), fixed), pl.BlockSpec((1, d), fixed)],
        out_specs=pl.BlockSpec((tm, d), row),
        out_shape=jax.ShapeDtypeStruct((m, d), F32),
        scratch_shapes=[pltpu.VMEM((tm, d), BF16), h_buf, h_buf, h_buf, h_buf, act_buf, act_buf,
                        pltpu.VMEM((nf, HALO, tf), F32), pltpu.VMEM((nf, HALO, tf), F32)],
        compiler_params=_params(("arbitrary", "arbitrary")),
    )(x, w_up, w_up, conv_w, conv_w, conv_b, conv_b, w_down, g, b)


def _alibi_slopes(n):
    return (2.0 ** (-8.0 * np.arange(1, n + 1) / n)).astype(np.float32)


def kernel(x, mem, w_in, w_mix_out, lambda_q1, lambda_k1, lambda_q2, lambda_k2, g_diff, g_dil, ln1_g, ln1_b,
           w_mem_q, w_mem_kv, w_mem_o, ln2_g, ln2_b, w_up, conv_w, conv_b, w_down, ln3_g, ln3_b):
    batch, seq, d = x.shape
    depth = w_in.shape[0]
    mem_len = mem.shape[1]
    diff_heads = dil_heads = d // 256
    alpha = (2 * depth) ** 0.25
    slopes = _alibi_slopes(diff_heads + dil_heads)
    log2e = math.log2(math.e)
    slopes_diff = _split_slopes(slopes[0::2])
    slopes_dil = jnp.asarray(slopes[1::2].astype(np.float64) * log2e, F32)
    row = lambda a: a.reshape(1, -1)

    group = diff_heads * BLOCK
    col_scale = np.ones((1, 6 * group), np.float32)
    col_scale[:, :group] = DIFF_HALF_DIM ** -0.5 * log2e
    col_scale[:, 3 * group:4 * group] = BLOCK ** -0.5 * log2e
    col_scale = jnp.asarray(col_scale)
    kv_scale = jnp.ones((1, 2 * d), F32)

    h = x.reshape(batch * seq, d)
    mem2 = mem.reshape(batch * mem_len, d)
    for l in range(depth):
        lam_init = 0.8 - 0.6 * math.exp(-0.3 * l)
        proj = _project(h, w_in[l].astype(BF16), col_scale, tm=1024, tn=1024).reshape(batch, seq, -1)
        o_diff, o_dil = _mixer_attention(proj, slopes_diff, slopes_dil, row(lambda_q1[l]), row(lambda_k1[l]),
                                         row(lambda_q2[l]), row(lambda_k2[l]), row(g_diff[l]), row(g_dil[l]),
                                         lam_init, diff_heads, t=512)
        h = _outproj_ln([o_diff.reshape(batch * seq, -1), o_dil.reshape(batch * seq, -1)],
                        w_mix_out[l].astype(BF16), h, row(ln1_g[l]), row(ln1_b[l]), alpha, tm=1024, tr=512)

        kv = _project(mem2, w_mem_kv[l].astype(BF16), kv_scale, tm=1024, tn=1024).reshape(batch, mem_len, 2 * d)
        o_mem = _mem_attention(h, w_mem_q[l].astype(BF16), kv, seq, tm=512)
        h = _outproj_ln([o_mem], w_mem_o[l].astype(BF16), h, row(ln2_g[l]), row(ln2_b[l]), alpha, tm=1024, tr=512)

        h = _conv_ffn_ln(h, w_up[l].astype(BF16), conv_w[l], row(conv_b[l]), w_down[l].astype(BF16),
                         row(ln3_g[l]), row(ln3_b[l]), alpha, seq, tm=512, tf=512)
    return h.reshape(batch, seq, d)
```

```python
import functools
import math

import numpy as np
import jax
import jax.numpy as jnp
from jax import lax
from jax.experimental import pallas as pl
from jax.experimental.pallas import tpu as pltpu

F32 = jnp.float32
BF16 = jnp.bfloat16

LN_EPS = 1e-5
RMS_EPS = 1e-5
BLOCK = 128
DIFF_HALF_DIM = 64
DIL_CONFIGS = ((128, 1), (512, 4), (2048, 16))
DIFF_CHAINS = 2
MEM_HEADS = 4
CONV_WIDTH = 3
NEG_INF = float("-inf")
HALO = 8
ROW_CHUNK = 64
SLOPE_PARTS = 3

V7X_VMEM_LIMIT_BYTES = 56 * 1024 * 1024
V7X_MXU_COLUMNS = 256
BF16_SUBLANES = 16


def _params(semantics):
    return pltpu.CompilerParams(dimension_semantics=semantics, vmem_limit_bytes=V7X_VMEM_LIMIT_BYTES)


def _resident(shape, index_map):
    return pl.BlockSpec(shape, index_map, pipeline_mode=pl.Buffered(1))


def _layer_norm(z, g, b):
    mu = jnp.mean(z, axis=-1, keepdims=True)
    zc = z - mu
    var = jnp.mean(zc * zc, axis=-1, keepdims=True)
    return zc * lax.rsqrt(var + LN_EPS) * g + b


def _rms_norm(o, g):
    return o * lax.rsqrt(jnp.mean(o * o, axis=-1, keepdims=True) + RMS_EPS) * g


def _interleave(*stages):
    order = sorted(((k + 0.5) / len(parts), s, k) for s, parts in enumerate(stages) for k in range(len(parts)))
    for _, s, k in order:
        stages[s][k]()


def _dot(a, b):
    return jnp.dot(a, b, preferred_element_type=F32)


def _dot_nt(a, b):
    return lax.dot_general(a, b, (((1,), (1,)), ((), ())), preferred_element_type=F32)


def _proj_kernel(x_ref, w_ref, cs_ref, o_ref, xb_ref):
    @pl.when(pl.program_id(1) == 0)
    def _():
        xb_ref[...] = x_ref[...].astype(BF16)

    o_ref[...] = (_dot(xb_ref[...], w_ref[...]) * cs_ref[...]).astype(o_ref.dtype)


def _project(x, w, layer, col_scale, tm, tn):
    m, k = x.shape
    n = w.shape[2]
    tm = min(tm, m)
    return pl.pallas_call(
        _proj_kernel,
        grid=(m // tm, n // tn),
        in_specs=[pl.BlockSpec((tm, k), lambda i, j: (i, 0)),
                  pl.BlockSpec((None, k, tn), lambda i, j: (layer, 0, j)),
                  pl.BlockSpec((1, tn), lambda i, j: (0, j))],
        out_specs=pl.BlockSpec((tm, tn), lambda i, j: (i, j)),
        out_shape=jax.ShapeDtypeStruct((m, n), BF16),
        scratch_shapes=[pltpu.VMEM((tm, k), BF16)],
        compiler_params=_params(("arbitrary", "arbitrary")),
    )(x, w, col_scale)


def _diff_parts(slope_ref, q_ref, k_ref, v_ref, lq1_ref, lk1_ref, lq2_ref, lk2_ref, g_ref, o_ref, vt_ref,
                *, t, lam_init):
    c = {}
    n_tiles = vt_ref.shape[0]
    assert t <= 2 * V7X_MXU_COLUMNS and V7X_MXU_COLUMNS == 1 << 8

    def setup():
        h = pl.program_id(1)
        c["slope"] = slope_ref[h, SLOPE_PARTS]
        for j in range(n_tiles):
            vt_ref[j, :BLOCK, :] = v_ref[0, j * t:(j + 1) * t, :].astype(F32).T.astype(BF16)
            vt_ref[j, BLOCK:, :] = jnp.ones((vt_ref.shape[1] - BLOCK, t), BF16)
        c["lam"] = (jnp.exp(jnp.sum(lq1_ref[...] * lk1_ref[...], axis=-1, keepdims=True))
                    - jnp.exp(jnp.sum(lq2_ref[...] * lk2_ref[...], axis=-1, keepdims=True)) + lam_init)
        c["lane"] = lax.broadcasted_iota(jnp.int32, (t, BLOCK), 1)
        key_index = lax.broadcasted_iota(jnp.int32, (t, BLOCK), 0)
        low, high = key_index & (V7X_MXU_COLUMNS - 1), key_index >> 8
        c["k_extra"] = jnp.where(c["lane"] < SLOPE_PARTS, low,
                                 jnp.where(c["lane"] < 2 * SLOPE_PARTS, high, 0)).astype(F32).astype(BF16)
        q_extra = jnp.zeros((2 * t, BLOCK), F32)
        lane2 = lax.broadcasted_iota(jnp.int32, (2 * t, BLOCK), 1)
        for i in range(SLOPE_PARTS):
            q_extra = jnp.where(lane2 == i, slope_ref[h, i], q_extra)
            q_extra = jnp.where(lane2 == SLOPE_PARTS + i, slope_ref[h, i] * float(V7X_MXU_COLUMNS), q_extra)
        c["q_extra"] = q_extra.astype(BF16)
        key = lax.broadcasted_iota(jnp.int32, (t, 2 * t), 0)
        query = lax.broadcasted_iota(jnp.int32, (t, 2 * t), 1)
        c["causal"] = key <= jnp.where(query >= t, query - t, query)

    def queries(qi):
        q = q_ref[0, qi * t:(qi + 1) * t, :]
        zero = jnp.zeros_like(q)
        qq = jnp.concatenate([jnp.where(c["lane"] < DIFF_HALF_DIM, q, zero),
                              jnp.where(c["lane"] >= DIFF_HALF_DIM, q, zero)], axis=0)
        return jnp.concatenate([qq, c["q_extra"]], axis=1)

    order = sorted(range(n_tiles), key=lambda qi: -qi)
    chains = [[] for _ in range(DIFF_CHAINS)]
    for qi in order:
        min(chains, key=len).extend((qi, j) for j in range(qi + 1))
    state = {}
    parts = [setup]
    for step in range(max(len(ch) for ch in chains)):
        live = [ch[step] for ch in chains if step < len(ch)]
        s, p = {}, {}

        def score(live=live, s=s):
            for qi, j in live:
                if j == 0:
                    state[qi] = dict(qq=queries(qi))
                k = jnp.concatenate([k_ref[0, j * t:(j + 1) * t, :], c["k_extra"]], axis=1)
                s[qi] = _dot_nt(k, state[qi]["qq"])
                if j == qi:
                    s[qi] = jnp.where(c["causal"], s[qi], NEG_INF)

        def softmax(live=live, s=s, p=p):
            for qi, j in live:
                st = state[qi]
                shift = c["slope"] * float((j - qi) * t)
                m_tile = jnp.max(s[qi], axis=0, keepdims=True) + shift
                st["m_old"], st["m"] = st.get("m"), (m_tile if j == 0 else jnp.maximum(st["m"], m_tile))
                p[qi] = jnp.exp2(s[qi] - (st["m"] - shift)).astype(BF16)

        def values(live=live, p=p):
            for qi, j in live:
                st = state[qi]
                acc_tile = _dot(vt_ref[j], p[qi])
                st["acc"] = acc_tile if j == 0 else jnp.exp2(st["m_old"] - st["m"]) * st["acc"] + acc_tile

        def finish(live=live):
            for qi, j in live:
                if j == qi:
                    acc = state.pop(qi)["acc"]
                    o = acc[:BLOCK] / acc[BLOCK:BLOCK + 1]
                    o = (o[:, :t] - c["lam"] * o[:, t:]).T
                    o_ref[0, qi * t:(qi + 1) * t, :] = (_rms_norm(o, g_ref[...])
                                                        * (1.0 - lam_init)).astype(o_ref.dtype)

        parts += [score, softmax, values, finish]
    return parts


def _dilated_parts(slope_ref, q_ref, k_ref, v_ref, g_ref, o_ref, qf, kf, vf, out_ref, lse_ref, *, seq, tc):
    c = {}

    def setup():
        c["slope"] = slope_ref[pl.program_id(1)]
        qf[...] = q_ref[0].astype(F32)
        kf[...] = k_ref[0].astype(F32)
        vf[...] = v_ref[0].astype(F32)
        qi = lax.broadcasted_iota(jnp.int32, (BLOCK, 2 * BLOCK), 0)
        kj = lax.broadcasted_iota(jnp.int32, (BLOCK, 2 * BLOCK), 1)
        c["rel"] = qi + BLOCK - kj

    def rows(start, size, dil):
        return pl.ds(start, size) if dil == 1 else pl.ds(start, size, stride=dil)

    parts = [setup]
    for branch, (window, dil) in enumerate(DIL_CONFIGS):
        nb = seq // dil // BLOCK
        step = dil * BLOCK

        def biases(branch=branch, dil=dil, n_back=window // dil):
            valid = (c["rel"] >= 0) & (c["rel"] <= n_back)
            c["bias", branch] = jnp.where(valid, (-c["slope"] * dil) * c["rel"].astype(F32), NEG_INF)

        parts.append(biases)
        groups = [([(cls, cls) for cls in range(dil)], True)]
        if nb > 1:
            groups.append(([(cls + n * step, cls + (n - 1) * step) for cls in range(dil) for n in range(1, nb)],
                           False))
        for starts, first in groups:
            nkeys = BLOCK if first else 2 * BLOCK
            q_rows = [rows(q_start, BLOCK, dil) for q_start, _ in starts]
            k_rows = [rows(k_start, nkeys, dil) for _, k_start in starts]
            v = {}

            def score(branch=branch, first=first, q_rows=q_rows, k_rows=k_rows, v=v):
                bias = c["bias", branch][:, BLOCK:] if first else c["bias", branch]
                v["s"] = [_dot_nt(qf[qr, :].astype(BF16), kf[kr, :].astype(BF16)) + bias
                          for qr, kr in zip(q_rows, k_rows)]

            def peak(v=v):
                v["m"] = [jnp.max(si, axis=-1, keepdims=True) for si in v["s"]]

            def weights(v=v):
                v["p"] = [jnp.exp2(si - mi).astype(BF16) for si, mi in zip(v["s"], v["m"])]

            def values(k_rows=k_rows, nkeys=nkeys, v=v):
                ones = jnp.ones((nkeys, BLOCK), BF16)
                v["r"] = [_dot(pi, jnp.concatenate([vf[kr, :].astype(BF16), ones], axis=1))
                          for pi, kr in zip(v["p"], k_rows)]

            def store(branch=branch, q_rows=q_rows, v=v):
                for qr, mi, ri in zip(q_rows, v["m"], v["r"]):
                    l = ri[:, BLOCK:]
                    out_ref[branch, qr, :] = ri[:, :BLOCK] / l
                    lse_ref[branch, qr, :] = mi + jnp.log2(l)

            parts += [score, peak, weights, values, store]

    def combine(t0):
        r = pl.ds(t0, tc)
        lses = [lse_ref[i, r, :] for i in range(len(DIL_CONFIGS))]
        top = functools.reduce(jnp.maximum, lses)
        ws = [jnp.exp2(lse - top) for lse in lses]
        num = sum(w * out_ref[i, r, :] for i, w in enumerate(ws))
        o_ref[0, r, :] = _rms_norm(num / sum(ws), g_ref[...]).astype(o_ref.dtype)

    return parts + [functools.partial(combine, t0) for t0 in range(0, seq, tc)]


def _mixer_attn_kernel(dslope_ref, sslope_ref, dq_ref, dk_ref, dv_ref, lq1_ref, lk1_ref, lq2_ref, lk2_ref,
                       gd_ref, sq_ref, sk_ref, sv_ref, gs_ref, od_ref, os_ref,
                       vt_ref, qf, kf, vf, out_ref, lse_ref, *, t, lam_init, seq, tc):
    _interleave(
        _diff_parts(dslope_ref, dq_ref, dk_ref, dv_ref, lq1_ref, lk1_ref, lq2_ref, lk2_ref, gd_ref, od_ref, vt_ref,
                    t=t, lam_init=lam_init),
        _dilated_parts(sslope_ref, sq_ref, sk_ref, sv_ref, gs_ref, os_ref, qf, kf, vf, out_ref, lse_ref,
                       seq=seq, tc=tc))


def _mixer_attention(proj, slopes_diff, slopes_dil, lq1, lk1, lq2, lk2, g_diff, g_dil, lam_init, heads, t):
    b, s, _ = proj.shape
    nbr = len(DIL_CONFIGS)
    vec = lambda n: pl.BlockSpec((1, n), lambda bi, hi: (0, 0))
    slab = lambda blk: pl.BlockSpec((1, s, BLOCK), lambda bi, hi: (bi, 0, blk + hi))
    smem = pl.BlockSpec(memory_space=pltpu.SMEM)
    out = jax.ShapeDtypeStruct((b, s, heads * BLOCK), BF16)
    return pl.pallas_call(
        functools.partial(_mixer_attn_kernel, t=t, lam_init=lam_init, seq=s, tc=BLOCK),
        grid=(b, heads),
        in_specs=[smem, smem, slab(0), slab(heads), slab(2 * heads),
                  vec(DIFF_HALF_DIM), vec(DIFF_HALF_DIM), vec(DIFF_HALF_DIM), vec(DIFF_HALF_DIM), vec(BLOCK),
                  slab(3 * heads), slab(4 * heads), slab(5 * heads), vec(BLOCK)],
        out_specs=[slab(0), slab(0)],
        out_shape=[out, out],
        scratch_shapes=[pltpu.VMEM((s // t, BLOCK + BF16_SUBLANES, t), BF16),
                        pltpu.VMEM((s, BLOCK), F32), pltpu.VMEM((s, BLOCK), F32), pltpu.VMEM((s, BLOCK), F32),
                        pltpu.VMEM((nbr, s, BLOCK), F32), pltpu.VMEM((nbr, s, BLOCK), F32)],
        compiler_params=_params(("arbitrary", "arbitrary")),
    )(slopes_diff, slopes_dil, proj, proj, proj, lq1, lk1, lq2, lk2, g_diff, proj, proj, proj, g_dil)


def _split_slopes(slopes):
    x = np.asarray(slopes, np.float64) * math.log2(math.e)
    x = x.astype(np.float32)
    parts, rest = [], x.copy()
    for _ in range(SLOPE_PARTS):
        part = rest.astype(BF16).astype(np.float32)
        parts.append(part)
        rest = rest - part
    return jnp.asarray(np.stack(parts + [x], axis=1), F32)


def _outproj_ln_kernel(*refs, n_in, alpha, tr):
    o_refs, w_refs = refs[:n_in], refs[n_in:2 * n_in]
    h_ref, g_ref, b_ref, out_ref = refs[2 * n_in:]
    tm, d = out_ref.shape

    def matmul_parts(r0):
        def part(c0):
            rows, cols = slice(r0, r0 + tr), slice(c0, c0 + V7X_MXU_COLUMNS)
            y = alpha * h_ref[rows, cols]
            for o_ref, w_ref in zip(o_refs, w_refs):
                y = y + _dot(o_ref[rows, :], w_ref[:, cols])
            out_ref[rows, cols] = y
        return [functools.partial(part, c0) for c0 in range(0, d, V7X_MXU_COLUMNS)]

    def norm_parts(r0):
        def part(r):
            rows = slice(r, r + ROW_CHUNK)
            out_ref[rows, :] = _layer_norm(out_ref[rows, :], g_ref[...], b_ref[...])
        return [functools.partial(part, r) for r in range(r0, r0 + tr, ROW_CHUNK)]

    pending = []
    for r0 in range(0, tm, tr):
        _interleave(matmul_parts(r0), pending)
        pending = norm_parts(r0)
    _interleave(pending)


def _outproj_ln(os, w, layer, h, g, b, alpha, tm, tr):
    m, d = h.shape
    n_in = len(os)
    kw = w.shape[1] // n_in
    return pl.pallas_call(
        functools.partial(_outproj_ln_kernel, n_in=n_in, alpha=alpha, tr=tr),
        grid=(m // tm,),
        in_specs=([pl.BlockSpec((tm, kw), lambda i: (i, 0)) for _ in os]
                  + [_resident((None, kw, d), functools.partial(lambda i, r: (layer, r, 0), r=r))
                     for r in range(n_in)]
                  + [pl.BlockSpec((tm, d), lambda i: (i, 0)),
                     pl.BlockSpec((1, d), lambda i: (0, 0)), pl.BlockSpec((1, d), lambda i: (0, 0))]),
        out_specs=pl.BlockSpec((tm, d), lambda i: (i, 0)),
        out_shape=jax.ShapeDtypeStruct((m, d), F32),
        compiler_params=_params(("arbitrary",)),
    )(*os, *([w] * n_in), h, g, b)


def _mem_attn_kernel(h_ref, wq_ref, kv_ref, o_ref, *, heads):
    d = h_ref.shape[1]
    hd = d // heads
    q = _dot(h_ref[...].astype(BF16), wq_ref[...]).astype(BF16)
    scale = hd ** -0.5
    for i in range(heads):
        k = kv_ref[0, :, i * hd:(i + 1) * hd]
        v = kv_ref[0, :, d + i * hd:d + (i + 1) * hd]
        s = _dot_nt(q[:, i * hd:(i + 1) * hd], k) * scale
        e = jnp.exp(s - jnp.max(s, axis=-1, keepdims=True))
        p = e * (1.0 / jnp.sum(e, axis=-1, keepdims=True))
        o_ref[:, i * hd:(i + 1) * hd] = _dot(p.astype(BF16), v).astype(o_ref.dtype)


def _mem_attention(h, wq, layer, kv, seq, tm):
    m, d = h.shape
    mem_len = kv.shape[1]
    per_seq = seq // tm
    return pl.pallas_call(
        functools.partial(_mem_attn_kernel, heads=MEM_HEADS),
        grid=(m // tm,),
        in_specs=[pl.BlockSpec((tm, d), lambda i: (i, 0)),
                  _resident((None, d, d), lambda i: (layer, 0, 0)),
                  pl.BlockSpec((1, mem_len, 2 * d), lambda i: (i // per_seq, 0, 0))],
        out_specs=pl.BlockSpec((tm, d), lambda i: (i, 0)),
        out_shape=jax.ShapeDtypeStruct((m, d), BF16),
        compiler_params=_params(("arbitrary",)),
    )(h, wq, kv)


def _ffn_kernel(x_ref, wg_ref, wu_ref, cwg_ref, cwu_ref, cbg_ref, cbu_ref, wd_ref, g_ref, b_ref, o_ref,
                xb_ref, hg0_ref, hu0_ref, hg1_ref, hu1_ref, act0_ref, act1_ref, carry_g_ref, carry_u_ref,
                *, tm, tiles_per_seq, alpha):
    i = pl.program_id(0)
    f = pl.program_id(1)
    nf = pl.num_programs(1) - 2
    seq_start = (i % tiles_per_seq) == 0
    h_refs = ((hg0_ref, hu0_ref), (hg1_ref, hu1_ref))
    act_refs = (act0_ref, act1_ref)
    acc_ref = o_ref

    tf = wg_ref.shape[1]
    d = wd_ref.shape[1]

    def up_project(slot):
        def part(which, c0):
            w_ref, hs_ref, carry_ref = ((wg_ref, wu_ref)[which], h_refs[slot][which],
                                        (carry_g_ref, carry_u_ref)[which])
            cols = slice(c0, c0 + V7X_MXU_COLUMNS)
            hs_ref[HALO:, cols] = _dot(xb_ref[...], w_ref[:, cols])
            hs_ref[:HALO, cols] = jnp.where(seq_start, 0.0, carry_ref[f, :, cols])
            carry_ref[f, :, cols] = hs_ref[tm:, cols]
        return [functools.partial(part, which, c0) for which in range(2)
                for c0 in range(0, tf, V7X_MXU_COLUMNS)]

    def conv(hs_ref, cw_ref, cb_ref, row0, cols):
        out = cb_ref[:, cols]
        for tap in range(CONV_WIDTH):
            off = row0 + HALO - (CONV_WIDTH - 1) + tap
            out = out + hs_ref[off:off + ROW_CHUNK, cols] * cw_ref[tap:tap + 1, cols]
        return out

    def activate(slot):
        def part(c0, row0):
            cols = slice(c0, c0 + BLOCK)
            gate = conv(h_refs[slot][0], cwg_ref, cbg_ref, row0, cols)
            up = conv(h_refs[slot][1], cwu_ref, cbu_ref, row0, cols)
            act = gate * (1.0 / (1.0 + jnp.exp2(gate * -math.log2(math.e)))) * up
            act_refs[slot][row0:row0 + ROW_CHUNK, cols] = act.astype(BF16)
        return [functools.partial(part, c0, row0) for c0 in range(0, tf, BLOCK)
                for row0 in range(0, tm, ROW_CHUNK)]

    def down_project(slot):
        def part(c0):
            cols = slice(c0, c0 + V7X_MXU_COLUMNS)
            acc_ref[:, cols] += _dot(act_refs[slot][...], wd_ref[:, cols])
        return [functools.partial(part, c0) for c0 in range(0, d, V7X_MXU_COLUMNS)]

    run = _interleave

    @pl.when(f == 0)
    def _():
        @pl.when(i == 0)
        def _():
            carry_g_ref[...] = jnp.zeros(carry_g_ref.shape, F32)
            carry_u_ref[...] = jnp.zeros(carry_u_ref.shape, F32)

        xb_ref[...] = x_ref[...].astype(BF16)
        acc_ref[...] = jnp.zeros(acc_ref.shape, F32)
        run(up_project(0))

    @pl.when(f == 1)
    def _():
        run(up_project(1), activate(0))

    for parity in range(2):
        @pl.when((f > 1) & (f < nf) & (f % 2 == parity))
        def _(parity=parity):
            run(up_project(parity), activate(1 - parity), down_project(parity))

    @pl.when(f == nf)
    def _():
        run(down_project((nf - 2) % 2), activate((nf - 1) % 2))

    @pl.when(f == nf + 1)
    def _():
        run(down_project((nf - 1) % 2))
        o_ref[...] = _layer_norm(alpha * x_ref[...] + acc_ref[...], g_ref[...], b_ref[...])


def _conv_ffn_ln(x, w_up, w_down, layer, conv_w, conv_b, g, b, alpha, seq, tm, tf):
    m, d = x.shape
    d_ff = w_down.shape[1]
    nf = d_ff // tf
    row = lambda i, f: (i, 0)
    fixed = lambda i, f: (0, 0)
    chunk = lambda f, lag: jnp.clip(f - lag, 0, nf - 1)
    h_buf = pltpu.VMEM((tm + HALO, tf), F32)
    act_buf = pltpu.VMEM((tm, tf), BF16)
    return pl.pallas_call(
        functools.partial(_ffn_kernel, tm=tm, tiles_per_seq=seq // tm, alpha=alpha),
        grid=(m // tm, nf + 2),
        in_specs=[pl.BlockSpec((tm, d), row),
                  pl.BlockSpec((None, d, tf), lambda i, f: (layer, 0, chunk(f, 0))),
                  pl.BlockSpec((None, d, tf), lambda i, f: (layer, 0, nf + chunk(f, 0))),
                  pl.BlockSpec((CONV_WIDTH, tf), lambda i, f: (0, chunk(f, 1))),
                  pl.BlockSpec((CONV_WIDTH, tf), lambda i, f: (0, nf + chunk(f, 1))),
                  pl.BlockSpec((1, tf), lambda i, f: (0, chunk(f, 1))),
                  pl.BlockSpec((1, tf), lambda i, f: (0, nf + chunk(f, 1))),
                  pl.BlockSpec((None, tf, d), lambda i, f: (layer, chunk(f, 2), 0)),
                  pl.BlockSpec((1, d), fixed), pl.BlockSpec((1, d), fixed)],
        out_specs=pl.BlockSpec((tm, d), row),
        out_shape=jax.ShapeDtypeStruct((m, d), F32),
        scratch_shapes=[pltpu.VMEM((tm, d), BF16), h_buf, h_buf, h_buf, h_buf, act_buf, act_buf,
                        pltpu.VMEM((nf, HALO, tf), F32), pltpu.VMEM((nf, HALO, tf), F32)],
        compiler_params=_params(("arbitrary", "arbitrary")),
    )(x, w_up, w_up, conv_w, conv_w, conv_b, conv_b, w_down, g, b)


def _alibi_slopes(n):
    return (2.0 ** (-8.0 * np.arange(1, n + 1) / n)).astype(np.float32)


def kernel(x, mem, w_in, w_mix_out, lambda_q1, lambda_k1, lambda_q2, lambda_k2, g_diff, g_dil, ln1_g, ln1_b,
           w_mem_q, w_mem_kv, w_mem_o, ln2_g, ln2_b, w_up, conv_w, conv_b, w_down, ln3_g, ln3_b):
    batch, seq, d = x.shape
    depth = w_in.shape[0]
    mem_len = mem.shape[1]
    diff_heads = dil_heads = d // 256
    alpha = (2 * depth) ** 0.25
    slopes = _alibi_slopes(diff_heads + dil_heads)
    log2e = math.log2(math.e)
    slopes_diff = _split_slopes(slopes[0::2])
    slopes_dil = jnp.asarray(slopes[1::2].astype(np.float64) * log2e, F32)
    row = lambda a: a.reshape(1, -1)

    group = diff_heads * BLOCK
    col_scale = np.ones((1, 6 * group), np.float32)
    col_scale[:, :group] = DIFF_HALF_DIM ** -0.5 * log2e
    col_scale[:, 3 * group:4 * group] = BLOCK ** -0.5 * log2e
    col_scale = jnp.asarray(col_scale)
    kv_scale = jnp.ones((1, 2 * d), F32)

    w_in, w_mix_out, w_mem_q, w_mem_kv, w_mem_o, w_up, w_down = (
        w.astype(BF16) for w in (w_in, w_mix_out, w_mem_q, w_mem_kv, w_mem_o, w_up, w_down))

    h = x.reshape(batch * seq, d)
    mem2 = mem.reshape(batch * mem_len, d)
    for l in range(depth):
        lam_init = 0.8 - 0.6 * math.exp(-0.3 * l)
        proj = _project(h, w_in, l, col_scale, tm=1024, tn=1024).reshape(batch, seq, -1)
        o_diff, o_dil = _mixer_attention(proj, slopes_diff, slopes_dil, row(lambda_q1[l]), row(lambda_k1[l]),
                                         row(lambda_q2[l]), row(lambda_k2[l]), row(g_diff[l]), row(g_dil[l]),
                                         lam_init, diff_heads, t=512)
        h = _outproj_ln([o_diff.reshape(batch * seq, -1), o_dil.reshape(batch * seq, -1)],
                        w_mix_out, l, h, row(ln1_g[l]), row(ln1_b[l]), alpha, tm=1024, tr=512)

        kv = _project(mem2, w_mem_kv, l, kv_scale, tm=1024, tn=1024).reshape(batch, mem_len, 2 * d)
        o_mem = _mem_attention(h, w_mem_q, l, kv, seq, tm=512)
        h = _outproj_ln([o_mem], w_mem_o, l, h, row(ln2_g[l]), row(ln2_b[l]), alpha, tm=1024, tr=512)

        h = _conv_ffn_ln(h, w_up, w_down, l, conv_w[l], row(conv_b[l]), row(ln3_g[l]), row(ln3_b[l]),
                         alpha, seq, tm=512, tf=512)
    return h.reshape(batch, seq, d)
```

```python
import functools
import math

import numpy as np
import jax
import jax.numpy as jnp
from jax import lax
from jax.experimental import pallas as pl
from jax.experimental.pallas import tpu as pltpu

F32 = jnp.float32
BF16 = jnp.bfloat16

LN_EPS = 1e-5
RMS_EPS = 1e-5
BLOCK = 128
DIFF_HALF_DIM = 64
DIL_CONFIGS = ((128, 1), (512, 4), (2048, 16))
DIFF_CHAINS = 2
MEM_HEADS = 4
CONV_WIDTH = 3
NEG_INF = float("-inf")
HALO = 8
ROW_CHUNK = 64
SLOPE_PARTS = 3

V7X_VMEM_LIMIT_BYTES = 56 * 1024 * 1024
V7X_MXU_COLUMNS = 256
BF16_SUBLANES = 16


def _params(semantics):
    return pltpu.CompilerParams(dimension_semantics=semantics, vmem_limit_bytes=V7X_VMEM_LIMIT_BYTES)


def _resident(shape, index_map):
    return pl.BlockSpec(shape, index_map, pipeline_mode=pl.Buffered(1))


def _layer_norm(z, g, b):
    mu = jnp.mean(z, axis=-1, keepdims=True)
    zc = z - mu
    var = jnp.mean(zc * zc, axis=-1, keepdims=True)
    return zc * lax.rsqrt(var + LN_EPS) * g + b


def _rms_norm(o, g):
    return o * lax.rsqrt(jnp.mean(o * o, axis=-1, keepdims=True) + RMS_EPS) * g


def _interleave(*stages):
    order = sorted(((k + 0.5) / len(parts), s, k) for s, parts in enumerate(stages) for k in range(len(parts)))
    for _, s, k in order:
        stages[s][k]()


def _dot(a, b):
    return jnp.dot(a, b, preferred_element_type=F32)


def _dot_nt(a, b):
    return lax.dot_general(a, b, (((1,), (1,)), ((), ())), preferred_element_type=F32)


def _proj_kernel(x_ref, w_ref, cs_ref, *refs):
    n_cast = (len(refs) - 2) // 2
    cast_in, o_ref, cast_out, xb_ref = refs[:n_cast], refs[n_cast], refs[n_cast + 1:-1], refs[-1]

    @pl.when(pl.program_id(1) == 0)
    def _():
        xb_ref[...] = x_ref[...].astype(BF16)

    o_ref[...] = (_dot(xb_ref[...], w_ref[...]) * cs_ref[...]).astype(o_ref.dtype)
    for src, dst in zip(cast_in, cast_out):
        dst[...] = src[...].astype(BF16)


def _project(x, w, layer, col_scale, tm, tn, cast=()):
    m, k = x.shape
    n = w.shape[2]
    tm = min(tm, m)
    grid = (m // tm, n // tn)
    steps = grid[0] * grid[1]
    flat = [c.reshape(-1, c.shape[-1]) for c in cast]
    block_rows = [next(br for br in range(BF16_SUBLANES, f.shape[0] + 1, BF16_SUBLANES)
                       if f.shape[0] % br == 0 and f.shape[0] // br <= steps) for f in flat]

    def cast_spec(f, br):
        last = f.shape[0] // br - 1
        return pl.BlockSpec((br, f.shape[1]), lambda i, j: (jnp.minimum(i * grid[1] + j, last), 0))

    cast_specs = [cast_spec(f, br) for f, br in zip(flat, block_rows)]
    outs = pl.pallas_call(
        _proj_kernel,
        grid=grid,
        in_specs=[pl.BlockSpec((tm, k), lambda i, j: (i, 0)),
                  pl.BlockSpec((None, k, tn), lambda i, j: (layer, 0, j)),
                  pl.BlockSpec((1, tn), lambda i, j: (0, j))] + cast_specs,
        out_specs=[pl.BlockSpec((tm, tn), lambda i, j: (i, j))] + cast_specs,
        out_shape=[jax.ShapeDtypeStruct((m, n), BF16)] + [jax.ShapeDtypeStruct(f.shape, BF16) for f in flat],
        scratch_shapes=[pltpu.VMEM((tm, k), BF16)],
        compiler_params=_params(("arbitrary", "arbitrary")),
    )(x, w, col_scale, *flat)
    return (outs[0], *(o.reshape(c.shape) for o, c in zip(outs[1:], cast)))


def _diff_parts(slope_ref, q_ref, k_ref, v_ref, lq1_ref, lk1_ref, lq2_ref, lk2_ref, g_ref, o_ref, vt_ref,
                *, t, lam_init):
    c = {}
    n_tiles = vt_ref.shape[0]
    assert t <= 2 * V7X_MXU_COLUMNS and V7X_MXU_COLUMNS == 1 << 8

    def setup():
        h = pl.program_id(1)
        c["slope"] = slope_ref[h, SLOPE_PARTS]
        for j in range(n_tiles):
            vt_ref[j, :BLOCK, :] = v_ref[0, j * t:(j + 1) * t, :].astype(F32).T.astype(BF16)
            vt_ref[j, BLOCK:, :] = jnp.ones((vt_ref.shape[1] - BLOCK, t), BF16)
        c["lam"] = (jnp.exp(jnp.sum(lq1_ref[...] * lk1_ref[...], axis=-1, keepdims=True))
                    - jnp.exp(jnp.sum(lq2_ref[...] * lk2_ref[...], axis=-1, keepdims=True)) + lam_init)
        c["lane"] = lax.broadcasted_iota(jnp.int32, (t, BLOCK), 1)
        key_index = lax.broadcasted_iota(jnp.int32, (t, BLOCK), 0)
        low, high = key_index & (V7X_MXU_COLUMNS - 1), key_index >> 8
        c["k_extra"] = jnp.where(c["lane"] < SLOPE_PARTS, low,
                                 jnp.where(c["lane"] < 2 * SLOPE_PARTS, high, 0)).astype(F32).astype(BF16)
        q_extra = jnp.zeros((2 * t, BLOCK), F32)
        lane2 = lax.broadcasted_iota(jnp.int32, (2 * t, BLOCK), 1)
        for i in range(SLOPE_PARTS):
            q_extra = jnp.where(lane2 == i, slope_ref[h, i], q_extra)
            q_extra = jnp.where(lane2 == SLOPE_PARTS + i, slope_ref[h, i] * float(V7X_MXU_COLUMNS), q_extra)
        c["q_extra"] = q_extra.astype(BF16)
        key = lax.broadcasted_iota(jnp.int32, (t, 2 * t), 0)
        query = lax.broadcasted_iota(jnp.int32, (t, 2 * t), 1)
        c["causal"] = key <= jnp.where(query >= t, query - t, query)

    def queries(qi):
        q = q_ref[0, qi * t:(qi + 1) * t, :]
        zero = jnp.zeros_like(q)
        qq = jnp.concatenate([jnp.where(c["lane"] < DIFF_HALF_DIM, q, zero),
                              jnp.where(c["lane"] >= DIFF_HALF_DIM, q, zero)], axis=0)
        return jnp.concatenate([qq, c["q_extra"]], axis=1)

    order = sorted(range(n_tiles), key=lambda qi: -qi)
    chains = [[] for _ in range(DIFF_CHAINS)]
    for qi in order:
        min(chains, key=len).extend((qi, j) for j in range(qi + 1))
    state = {}
    parts = [setup]
    for step in range(max(len(ch) for ch in chains)):
        live = [ch[step] for ch in chains if step < len(ch)]
        s, p = {}, {}

        def score(live=live, s=s):
            for qi, j in live:
                if j == 0:
                    state[qi] = dict(qq=queries(qi))
                k = jnp.concatenate([k_ref[0, j * t:(j + 1) * t, :], c["k_extra"]], axis=1)
                s[qi] = _dot_nt(k, state[qi]["qq"])
                if j == qi:
                    s[qi] = jnp.where(c["causal"], s[qi], NEG_INF)

        def softmax(live=live, s=s, p=p):
            for qi, j in live:
                st = state[qi]
                shift = c["slope"] * float((j - qi) * t)
                m_tile = jnp.max(s[qi], axis=0, keepdims=True) + shift
                st["m_old"], st["m"] = st.get("m"), (m_tile if j == 0 else jnp.maximum(st["m"], m_tile))
                p[qi] = jnp.exp2(s[qi] - (st["m"] - shift)).astype(BF16)

        def values(live=live, p=p):
            for qi, j in live:
                st = state[qi]
                acc_tile = _dot(vt_ref[j], p[qi])
                st["acc"] = acc_tile if j == 0 else jnp.exp2(st["m_old"] - st["m"]) * st["acc"] + acc_tile

        def finish(live=live):
            for qi, j in live:
                if j == qi:
                    acc = state.pop(qi)["acc"]
                    o = acc[:BLOCK] / acc[BLOCK:BLOCK + 1]
                    o = (o[:, :t] - c["lam"] * o[:, t:]).T
                    o_ref[0, qi * t:(qi + 1) * t, :] = (_rms_norm(o, g_ref[...])
                                                        * (1.0 - lam_init)).astype(o_ref.dtype)

        parts += [score, softmax, values, finish]
    return parts


def _dilated_parts(slope_ref, q_ref, k_ref, v_ref, g_ref, o_ref, qf, kf, vf, out_ref, lse_ref, *, seq, tc):
    c = {}

    def setup():
        c["slope"] = slope_ref[pl.program_id(1)]
        qf[...] = q_ref[0].astype(F32)
        kf[...] = k_ref[0].astype(F32)
        vf[...] = v_ref[0].astype(F32)
        qi = lax.broadcasted_iota(jnp.int32, (BLOCK, 2 * BLOCK), 0)
        kj = lax.broadcasted_iota(jnp.int32, (BLOCK, 2 * BLOCK), 1)
        c["rel"] = qi + BLOCK - kj

    def rows(start, size, dil):
        return pl.ds(start, size) if dil == 1 else pl.ds(start, size, stride=dil)

    parts = [setup]
    for branch, (window, dil) in enumerate(DIL_CONFIGS):
        nb = seq // dil // BLOCK
        step = dil * BLOCK

        def biases(branch=branch, dil=dil, n_back=window // dil):
            valid = (c["rel"] >= 0) & (c["rel"] <= n_back)
            c["bias", branch] = jnp.where(valid, (-c["slope"] * dil) * c["rel"].astype(F32), NEG_INF)

        parts.append(biases)
        groups = [([(cls, cls) for cls in range(dil)], True)]
        if nb > 1:
            groups.append(([(cls + n * step, cls + (n - 1) * step) for cls in range(dil) for n in range(1, nb)],
                           False))
        for starts, first in groups:
            nkeys = BLOCK if first else 2 * BLOCK
            q_rows = [rows(q_start, BLOCK, dil) for q_start, _ in starts]
            k_rows = [rows(k_start, nkeys, dil) for _, k_start in starts]
            v = {}

            def score(branch=branch, first=first, q_rows=q_rows, k_rows=k_rows, v=v):
                bias = c["bias", branch][:, BLOCK:] if first else c["bias", branch]
                v["s"] = [_dot_nt(qf[qr, :].astype(BF16), kf[kr, :].astype(BF16)) + bias
                          for qr, kr in zip(q_rows, k_rows)]

            def peak(v=v):
                v["m"] = [jnp.max(si, axis=-1, keepdims=True) for si in v["s"]]

            def weights(v=v):
                v["p"] = [jnp.exp2(si - mi).astype(BF16) for si, mi in zip(v["s"], v["m"])]

            def values(k_rows=k_rows, nkeys=nkeys, v=v):
                ones = jnp.ones((nkeys, BLOCK), BF16)
                v["r"] = [_dot(pi, jnp.concatenate([vf[kr, :].astype(BF16), ones], axis=1))
                          for pi, kr in zip(v["p"], k_rows)]

            def store(branch=branch, q_rows=q_rows, v=v):
                for qr, mi, ri in zip(q_rows, v["m"], v["r"]):
                    l = ri[:, BLOCK:]
                    out_ref[branch, qr, :] = ri[:, :BLOCK] / l
                    lse_ref[branch, qr, :] = mi + jnp.log2(l)

            parts += [score, peak, weights, values, store]

    def combine(t0):
        r = pl.ds(t0, tc)
        lses = [lse_ref[i, r, :] for i in range(len(DIL_CONFIGS))]
        top = functools.reduce(jnp.maximum, lses)
        ws = [jnp.exp2(lse - top) for lse in lses]
        num = sum(w * out_ref[i, r, :] for i, w in enumerate(ws))
        o_ref[0, r, :] = _rms_norm(num / sum(ws), g_ref[...]).astype(o_ref.dtype)

    return parts + [functools.partial(combine, t0) for t0 in range(0, seq, tc)]


def _mixer_attn_kernel(dslope_ref, sslope_ref, dq_ref, dk_ref, dv_ref, lq1_ref, lk1_ref, lq2_ref, lk2_ref,
                       gd_ref, sq_ref, sk_ref, sv_ref, gs_ref, od_ref, os_ref,
                       vt_ref, qf, kf, vf, out_ref, lse_ref, *, t, lam_init, seq, tc):
    _interleave(
        _diff_parts(dslope_ref, dq_ref, dk_ref, dv_ref, lq1_ref, lk1_ref, lq2_ref, lk2_ref, gd_ref, od_ref, vt_ref,
                    t=t, lam_init=lam_init),
        _dilated_parts(sslope_ref, sq_ref, sk_ref, sv_ref, gs_ref, os_ref, qf, kf, vf, out_ref, lse_ref,
                       seq=seq, tc=tc))


def _mixer_attention(proj, slopes_diff, slopes_dil, lq1, lk1, lq2, lk2, g_diff, g_dil, lam_init, heads, t):
    b, s, _ = proj.shape
    nbr = len(DIL_CONFIGS)
    vec = lambda n: pl.BlockSpec((1, n), lambda bi, hi: (0, 0))
    slab = lambda blk: pl.BlockSpec((1, s, BLOCK), lambda bi, hi: (bi, 0, blk + hi))
    smem = pl.BlockSpec(memory_space=pltpu.SMEM)
    out = jax.ShapeDtypeStruct((b, s, heads * BLOCK), BF16)
    return pl.pallas_call(
        functools.partial(_mixer_attn_kernel, t=t, lam_init=lam_init, seq=s, tc=BLOCK),
        grid=(b, heads),
        in_specs=[smem, smem, slab(0), slab(heads), slab(2 * heads),
                  vec(DIFF_HALF_DIM), vec(DIFF_HALF_DIM), vec(DIFF_HALF_DIM), vec(DIFF_HALF_DIM), vec(BLOCK),
                  slab(3 * heads), slab(4 * heads), slab(5 * heads), vec(BLOCK)],
        out_specs=[slab(0), slab(0)],
        out_shape=[out, out],
        scratch_shapes=[pltpu.VMEM((s // t, BLOCK + BF16_SUBLANES, t), BF16),
                        pltpu.VMEM((s, BLOCK), F32), pltpu.VMEM((s, BLOCK), F32), pltpu.VMEM((s, BLOCK), F32),
                        pltpu.VMEM((nbr, s, BLOCK), F32), pltpu.VMEM((nbr, s, BLOCK), F32)],
        compiler_params=_params(("arbitrary", "arbitrary")),
    )(slopes_diff, slopes_dil, proj, proj, proj, lq1, lk1, lq2, lk2, g_diff, proj, proj, proj, g_dil)


def _split_slopes(slopes):
    x = np.asarray(slopes, np.float64) * math.log2(math.e)
    x = x.astype(np.float32)
    parts, rest = [], x.copy()
    for _ in range(SLOPE_PARTS):
        part = rest.astype(BF16).astype(np.float32)
        parts.append(part)
        rest = rest - part
    return jnp.asarray(np.stack(parts + [x], axis=1), F32)


def _outproj_ln_kernel(*refs, n_in, alpha, tr):
    o_refs, w_refs = refs[:n_in], refs[n_in:2 * n_in]
    h_ref, g_ref, b_ref, out_ref = refs[2 * n_in:]
    tm, d = out_ref.shape

    def matmul_parts(r0):
        def part(c0):
            rows, cols = slice(r0, r0 + tr), slice(c0, c0 + V7X_MXU_COLUMNS)
            y = alpha * h_ref[rows, cols]
            for o_ref, w_ref in zip(o_refs, w_refs):
                y = y + _dot(o_ref[rows, :], w_ref[:, cols])
            out_ref[rows, cols] = y
        return [functools.partial(part, c0) for c0 in range(0, d, V7X_MXU_COLUMNS)]

    def norm_parts(r0):
        def part(r):
            rows = slice(r, r + ROW_CHUNK)
            out_ref[rows, :] = _layer_norm(out_ref[rows, :], g_ref[...], b_ref[...])
        return [functools.partial(part, r) for r in range(r0, r0 + tr, ROW_CHUNK)]

    pending = []
    for r0 in range(0, tm, tr):
        _interleave(matmul_parts(r0), pending)
        pending = norm_parts(r0)
    _interleave(pending)


def _outproj_ln(os, w, layer, h, g, b, alpha, tm, tr):
    m, d = h.shape
    n_in = len(os)
    kw = w.shape[1] // n_in
    return pl.pallas_call(
        functools.partial(_outproj_ln_kernel, n_in=n_in, alpha=alpha, tr=tr),
        grid=(m // tm,),
        in_specs=([pl.BlockSpec((tm, kw), lambda i: (i, 0)) for _ in os]
                  + [_resident((None, kw, d), functools.partial(lambda i, r: (layer, r, 0), r=r))
                     for r in range(n_in)]
                  + [pl.BlockSpec((tm, d), lambda i: (i, 0)),
                     pl.BlockSpec((1, d), lambda i: (0, 0)), pl.BlockSpec((1, d), lambda i: (0, 0))]),
        out_specs=pl.BlockSpec((tm, d), lambda i: (i, 0)),
        out_shape=jax.ShapeDtypeStruct((m, d), F32),
        compiler_params=_params(("arbitrary",)),
    )(*os, *([w] * n_in), h, g, b)


def _mem_attn_kernel(h_ref, wq_ref, kv_ref, o_ref, *, heads):
    d = h_ref.shape[1]
    hd = d // heads
    q = _dot(h_ref[...].astype(BF16), wq_ref[...]).astype(BF16)
    scale = hd ** -0.5
    for i in range(heads):
        k = kv_ref[0, :, i * hd:(i + 1) * hd]
        v = kv_ref[0, :, d + i * hd:d + (i + 1) * hd]
        s = _dot_nt(q[:, i * hd:(i + 1) * hd], k) * scale
        e = jnp.exp(s - jnp.max(s, axis=-1, keepdims=True))
        p = e * (1.0 / jnp.sum(e, axis=-1, keepdims=True))
        o_ref[:, i * hd:(i + 1) * hd] = _dot(p.astype(BF16), v).astype(o_ref.dtype)


def _mem_attention(h, wq, layer, kv, seq, tm):
    m, d = h.shape
    mem_len = kv.shape[1]
    per_seq = seq // tm
    return pl.pallas_call(
        functools.partial(_mem_attn_kernel, heads=MEM_HEADS),
        grid=(m // tm,),
        in_specs=[pl.BlockSpec((tm, d), lambda i: (i, 0)),
                  _resident((None, d, d), lambda i: (layer, 0, 0)),
                  pl.BlockSpec((1, mem_len, 2 * d), lambda i: (i // per_seq, 0, 0))],
        out_specs=pl.BlockSpec((tm, d), lambda i: (i, 0)),
        out_shape=jax.ShapeDtypeStruct((m, d), BF16),
        compiler_params=_params(("arbitrary",)),
    )(h, wq, kv)


def _ffn_kernel(x_ref, wg_ref, wu_ref, cwg_ref, cwu_ref, cbg_ref, cbu_ref, wd_ref, g_ref, b_ref, o_ref,
                xb_ref, hg0_ref, hu0_ref, hg1_ref, hu1_ref, act0_ref, act1_ref, carry_g_ref, carry_u_ref,
                *, tm, tiles_per_seq, alpha):
    i = pl.program_id(0)
    f = pl.program_id(1)
    nf = pl.num_programs(1) - 2
    seq_start = (i % tiles_per_seq) == 0
    h_refs = ((hg0_ref, hu0_ref), (hg1_ref, hu1_ref))
    act_refs = (act0_ref, act1_ref)
    acc_ref = o_ref

    tf = wg_ref.shape[1]
    d = wd_ref.shape[1]

    def up_project(slot):
        def part(which, c0):
            w_ref, hs_ref, carry_ref = ((wg_ref, wu_ref)[which], h_refs[slot][which],
                                        (carry_g_ref, carry_u_ref)[which])
            cols = slice(c0, c0 + V7X_MXU_COLUMNS)
            hs_ref[HALO:, cols] = _dot(xb_ref[...], w_ref[:, cols])
            hs_ref[:HALO, cols] = jnp.where(seq_start, 0.0, carry_ref[f, :, cols])
            carry_ref[f, :, cols] = hs_ref[tm:, cols]
        return [functools.partial(part, which, c0) for which in range(2)
                for c0 in range(0, tf, V7X_MXU_COLUMNS)]

    def conv(hs_ref, cw_ref, cb_ref, row0, cols):
        out = cb_ref[:, cols]
        for tap in range(CONV_WIDTH):
            off = row0 + HALO - (CONV_WIDTH - 1) + tap
            out = out + hs_ref[off:off + ROW_CHUNK, cols] * cw_ref[tap:tap + 1, cols]
        return out

    def activate(slot):
        def part(c0, row0):
            cols = slice(c0, c0 + BLOCK)
            gate = conv(h_refs[slot][0], cwg_ref, cbg_ref, row0, cols)
            up = conv(h_refs[slot][1], cwu_ref, cbu_ref, row0, cols)
            act = gate * (1.0 / (1.0 + jnp.exp2(gate * -math.log2(math.e)))) * up
            act_refs[slot][row0:row0 + ROW_CHUNK, cols] = act.astype(BF16)
        return [functools.partial(part, c0, row0) for c0 in range(0, tf, BLOCK)
                for row0 in range(0, tm, ROW_CHUNK)]

    def down_project(slot):
        def part(c0):
            cols = slice(c0, c0 + V7X_MXU_COLUMNS)
            acc_ref[:, cols] += _dot(act_refs[slot][...], wd_ref[:, cols])
        return [functools.partial(part, c0) for c0 in range(0, d, V7X_MXU_COLUMNS)]

    run = _interleave

    @pl.when(f == 0)
    def _():
        @pl.when(i == 0)
        def _():
            carry_g_ref[...] = jnp.zeros(carry_g_ref.shape, F32)
            carry_u_ref[...] = jnp.zeros(carry_u_ref.shape, F32)

        xb_ref[...] = x_ref[...].astype(BF16)
        acc_ref[...] = jnp.zeros(acc_ref.shape, F32)
        run(up_project(0))

    @pl.when(f == 1)
    def _():
        run(up_project(1), activate(0))

    for parity in range(2):
        @pl.when((f > 1) & (f < nf) & (f % 2 == parity))
        def _(parity=parity):
            run(up_project(parity), activate(1 - parity), down_project(parity))

    @pl.when(f == nf)
    def _():
        run(down_project((nf - 2) % 2), activate((nf - 1) % 2))

    @pl.when(f == nf + 1)
    def _():
        run(down_project((nf - 1) % 2))
        o_ref[...] = _layer_norm(alpha * x_ref[...] + acc_ref[...], g_ref[...], b_ref[...])


def _conv_ffn_ln(x, w_up, w_down, layer, conv_w, conv_b, g, b, alpha, seq, tm, tf):
    m, d = x.shape
    d_ff = w_down.shape[1]
    nf = d_ff // tf
    row = lambda i, f: (i, 0)
    fixed = lambda i, f: (0, 0)
    chunk = lambda f, lag: jnp.clip(f - lag, 0, nf - 1)
    h_buf = pltpu.VMEM((tm + HALO, tf), F32)
    act_buf = pltpu.VMEM((tm, tf), BF16)
    return pl.pallas_call(
        functools.partial(_ffn_kernel, tm=tm, tiles_per_seq=seq // tm, alpha=alpha),
        grid=(m // tm, nf + 2),
        in_specs=[pl.BlockSpec((tm, d), row),
                  pl.BlockSpec((None, d, tf), lambda i, f: (layer, 0, chunk(f, 0))),
                  pl.BlockSpec((None, d, tf), lambda i, f: (layer, 0, nf + chunk(f, 0))),
                  pl.BlockSpec((CONV_WIDTH, tf), lambda i, f: (0, chunk(f, 1))),
                  pl.BlockSpec((CONV_WIDTH, tf), lambda i, f: (0, nf + chunk(f, 1))),
                  pl.BlockSpec((1, tf), lambda i, f: (0, chunk(f, 1))),
                  pl.BlockSpec((1, tf), lambda i, f: (0, nf + chunk(f, 1))),
                  pl.BlockSpec((None, tf, d), lambda i, f: (layer, chunk(f, 2), 0)),
                  pl.BlockSpec((1, d), fixed), pl.BlockSpec((1, d), fixed)],
        out_specs=pl.BlockSpec((tm, d), row),
        out_shape=jax.ShapeDtypeStruct((m, d), F32),
        scratch_shapes=[pltpu.VMEM((tm, d), BF16), h_buf, h_buf, h_buf, h_buf, act_buf, act_buf,
                        pltpu.VMEM((nf, HALO, tf), F32), pltpu.VMEM((nf, HALO, tf), F32)],
        compiler_params=_params(("arbitrary", "arbitrary")),
    )(x, w_up, w_up, conv_w, conv_w, conv_b, conv_b, w_down, g, b)


def _alibi_slopes(n):
    return (2.0 ** (-8.0 * np.arange(1, n + 1) / n)).astype(np.float32)


def kernel(x, mem, w_in, w_mix_out, lambda_q1, lambda_k1, lambda_q2, lambda_k2, g_diff, g_dil, ln1_g, ln1_b,
           w_mem_q, w_mem_kv, w_mem_o, ln2_g, ln2_b, w_up, conv_w, conv_b, w_down, ln3_g, ln3_b):
    batch, seq, d = x.shape
    depth = w_in.shape[0]
    mem_len = mem.shape[1]
    diff_heads = dil_heads = d // 256
    alpha = (2 * depth) ** 0.25
    slopes = _alibi_slopes(diff_heads + dil_heads)
    log2e = math.log2(math.e)
    slopes_diff = _split_slopes(slopes[0::2])
    slopes_dil = jnp.asarray(slopes[1::2].astype(np.float64) * log2e, F32)
    row = lambda a: a.reshape(1, -1)

    group = diff_heads * BLOCK
    col_scale = np.ones((1, 6 * group), np.float32)
    col_scale[:, :group] = DIFF_HALF_DIM ** -0.5 * log2e
    col_scale[:, 3 * group:4 * group] = BLOCK ** -0.5 * log2e
    col_scale = jnp.asarray(col_scale)
    kv_scale = jnp.ones((1, 2 * d), F32)

    w_in = w_in.astype(BF16)
    later_weights = (w_mix_out, w_mem_q, w_mem_kv, w_mem_o, w_up, w_down)

    h = x.reshape(batch * seq, d)
    mem2 = mem.reshape(batch * mem_len, d)
    for l in range(depth):
        lam_init = 0.8 - 0.6 * math.exp(-0.3 * l)
        if l == 0:
            proj, w_mix_out, w_mem_q, w_mem_kv, w_mem_o, w_up, w_down = _project(
                h, w_in, l, col_scale, tm=1024, tn=1024, cast=later_weights)
        else:
            proj, = _project(h, w_in, l, col_scale, tm=1024, tn=1024)
        proj = proj.reshape(batch, seq, -1)
        o_diff, o_dil = _mixer_attention(proj, slopes_diff, slopes_dil, row(lambda_q1[l]), row(lambda_k1[l]),
                                         row(lambda_q2[l]), row(lambda_k2[l]), row(g_diff[l]), row(g_dil[l]),
                                         lam_init, diff_heads, t=512)
        h = _outproj_ln([o_diff.reshape(batch * seq, -1), o_dil.reshape(batch * seq, -1)],
                        w_mix_out, l, h, row(ln1_g[l]), row(ln1_b[l]), alpha, tm=1024, tr=512)

        kv, = _project(mem2, w_mem_kv, l, kv_scale, tm=1024, tn=1024)
        kv = kv.reshape(batch, mem_len, 2 * d)
        o_mem = _mem_attention(h, w_mem_q, l, kv, seq, tm=512)
        h = _outproj_ln([o_mem], w_mem_o, l, h, row(ln2_g[l]), row(ln2_b[l]), alpha, tm=1024, tr=512)

        h = _conv_ffn_ln(h, w_up, w_down, l, conv_w[l], row(conv_b[l]), row(ln3_g[l]), row(ln3_b[l]),
                         alpha, seq, tm=512, tf=512)
    return h.reshape(batch, seq, d)
```

```python
import functools
import math

import numpy as np
import jax
import jax.numpy as jnp
from jax import lax
from jax.experimental import pallas as pl
from jax.experimental.pallas import tpu as pltpu

F32 = jnp.float32
BF16 = jnp.bfloat16

LN_EPS = 1e-5
RMS_EPS = 1e-5
BLOCK = 128
DIFF_HALF_DIM = 64
DIL_CONFIGS = ((128, 1), (512, 4), (2048, 16))
DIFF_CHAINS = 2
MEM_HEADS = 4
CONV_WIDTH = 3
NEG_INF = float("-inf")
HALO = 8
ROW_CHUNK = 64
SLOPE_PARTS = 3

V7X_VMEM_LIMIT_BYTES = 56 * 1024 * 1024
V7X_MXU_COLUMNS = 256
BF16_SUBLANES = 16


def _params(semantics):
    return pltpu.CompilerParams(dimension_semantics=semantics, vmem_limit_bytes=V7X_VMEM_LIMIT_BYTES)


def _resident(shape, index_map):
    return pl.BlockSpec(shape, index_map, pipeline_mode=pl.Buffered(1))


def _layer_norm(z, g, b):
    mu = jnp.mean(z, axis=-1, keepdims=True)
    zc = z - mu
    var = jnp.mean(zc * zc, axis=-1, keepdims=True)
    return zc * lax.rsqrt(var + LN_EPS) * g + b


def _rms_norm(o, g):
    return o * lax.rsqrt(jnp.mean(o * o, axis=-1, keepdims=True) + RMS_EPS) * g


def _interleave(*stages):
    order = sorted(((k + 0.5) / len(parts), s, k) for s, parts in enumerate(stages) for k in range(len(parts)))
    for _, s, k in order:
        stages[s][k]()


def _dot(a, b):
    return jnp.dot(a, b, preferred_element_type=F32)


def _dot_nt(a, b):
    return lax.dot_general(a, b, (((1,), (1,)), ((), ())), preferred_element_type=F32)


def _proj_kernel(x_ref, w_ref, cs_ref, *refs):
    n_cast = (len(refs) - 2) // 2
    cast_in, o_ref, cast_out, xb_ref = refs[:n_cast], refs[n_cast], refs[n_cast + 1:-1], refs[-1]

    @pl.when(pl.program_id(1) == 0)
    def _():
        xb_ref[...] = x_ref[...].astype(BF16)

    o_ref[...] = (_dot(xb_ref[...], w_ref[...]) * cs_ref[...]).astype(o_ref.dtype)
    for src, dst in zip(cast_in, cast_out):
        dst[...] = src[...].astype(BF16)


def _project(x, w, layer, col_scale, tm, tn, cast=()):
    m, k = x.shape
    n = w.shape[2]
    tm = min(tm, m)
    grid = (m // tm, n // tn)
    steps = grid[0] * grid[1]
    flat = [c.reshape(-1, c.shape[-1]) for c in cast]
    block_rows = [next(br for br in range(BF16_SUBLANES, f.shape[0] + 1, BF16_SUBLANES)
                       if f.shape[0] % br == 0 and f.shape[0] // br <= steps) for f in flat]

    def cast_spec(f, br):
        last = f.shape[0] // br - 1
        return pl.BlockSpec((br, f.shape[1]), lambda i, j: (jnp.minimum(i * grid[1] + j, last), 0))

    cast_specs = [cast_spec(f, br) for f, br in zip(flat, block_rows)]
    outs = pl.pallas_call(
        _proj_kernel,
        grid=grid,
        in_specs=[pl.BlockSpec((tm, k), lambda i, j: (i, 0)),
                  pl.BlockSpec((None, k, tn), lambda i, j: (layer, 0, j)),
                  pl.BlockSpec((1, tn), lambda i, j: (0, j))] + cast_specs,
        out_specs=[pl.BlockSpec((tm, tn), lambda i, j: (i, j))] + cast_specs,
        out_shape=[jax.ShapeDtypeStruct((m, n), BF16)] + [jax.ShapeDtypeStruct(f.shape, BF16) for f in flat],
        scratch_shapes=[pltpu.VMEM((tm, k), BF16)],
        compiler_params=_params(("arbitrary", "arbitrary")),
    )(x, w, col_scale, *flat)
    return (outs[0], *(o.reshape(c.shape) for o, c in zip(outs[1:], cast)))


def _diff_parts(slope_ref, q_ref, k_ref, v_ref, lq1_ref, lk1_ref, lq2_ref, lk2_ref, g_ref, o_ref, vt_ref,
                *, t, lam_init):
    c = {}
    n_tiles = vt_ref.shape[0]
    assert t <= 2 * V7X_MXU_COLUMNS and V7X_MXU_COLUMNS == 1 << 8

    def setup():
        h = pl.program_id(1)
        c["slope"] = slope_ref[h, SLOPE_PARTS]
        for j in range(n_tiles):
            vt_ref[j, :BLOCK, :] = v_ref[0, j * t:(j + 1) * t, :].astype(F32).T.astype(BF16)
            vt_ref[j, BLOCK:, :] = jnp.ones((vt_ref.shape[1] - BLOCK, t), BF16)
        c["lam"] = (jnp.exp(jnp.sum(lq1_ref[...] * lk1_ref[...], axis=-1, keepdims=True))
                    - jnp.exp(jnp.sum(lq2_ref[...] * lk2_ref[...], axis=-1, keepdims=True)) + lam_init)
        c["lane"] = lax.broadcasted_iota(jnp.int32, (t, BLOCK), 1)
        key_index = lax.broadcasted_iota(jnp.int32, (t, BLOCK), 0)
        low, high = key_index & (V7X_MXU_COLUMNS - 1), key_index >> 8
        c["k_extra"] = jnp.where(c["lane"] < SLOPE_PARTS, low,
                                 jnp.where(c["lane"] < 2 * SLOPE_PARTS, high, 0)).astype(F32).astype(BF16)
        q_extra = jnp.zeros((2 * t, BLOCK), F32)
        lane2 = lax.broadcasted_iota(jnp.int32, (2 * t, BLOCK), 1)
        for i in range(SLOPE_PARTS):
            q_extra = jnp.where(lane2 == i, slope_ref[h, i], q_extra)
            q_extra = jnp.where(lane2 == SLOPE_PARTS + i, slope_ref[h, i] * float(V7X_MXU_COLUMNS), q_extra)
        c["q_extra"] = q_extra.astype(BF16)
        key = lax.broadcasted_iota(jnp.int32, (t, 2 * t), 0)
        query = lax.broadcasted_iota(jnp.int32, (t, 2 * t), 1)
        c["causal"] = key <= jnp.where(query >= t, query - t, query)

    def queries(qi):
        q = q_ref[0, qi * t:(qi + 1) * t, :]
        zero = jnp.zeros_like(q)
        qq = jnp.concatenate([jnp.where(c["lane"] < DIFF_HALF_DIM, q, zero),
                              jnp.where(c["lane"] >= DIFF_HALF_DIM, q, zero)], axis=0)
        return jnp.concatenate([qq, c["q_extra"]], axis=1)

    order = sorted(range(n_tiles), key=lambda qi: -qi)
    chains = [[] for _ in range(DIFF_CHAINS)]
    for qi in order:
        min(chains, key=len).extend((qi, j) for j in range(qi + 1))
    state = {}
    parts = [setup]
    for step in range(max(len(ch) for ch in chains)):
        live = [ch[step] for ch in chains if step < len(ch)]
        s, p = {}, {}

        def score(live=live, s=s):
            for qi, j in live:
                if j == 0:
                    state[qi] = dict(qq=queries(qi))
                k = jnp.concatenate([k_ref[0, j * t:(j + 1) * t, :], c["k_extra"]], axis=1)
                s[qi] = _dot_nt(k, state[qi]["qq"])
                if j == qi:
                    s[qi] = jnp.where(c["causal"], s[qi], NEG_INF)

        def softmax(live=live, s=s, p=p):
            for qi, j in live:
                st = state[qi]
                shift = c["slope"] * float((j - qi) * t)
                m_tile = jnp.max(s[qi], axis=0, keepdims=True) + shift
                st["m_old"], st["m"] = st.get("m"), (m_tile if j == 0 else jnp.maximum(st["m"], m_tile))
                p[qi] = jnp.exp2(s[qi] - (st["m"] - shift)).astype(BF16)

        def values(live=live, p=p):
            for qi, j in live:
                st = state[qi]
                acc_tile = _dot(vt_ref[j], p[qi])
                st["acc"] = acc_tile if j == 0 else jnp.exp2(st["m_old"] - st["m"]) * st["acc"] + acc_tile

        def finish(live=live):
            for qi, j in live:
                if j == qi:
                    acc = state.pop(qi)["acc"]
                    o = acc[:BLOCK] / acc[BLOCK:BLOCK + 1]
                    o = (o[:, :t] - c["lam"] * o[:, t:]).T
                    o_ref[0, qi * t:(qi + 1) * t, :] = (_rms_norm(o, g_ref[...])
                                                        * (1.0 - lam_init)).astype(o_ref.dtype)

        parts += [score, softmax, values, finish]
    return parts


def _dilated_parts(slope_ref, q_ref, k_ref, v_ref, g_ref, o_ref, qf, kf, vf, out_ref, lse_ref, *, seq, tc):
    c = {}

    def setup():
        c["slope"] = slope_ref[pl.program_id(1)]
        qf[...] = q_ref[0].astype(F32)
        kf[...] = k_ref[0].astype(F32)
        vf[...] = v_ref[0].astype(F32)
        qi = lax.broadcasted_iota(jnp.int32, (BLOCK, 2 * BLOCK), 0)
        kj = lax.broadcasted_iota(jnp.int32, (BLOCK, 2 * BLOCK), 1)
        c["rel"] = qi + BLOCK - kj

    def rows(start, size, dil):
        return pl.ds(start, size) if dil == 1 else pl.ds(start, size, stride=dil)

    parts = [setup]
    for branch, (window, dil) in enumerate(DIL_CONFIGS):
        nb = seq // dil // BLOCK
        step = dil * BLOCK

        def biases(branch=branch, dil=dil, n_back=window // dil):
            valid = (c["rel"] >= 0) & (c["rel"] <= n_back)
            c["bias", branch] = jnp.where(valid, (-c["slope"] * dil) * c["rel"].astype(F32), NEG_INF)

        parts.append(biases)
        groups = [([(cls, cls) for cls in range(dil)], True)]
        if nb > 1:
            groups.append(([(cls + n * step, cls + (n - 1) * step) for cls in range(dil) for n in range(1, nb)],
                           False))
        for starts, first in groups:
            nkeys = BLOCK if first else 2 * BLOCK
            q_rows = [rows(q_start, BLOCK, dil) for q_start, _ in starts]
            k_rows = [rows(k_start, nkeys, dil) for _, k_start in starts]
            v = {}

            def score(branch=branch, first=first, q_rows=q_rows, k_rows=k_rows, v=v):
                bias = c["bias", branch][:, BLOCK:] if first else c["bias", branch]
                v["s"] = [_dot_nt(qf[qr, :].astype(BF16), kf[kr, :].astype(BF16)) + bias
                          for qr, kr in zip(q_rows, k_rows)]

            def peak(v=v):
                v["m"] = [jnp.max(si, axis=-1, keepdims=True) for si in v["s"]]

            def weights(v=v):
                v["p"] = [jnp.exp2(si - mi).astype(BF16) for si, mi in zip(v["s"], v["m"])]

            def values(k_rows=k_rows, nkeys=nkeys, v=v):
                ones = jnp.ones((nkeys, BLOCK), BF16)
                v["r"] = [_dot(pi, jnp.concatenate([vf[kr, :].astype(BF16), ones], axis=1))
                          for pi, kr in zip(v["p"], k_rows)]

            def store(branch=branch, q_rows=q_rows, v=v):
                for qr, mi, ri in zip(q_rows, v["m"], v["r"]):
                    l = ri[:, BLOCK:]
                    out_ref[branch, qr, :] = ri[:, :BLOCK] / l
                    lse_ref[branch, qr, :] = mi + jnp.log2(l)

            parts += [score, peak, weights, values, store]

    def combine(t0):
        r = pl.ds(t0, tc)
        lses = [lse_ref[i, r, :] for i in range(len(DIL_CONFIGS))]
        top = functools.reduce(jnp.maximum, lses)
        ws = [jnp.exp2(lse - top) for lse in lses]
        num = sum(w * out_ref[i, r, :] for i, w in enumerate(ws))
        o_ref[0, r, :] = _rms_norm(num / sum(ws), g_ref[...]).astype(o_ref.dtype)

    return parts + [functools.partial(combine, t0) for t0 in range(0, seq, tc)]


def _mixer_attn_kernel(dslope_ref, sslope_ref, dq_ref, dk_ref, dv_ref, lq1_ref, lk1_ref, lq2_ref, lk2_ref,
                       gd_ref, sq_ref, sk_ref, sv_ref, gs_ref, od_ref, os_ref,
                       vt_ref, qf, kf, vf, out_ref, lse_ref, *, t, lam_init, seq, tc):
    _interleave(
        _diff_parts(dslope_ref, dq_ref, dk_ref, dv_ref, lq1_ref, lk1_ref, lq2_ref, lk2_ref, gd_ref, od_ref, vt_ref,
                    t=t, lam_init=lam_init),
        _dilated_parts(sslope_ref, sq_ref, sk_ref, sv_ref, gs_ref, os_ref, qf, kf, vf, out_ref, lse_ref,
                       seq=seq, tc=tc))


def _mixer_attention(proj, slopes_diff, slopes_dil, lq1, lk1, lq2, lk2, g_diff, g_dil, lam_init, heads, t):
    b, s, _ = proj.shape
    nbr = len(DIL_CONFIGS)
    vec = lambda n: pl.BlockSpec((1, n), lambda bi, hi: (0, 0))
    slab = lambda blk: pl.BlockSpec((1, s, BLOCK), lambda bi, hi: (bi, 0, blk + hi))
    smem = pl.BlockSpec(memory_space=pltpu.SMEM)
    out = jax.ShapeDtypeStruct((b, s, heads * BLOCK), BF16)
    return pl.pallas_call(
        functools.partial(_mixer_attn_kernel, t=t, lam_init=lam_init, seq=s, tc=BLOCK),
        grid=(b, heads),
        in_specs=[smem, smem, slab(0), slab(heads), slab(2 * heads),
                  vec(DIFF_HALF_DIM), vec(DIFF_HALF_DIM), vec(DIFF_HALF_DIM), vec(DIFF_HALF_DIM), vec(BLOCK),
                  slab(3 * heads), slab(4 * heads), slab(5 * heads), vec(BLOCK)],
        out_specs=[slab(0), slab(0)],
        out_shape=[out, out],
        scratch_shapes=[pltpu.VMEM((s // t, BLOCK + BF16_SUBLANES, t), BF16),
                        pltpu.VMEM((s, BLOCK), F32), pltpu.VMEM((s, BLOCK), F32), pltpu.VMEM((s, BLOCK), F32),
                        pltpu.VMEM((nbr, s, BLOCK), F32), pltpu.VMEM((nbr, s, BLOCK), F32)],
        compiler_params=_params(("arbitrary", "arbitrary")),
    )(slopes_diff, slopes_dil, proj, proj, proj, lq1, lk1, lq2, lk2, g_diff, proj, proj, proj, g_dil)


def _split_slopes(slopes):
    x = np.asarray(slopes, np.float64) * math.log2(math.e)
    x = x.astype(np.float32)
    parts, rest = [], x.copy()
    for _ in range(SLOPE_PARTS):
        part = rest.astype(BF16).astype(np.float32)
        parts.append(part)
        rest = rest - part
    return jnp.asarray(np.stack(parts + [x], axis=1), F32)


def _outproj_ln_kernel(*refs, n_in, alpha, tr):
    o_refs, w_refs = refs[:n_in], refs[n_in:2 * n_in]
    h_ref, g_ref, b_ref, out_ref = refs[2 * n_in:]
    tm, d = out_ref.shape

    def matmul_parts(r0):
        def part(c0):
            rows, cols = slice(r0, r0 + tr), slice(c0, c0 + V7X_MXU_COLUMNS)
            y = alpha * h_ref[rows, cols]
            for o_ref, w_ref in zip(o_refs, w_refs):
                y = y + _dot(o_ref[rows, :], w_ref[:, cols])
            out_ref[rows, cols] = y
        return [functools.partial(part, c0) for c0 in range(0, d, V7X_MXU_COLUMNS)]

    def norm_parts(r0):
        def part(r):
            rows = slice(r, r + ROW_CHUNK)
            out_ref[rows, :] = _layer_norm(out_ref[rows, :], g_ref[...], b_ref[...])
        return [functools.partial(part, r) for r in range(r0, r0 + tr, ROW_CHUNK)]

    pending = []
    for r0 in range(0, tm, tr):
        _interleave(matmul_parts(r0), pending)
        pending = norm_parts(r0)
    _interleave(pending)


def _outproj_ln(os, w, layer, h, g, b, alpha, tm, tr):
    m, d = h.shape
    n_in = len(os)
    kw = w.shape[1] // n_in
    return pl.pallas_call(
        functools.partial(_outproj_ln_kernel, n_in=n_in, alpha=alpha, tr=tr),
        grid=(m // tm,),
        in_specs=([pl.BlockSpec((tm, kw), lambda i: (i, 0)) for _ in os]
                  + [_resident((None, kw, d), functools.partial(lambda i, r: (layer, r, 0), r=r))
                     for r in range(n_in)]
                  + [pl.BlockSpec((tm, d), lambda i: (i, 0)),
                     pl.BlockSpec((1, d), lambda i: (0, 0)), pl.BlockSpec((1, d), lambda i: (0, 0))]),
        out_specs=pl.BlockSpec((tm, d), lambda i: (i, 0)),
        out_shape=jax.ShapeDtypeStruct((m, d), F32),
        compiler_params=_params(("arbitrary",)),
    )(*os, *([w] * n_in), h, g, b)


def _mem_attn_kernel(h_ref, wq_ref, kv_ref, o_ref, *, heads):
    d = h_ref.shape[1]
    hd = d // heads
    q = _dot(h_ref[...].astype(BF16), wq_ref[...]).astype(BF16)
    scale = hd ** -0.5
    for i in range(heads):
        k = kv_ref[0, :, i * hd:(i + 1) * hd]
        v = kv_ref[0, :, d + i * hd:d + (i + 1) * hd]
        s = _dot_nt(q[:, i * hd:(i + 1) * hd], k) * scale
        e = jnp.exp(s - jnp.max(s, axis=-1, keepdims=True))
        p = e * (1.0 / jnp.sum(e, axis=-1, keepdims=True))
        o_ref[:, i * hd:(i + 1) * hd] = _dot(p.astype(BF16), v).astype(o_ref.dtype)


def _mem_attention(h, wq, layer, kv, seq, tm):
    m, d = h.shape
    mem_len = kv.shape[1]
    per_seq = seq // tm
    return pl.pallas_call(
        functools.partial(_mem_attn_kernel, heads=MEM_HEADS),
        grid=(m // tm,),
        in_specs=[pl.BlockSpec((tm, d), lambda i: (i, 0)),
                  _resident((None, d, d), lambda i: (layer, 0, 0)),
                  pl.BlockSpec((1, mem_len, 2 * d), lambda i: (i // per_seq, 0, 0))],
        out_specs=pl.BlockSpec((tm, d), lambda i: (i, 0)),
        out_shape=jax.ShapeDtypeStruct((m, d), BF16),
        compiler_params=_params(("arbitrary",)),
    )(h, wq, kv)


def _ffn_kernel(x_ref, wg_ref, wu_ref, cwg_ref, cwu_ref, cbg_ref, cbu_ref, wd_ref, g_ref, b_ref, o_ref,
                xb_ref, hg0_ref, hu0_ref, hg1_ref, hu1_ref, act0_ref, act1_ref, carry_g_ref, carry_u_ref,
                *, tm, tiles_per_seq, alpha):
    i = pl.program_id(0)
    f = pl.program_id(1)
    nf = pl.num_programs(1) - 2
    seq_start = (i % tiles_per_seq) == 0
    h_refs = ((hg0_ref, hu0_ref), (hg1_ref, hu1_ref))
    act_refs = (act0_ref, act1_ref)
    acc_ref = o_ref

    tf = wg_ref.shape[1]
    d = wd_ref.shape[1]

    def up_project(slot):
        def part(which, c0):
            w_ref, hs_ref, carry_ref = ((wg_ref, wu_ref)[which], h_refs[slot][which],
                                        (carry_g_ref, carry_u_ref)[which])
            cols = slice(c0, c0 + V7X_MXU_COLUMNS)
            hs_ref[HALO:, cols] = _dot(xb_ref[...], w_ref[:, cols])
            hs_ref[:HALO, cols] = jnp.where(seq_start, 0.0, carry_ref[f, :, cols])
            carry_ref[f, :, cols] = hs_ref[tm:, cols]
        return [functools.partial(part, which, c0) for which in range(2)
                for c0 in range(0, tf, V7X_MXU_COLUMNS)]

    def conv(hs_ref, cw_ref, cb_ref, row0, cols):
        out = cb_ref[:, cols]
        for tap in range(CONV_WIDTH):
            off = row0 + HALO - (CONV_WIDTH - 1) + tap
            out = out + hs_ref[off:off + ROW_CHUNK, cols] * cw_ref[tap:tap + 1, cols]
        return out

    def activate(slot):
        def part(c0, row0):
            cols = slice(c0, c0 + BLOCK)
            gate = conv(h_refs[slot][0], cwg_ref, cbg_ref, row0, cols)
            up = conv(h_refs[slot][1], cwu_ref, cbu_ref, row0, cols)
            act = gate * (1.0 / (1.0 + jnp.exp2(gate * -math.log2(math.e)))) * up
            act_refs[slot][row0:row0 + ROW_CHUNK, cols] = act.astype(BF16)
        return [functools.partial(part, c0, row0) for c0 in range(0, tf, BLOCK)
                for row0 in range(0, tm, ROW_CHUNK)]

    def down_project(slot):
        def part(c0):
            cols = slice(c0, c0 + V7X_MXU_COLUMNS)
            acc_ref[:, cols] += _dot(act_refs[slot][...], wd_ref[:, cols])
        return [functools.partial(part, c0) for c0 in range(0, d, V7X_MXU_COLUMNS)]

    run = _interleave

    @pl.when(f == 0)
    def _():
        @pl.when(i == 0)
        def _():
            carry_g_ref[...] = jnp.zeros(carry_g_ref.shape, F32)
            carry_u_ref[...] = jnp.zeros(carry_u_ref.shape, F32)

        xb_ref[...] = x_ref[...].astype(BF16)
        acc_ref[...] = jnp.zeros(acc_ref.shape, F32)
        run(up_project(0))

    @pl.when(f == 1)
    def _():
        run(up_project(1), activate(0))

    for parity in range(2):
        @pl.when((f > 1) & (f < nf) & (f % 2 == parity))
        def _(parity=parity):
            run(up_project(parity), activate(1 - parity), down_project(parity))

    @pl.when(f == nf)
    def _():
        run(down_project((nf - 2) % 2), activate((nf - 1) % 2))

    @pl.when(f == nf + 1)
    def _():
        run(down_project((nf - 1) % 2))
        o_ref[...] = _layer_norm(alpha * x_ref[...] + acc_ref[...], g_ref[...], b_ref[...])


def _conv_ffn_ln(x, w_up, w_down, layer, conv_w, conv_b, g, b, alpha, seq, tm, tf):
    m, d = x.shape
    d_ff = w_down.shape[1]
    nf = d_ff // tf
    row = lambda i, f: (i, 0)
    fixed = lambda i, f: (0, 0)
    chunk = lambda f, lag: jnp.clip(f - lag, 0, nf - 1)
    h_buf = pltpu.VMEM((tm + HALO, tf), F32)
    act_buf = pltpu.VMEM((tm, tf), BF16)
    return pl.pallas_call(
        functools.partial(_ffn_kernel, tm=tm, tiles_per_seq=seq // tm, alpha=alpha),
        grid=(m // tm, nf + 2),
        in_specs=[pl.BlockSpec((tm, d), row),
                  pl.BlockSpec((None, d, tf), lambda i, f: (layer, 0, chunk(f, 0))),
                  pl.BlockSpec((None, d, tf), lambda i, f: (layer, 0, nf + chunk(f, 0))),
                  pl.BlockSpec((CONV_WIDTH, tf), lambda i, f: (0, chunk(f, 1))),
                  pl.BlockSpec((CONV_WIDTH, tf), lambda i, f: (0, nf + chunk(f, 1))),
                  pl.BlockSpec((1, tf), lambda i, f: (0, chunk(f, 1))),
                  pl.BlockSpec((1, tf), lambda i, f: (0, nf + chunk(f, 1))),
                  pl.BlockSpec((None, tf, d), lambda i, f: (layer, chunk(f, 2), 0)),
                  pl.BlockSpec((1, d), fixed), pl.BlockSpec((1, d), fixed)],
        out_specs=pl.BlockSpec((tm, d), row),
        out_shape=jax.ShapeDtypeStruct((m, d), F32),
        scratch_shapes=[pltpu.VMEM((tm, d), BF16), h_buf, h_buf, h_buf, h_buf, act_buf, act_buf,
                        pltpu.VMEM((nf, HALO, tf), F32), pltpu.VMEM((nf, HALO, tf), F32)],
        compiler_params=_params(("arbitrary", "arbitrary")),
    )(x, w_up, w_up, conv_w, conv_w, conv_b, conv_b, w_down, g, b)


def _alibi_slopes(n):
    return (2.0 ** (-8.0 * np.arange(1, n + 1) / n)).astype(np.float32)


def kernel(x, mem, w_in, w_mix_out, lambda_q1, lambda_k1, lambda_q2, lambda_k2, g_diff, g_dil, ln1_g, ln1_b,
           w_mem_q, w_mem_kv, w_mem_o, ln2_g, ln2_b, w_up, conv_w, conv_b, w_down, ln3_g, ln3_b):
    batch, seq, d = x.shape
    depth = w_in.shape[0]
    mem_len = mem.shape[1]
    diff_heads = dil_heads = d // 256
    alpha = (2 * depth) ** 0.25
    slopes = _alibi_slopes(diff_heads + dil_heads)
    log2e = math.log2(math.e)
    slopes_diff = _split_slopes(slopes[0::2])
    slopes_dil = jnp.asarray(slopes[1::2].astype(np.float64) * log2e, F32)
    row = lambda a: a.reshape(1, -1)

    group = diff_heads * BLOCK
    col_scale = np.ones((1, 6 * group), np.float32)
    col_scale[:, :group] = DIFF_HALF_DIM ** -0.5 * log2e
    col_scale[:, 3 * group:4 * group] = BLOCK ** -0.5 * log2e
    col_scale = jnp.asarray(col_scale)
    kv_scale = jnp.ones((1, 2 * d), F32)

    w_in = w_in.astype(BF16)
    later_weights = (w_mix_out, w_mem_q, w_mem_kv, w_mem_o, w_up, w_down)

    h = x.reshape(batch * seq, d)
    mem2 = mem.reshape(batch * mem_len, d)
    for l in range(depth):
        lam_init = 0.8 - 0.6 * math.exp(-0.3 * l)
        if l == 0:
            proj, w_mix_out, w_mem_q, w_mem_kv, w_mem_o, w_up, w_down = _project(
                h, w_in, l, col_scale, tm=1024, tn=1024, cast=later_weights)
        else:
            proj, = _project(h, w_in, l, col_scale, tm=1024, tn=1024)
        proj = proj.reshape(batch, seq, -1)
        o_diff, o_dil = _mixer_attention(proj, slopes_diff, slopes_dil, row(lambda_q1[l]), row(lambda_k1[l]),
                                         row(lambda_q2[l]), row(lambda_k2[l]), row(g_diff[l]), row(g_dil[l]),
                                         lam_init, diff_heads, t=512)
        h = _outproj_ln([o_diff.reshape(batch * seq, -1), o_dil.reshape(batch * seq, -1)],
                        w_mix_out, l, h, row(ln1_g[l]), row(ln1_b[l]), alpha, tm=1024, tr=256)

        kv, = _project(mem2, w_mem_kv, l, kv_scale, tm=1024, tn=1024)
        kv = kv.reshape(batch, mem_len, 2 * d)
        o_mem = _mem_attention(h, w_mem_q, l, kv, seq, tm=512)
        h = _outproj_ln([o_mem], w_mem_o, l, h, row(ln2_g[l]), row(ln2_b[l]), alpha, tm=1024, tr=256)

        h = _conv_ffn_ln(h, w_up, w_down, l, conv_w[l], row(conv_b[l]), row(ln3_g[l]), row(ln3_b[l]),
                         alpha, seq, tm=512, tf=512)
    return h.reshape(batch, seq, d)
```

```python
import functools
import math

import numpy as np
import jax
import jax.numpy as jnp
from jax import lax
from jax.experimental import pallas as pl
from jax.experimental.pallas import tpu as pltpu

F32 = jnp.float32
BF16 = jnp.bfloat16

LN_EPS = 1e-5
RMS_EPS = 1e-5
BLOCK = 128
DIFF_HALF_DIM = 64
DIL_CONFIGS = ((128, 1), (512, 4), (2048, 16))
DIFF_CHAINS = 2
MEM_HEADS = 4
CONV_WIDTH = 3
NEG_INF = float("-inf")
HALO = 8
ROW_CHUNK = 64
SLOPE_PARTS = 3

V7X_VMEM_LIMIT_BYTES = 56 * 1024 * 1024
V7X_MXU_COLUMNS = 256
BF16_SUBLANES = 16


def _params(semantics):
    return pltpu.CompilerParams(dimension_semantics=semantics, vmem_limit_bytes=V7X_VMEM_LIMIT_BYTES)


def _resident(shape, index_map):
    return pl.BlockSpec(shape, index_map, pipeline_mode=pl.Buffered(1))


def _layer_norm(z, g, b):
    mu = jnp.mean(z, axis=-1, keepdims=True)
    zc = z - mu
    var = jnp.mean(zc * zc, axis=-1, keepdims=True)
    return zc * lax.rsqrt(var + LN_EPS) * g + b


def _rms_norm(o, g):
    return o * lax.rsqrt(jnp.mean(o * o, axis=-1, keepdims=True) + RMS_EPS) * g


def _interleave(*stages):
    order = sorted(((k + 0.5) / len(parts), s, k) for s, parts in enumerate(stages) for k in range(len(parts)))
    for _, s, k in order:
        stages[s][k]()


def _dot(a, b):
    return jnp.dot(a, b, preferred_element_type=F32)


def _dot_nt(a, b):
    return lax.dot_general(a, b, (((1,), (1,)), ((), ())), preferred_element_type=F32)


def _proj_kernel(x_ref, w_ref, cs_ref, *refs):
    n_cast = (len(refs) - 2) // 2
    cast_in, o_ref, cast_out, xb_ref = refs[:n_cast], refs[n_cast], refs[n_cast + 1:-1], refs[-1]

    @pl.when(pl.program_id(1) == 0)
    def _():
        xb_ref[...] = x_ref[...].astype(BF16)

    o_ref[...] = (_dot(xb_ref[...], w_ref[...]) * cs_ref[...]).astype(o_ref.dtype)
    for src, dst in zip(cast_in, cast_out):
        dst[...] = src[...].astype(BF16)


def _project(x, w, layer, col_scale, tm, tn, cast=()):
    m, k = x.shape
    n = w.shape[2]
    tm = min(tm, m)
    grid = (m // tm, n // tn)
    steps = grid[0] * grid[1]
    flat = [c.reshape(-1, c.shape[-1]) for c in cast]
    block_rows = [next(br for br in range(BF16_SUBLANES, f.shape[0] + 1, BF16_SUBLANES)
                       if f.shape[0] % br == 0 and f.shape[0] // br <= steps) for f in flat]

    def cast_spec(f, br):
        last = f.shape[0] // br - 1
        return pl.BlockSpec((br, f.shape[1]), lambda i, j: (jnp.minimum(i * grid[1] + j, last), 0))

    cast_specs = [cast_spec(f, br) for f, br in zip(flat, block_rows)]
    outs = pl.pallas_call(
        _proj_kernel,
        grid=grid,
        in_specs=[pl.BlockSpec((tm, k), lambda i, j: (i, 0)),
                  pl.BlockSpec((None, k, tn), lambda i, j: (layer, 0, j)),
                  pl.BlockSpec((1, tn), lambda i, j: (0, j))] + cast_specs,
        out_specs=[pl.BlockSpec((tm, tn), lambda i, j: (i, j))] + cast_specs,
        out_shape=[jax.ShapeDtypeStruct((m, n), BF16)] + [jax.ShapeDtypeStruct(f.shape, BF16) for f in flat],
        scratch_shapes=[pltpu.VMEM((tm, k), BF16)],
        compiler_params=_params(("arbitrary", "arbitrary")),
    )(x, w, col_scale, *flat)
    return (outs[0], *(o.reshape(c.shape) for o, c in zip(outs[1:], cast)))


def _diff_parts(slope_ref, q_ref, k_ref, v_ref, lq1_ref, lk1_ref, lq2_ref, lk2_ref, g_ref, o_ref, vt_ref,
                *, t, lam_init):
    c = {}
    n_tiles = vt_ref.shape[0]
    assert t <= 2 * V7X_MXU_COLUMNS and V7X_MXU_COLUMNS == 1 << 8

    def setup():
        h = pl.program_id(1)
        c["slope"] = slope_ref[h, SLOPE_PARTS]
        for j in range(n_tiles):
            vt_ref[j, :BLOCK, :] = v_ref[0, j * t:(j + 1) * t, :].astype(F32).T.astype(BF16)
            vt_ref[j, BLOCK:, :] = jnp.ones((vt_ref.shape[1] - BLOCK, t), BF16)
        c["lam"] = (jnp.exp(jnp.sum(lq1_ref[...] * lk1_ref[...], axis=-1, keepdims=True))
                    - jnp.exp(jnp.sum(lq2_ref[...] * lk2_ref[...], axis=-1, keepdims=True)) + lam_init)
        c["lane"] = lax.broadcasted_iota(jnp.int32, (t, BLOCK), 1)
        key_index = lax.broadcasted_iota(jnp.int32, (t, BLOCK), 0)
        low, high = key_index & (V7X_MXU_COLUMNS - 1), key_index >> 8
        c["k_extra"] = jnp.where(c["lane"] < SLOPE_PARTS, low,
                                 jnp.where(c["lane"] < 2 * SLOPE_PARTS, high, 0)).astype(F32).astype(BF16)
        q_extra = jnp.zeros((2 * t, BLOCK), F32)
        lane2 = lax.broadcasted_iota(jnp.int32, (2 * t, BLOCK), 1)
        for i in range(SLOPE_PARTS):
            q_extra = jnp.where(lane2 == i, slope_ref[h, i], q_extra)
            q_extra = jnp.where(lane2 == SLOPE_PARTS + i, slope_ref[h, i] * float(V7X_MXU_COLUMNS), q_extra)
        c["q_extra"] = q_extra.astype(BF16)
        key = lax.broadcasted_iota(jnp.int32, (t, 2 * t), 0)
        query = lax.broadcasted_iota(jnp.int32, (t, 2 * t), 1)
        c["causal"] = key <= jnp.where(query >= t, query - t, query)

    def queries(qi):
        q = q_ref[0, qi * t:(qi + 1) * t, :]
        zero = jnp.zeros_like(q)
        qq = jnp.concatenate([jnp.where(c["lane"] < DIFF_HALF_DIM, q, zero),
                              jnp.where(c["lane"] >= DIFF_HALF_DIM, q, zero)], axis=0)
        return jnp.concatenate([qq, c["q_extra"]], axis=1)

    order = sorted(range(n_tiles), key=lambda qi: -qi)
    chains = [[] for _ in range(DIFF_CHAINS)]
    for qi in order:
        min(chains, key=len).extend((qi, j) for j in range(qi + 1))
    state = {}
    parts = [setup]
    for step in range(max(len(ch) for ch in chains)):
        live = [ch[step] for ch in chains if step < len(ch)]
        s, p = {}, {}

        def score(live=live, s=s):
            for qi, j in live:
                if j == 0:
                    state[qi] = dict(qq=queries(qi))
                k = jnp.concatenate([k_ref[0, j * t:(j + 1) * t, :], c["k_extra"]], axis=1)
                s[qi] = _dot_nt(k, state[qi]["qq"])
                if j == qi:
                    s[qi] = jnp.where(c["causal"], s[qi], NEG_INF)

        def softmax(live=live, s=s, p=p):
            for qi, j in live:
                st = state[qi]
                shift = c["slope"] * float((j - qi) * t)
                m_tile = jnp.max(s[qi], axis=0, keepdims=True) + shift
                st["m_old"], st["m"] = st.get("m"), (m_tile if j == 0 else jnp.maximum(st["m"], m_tile))
                p[qi] = jnp.exp2(s[qi] - (st["m"] - shift)).astype(BF16)

        def values(live=live, p=p):
            for qi, j in live:
                st = state[qi]
                acc_tile = _dot(vt_ref[j], p[qi])
                st["acc"] = acc_tile if j == 0 else jnp.exp2(st["m_old"] - st["m"]) * st["acc"] + acc_tile

        def finish(live=live):
            for qi, j in live:
                if j == qi:
                    acc = state.pop(qi)["acc"]
                    o = acc[:BLOCK] / acc[BLOCK:BLOCK + 1]
                    o = (o[:, :t] - c["lam"] * o[:, t:]).T
                    o_ref[0, qi * t:(qi + 1) * t, :] = (_rms_norm(o, g_ref[...])
                                                        * (1.0 - lam_init)).astype(o_ref.dtype)

        parts += [score, softmax, values, finish]
    return parts


def _dilated_parts(slope_ref, q_ref, k_ref, v_ref, g_ref, o_ref, qf, kf, vf, out_ref, lse_ref, *, seq, tc):
    c = {}

    def setup():
        c["slope"] = slope_ref[pl.program_id(1)]
        qf[...] = q_ref[0].astype(F32)
        kf[...] = k_ref[0].astype(F32)
        vf[...] = v_ref[0].astype(F32)
        qi = lax.broadcasted_iota(jnp.int32, (BLOCK, 2 * BLOCK), 0)
        kj = lax.broadcasted_iota(jnp.int32, (BLOCK, 2 * BLOCK), 1)
        c["rel"] = qi + BLOCK - kj

    def rows(start, size, dil):
        return pl.ds(start, size) if dil == 1 else pl.ds(start, size, stride=dil)

    parts = [setup]
    for branch, (window, dil) in enumerate(DIL_CONFIGS):
        nb = seq // dil // BLOCK
        step = dil * BLOCK

        def biases(branch=branch, dil=dil, n_back=window // dil):
            valid = (c["rel"] >= 0) & (c["rel"] <= n_back)
            c["bias", branch] = jnp.where(valid, (-c["slope"] * dil) * c["rel"].astype(F32), NEG_INF)

        parts.append(biases)
        groups = [([(cls, cls) for cls in range(dil)], True)]
        if nb > 1:
            groups.append(([(cls + n * step, cls + (n - 1) * step) for cls in range(dil) for n in range(1, nb)],
                           False))
        for starts, first in groups:
            nkeys = BLOCK if first else 2 * BLOCK
            q_rows = [rows(q_start, BLOCK, dil) for q_start, _ in starts]
            k_rows = [rows(k_start, nkeys, dil) for _, k_start in starts]
            v = {}

            def score(branch=branch, first=first, q_rows=q_rows, k_rows=k_rows, v=v):
                bias = c["bias", branch][:, BLOCK:] if first else c["bias", branch]
                v["s"] = [_dot_nt(qf[qr, :].astype(BF16), kf[kr, :].astype(BF16)) + bias
                          for qr, kr in zip(q_rows, k_rows)]

            def peak(v=v):
                v["m"] = [jnp.max(si, axis=-1, keepdims=True) for si in v["s"]]

            def weights(v=v):
                v["p"] = [jnp.exp2(si - mi).astype(BF16) for si, mi in zip(v["s"], v["m"])]

            def values(k_rows=k_rows, nkeys=nkeys, v=v):
                ones = jnp.ones((nkeys, BLOCK), BF16)
                v["r"] = [_dot(pi, jnp.concatenate([vf[kr, :].astype(BF16), ones], axis=1))
                          for pi, kr in zip(v["p"], k_rows)]

            def store(branch=branch, q_rows=q_rows, v=v):
                for qr, mi, ri in zip(q_rows, v["m"], v["r"]):
                    l = ri[:, BLOCK:]
                    out_ref[branch, qr, :] = ri[:, :BLOCK] / l
                    lse_ref[branch, qr, :] = mi + jnp.log2(l)

            parts += [score, peak, weights, values, store]

    def combine(t0):
        r = pl.ds(t0, tc)
        lses = [lse_ref[i, r, :] for i in range(len(DIL_CONFIGS))]
        top = functools.reduce(jnp.maximum, lses)
        ws = [jnp.exp2(lse - top) for lse in lses]
        num = sum(w * out_ref[i, r, :] for i, w in enumerate(ws))
        o_ref[0, r, :] = _rms_norm(num / sum(ws), g_ref[...]).astype(o_ref.dtype)

    return parts + [functools.partial(combine, t0) for t0 in range(0, seq, tc)]


def _mixer_attn_kernel(dslope_ref, sslope_ref, dq_ref, dk_ref, dv_ref, lq1_ref, lk1_ref, lq2_ref, lk2_ref,
                       gd_ref, sq_ref, sk_ref, sv_ref, gs_ref, od_ref, os_ref,
                       vt_ref, qf, kf, vf, out_ref, lse_ref, *, t, lam_init, seq, tc):
    _interleave(
        _diff_parts(dslope_ref, dq_ref, dk_ref, dv_ref, lq1_ref, lk1_ref, lq2_ref, lk2_ref, gd_ref, od_ref, vt_ref,
                    t=t, lam_init=lam_init),
        _dilated_parts(sslope_ref, sq_ref, sk_ref, sv_ref, gs_ref, os_ref, qf, kf, vf, out_ref, lse_ref,
                       seq=seq, tc=tc))


def _mixer_attention(proj, slopes_diff, slopes_dil, lq1, lk1, lq2, lk2, g_diff, g_dil, lam_init, heads, t):
    b, s, _ = proj.shape
    nbr = len(DIL_CONFIGS)
    vec = lambda n: pl.BlockSpec((1, n), lambda bi, hi: (0, 0))
    slab = lambda blk: pl.BlockSpec((1, s, BLOCK), lambda bi, hi: (bi, 0, blk + hi))
    smem = pl.BlockSpec(memory_space=pltpu.SMEM)
    out = jax.ShapeDtypeStruct((b, s, heads * BLOCK), BF16)
    return pl.pallas_call(
        functools.partial(_mixer_attn_kernel, t=t, lam_init=lam_init, seq=s, tc=BLOCK),
        grid=(b, heads),
        in_specs=[smem, smem, slab(0), slab(heads), slab(2 * heads),
                  vec(DIFF_HALF_DIM), vec(DIFF_HALF_DIM), vec(DIFF_HALF_DIM), vec(DIFF_HALF_DIM), vec(BLOCK),
                  slab(3 * heads), slab(4 * heads), slab(5 * heads), vec(BLOCK)],
        out_specs=[slab(0), slab(0)],
        out_shape=[out, out],
        scratch_shapes=[pltpu.VMEM((s // t, BLOCK + BF16_SUBLANES, t), BF16),
                        pltpu.VMEM((s, BLOCK), F32), pltpu.VMEM((s, BLOCK), F32), pltpu.VMEM((s, BLOCK), F32),
                        pltpu.VMEM((nbr, s, BLOCK), F32), pltpu.VMEM((nbr, s, BLOCK), F32)],
        compiler_params=_params(("arbitrary", "arbitrary")),
    )(slopes_diff, slopes_dil, proj, proj, proj, lq1, lk1, lq2, lk2, g_diff, proj, proj, proj, g_dil)


def _split_slopes(slopes):
    x = np.asarray(slopes, np.float64) * math.log2(math.e)
    x = x.astype(np.float32)
    parts, rest = [], x.copy()
    for _ in range(SLOPE_PARTS):
        part = rest.astype(BF16).astype(np.float32)
        parts.append(part)
        rest = rest - part
    return jnp.asarray(np.stack(parts + [x], axis=1), F32)


def _outproj_ln_kernel(*refs, n_in, alpha, tr):
    o_refs, w_refs = refs[:n_in], refs[n_in:2 * n_in]
    h_ref, g_ref, b_ref, out_ref = refs[2 * n_in:]
    tm, d = out_ref.shape

    def matmul_parts(r0):
        def part(c0):
            rows, cols = slice(r0, r0 + tr), slice(c0, c0 + V7X_MXU_COLUMNS)
            y = alpha * h_ref[rows, cols]
            for o_ref, w_ref in zip(o_refs, w_refs):
                y = y + _dot(o_ref[rows, :], w_ref[:, cols])
            out_ref[rows, cols] = y
        return [functools.partial(part, c0) for c0 in range(0, d, V7X_MXU_COLUMNS)]

    def norm_parts(r0):
        def part(r):
            rows = slice(r, r + ROW_CHUNK)
            out_ref[rows, :] = _layer_norm(out_ref[rows, :], g_ref[...], b_ref[...])
        return [functools.partial(part, r) for r in range(r0, r0 + tr, ROW_CHUNK)]

    pending = []
    for r0 in range(0, tm, tr):
        _interleave(matmul_parts(r0), pending)
        pending = norm_parts(r0)
    _interleave(pending)


def _outproj_ln(os, w, layer, h, g, b, alpha, tm, tr):
    m, d = h.shape
    n_in = len(os)
    kw = w.shape[1] // n_in
    return pl.pallas_call(
        functools.partial(_outproj_ln_kernel, n_in=n_in, alpha=alpha, tr=tr),
        grid=(m // tm,),
        in_specs=([pl.BlockSpec((tm, kw), lambda i: (i, 0)) for _ in os]
                  + [_resident((None, kw, d), functools.partial(lambda i, r: (layer, r, 0), r=r))
                     for r in range(n_in)]
                  + [pl.BlockSpec((tm, d), lambda i: (i, 0)),
                     pl.BlockSpec((1, d), lambda i: (0, 0)), pl.BlockSpec((1, d), lambda i: (0, 0))]),
        out_specs=pl.BlockSpec((tm, d), lambda i: (i, 0)),
        out_shape=jax.ShapeDtypeStruct((m, d), F32),
        compiler_params=_params(("arbitrary",)),
    )(*os, *([w] * n_in), h, g, b)


def _mem_attn_kernel(h_ref, wq_ref, kv_ref, o_ref, *, heads):
    d = h_ref.shape[1]
    hd = d // heads
    q = _dot(h_ref[...].astype(BF16), wq_ref[...]).astype(BF16)
    scale = hd ** -0.5
    for i in range(heads):
        k = kv_ref[0, :, i * hd:(i + 1) * hd]
        v = kv_ref[0, :, d + i * hd:d + (i + 1) * hd]
        s = _dot_nt(q[:, i * hd:(i + 1) * hd], k) * scale
        e = jnp.exp(s - jnp.max(s, axis=-1, keepdims=True))
        p = e * (1.0 / jnp.sum(e, axis=-1, keepdims=True))
        o_ref[:, i * hd:(i + 1) * hd] = _dot(p.astype(BF16), v).astype(o_ref.dtype)


def _mem_block_kernel(h_ref, wq_ref, kv_ref, wo_ref, g_ref, b_ref, out_ref, hb_ref, q_ref, o_ref,
                      *, heads, alpha, tr):
    tm, d = out_ref.shape
    hd = d // heads
    scale = hd ** -0.5
    hb_ref[...] = h_ref[...].astype(BF16)

    def block_parts(r0):
        rows = slice(r0, r0 + tr)

        def q_part(c0):
            cols = slice(c0, c0 + V7X_MXU_COLUMNS)
            q_ref[rows, cols] = _dot(hb_ref[rows, :], wq_ref[:, cols]).astype(BF16)

        def head(i):
            k = kv_ref[0, :, i * hd:(i + 1) * hd]
            v = kv_ref[0, :, d + i * hd:d + (i + 1) * hd]
            s = _dot_nt(q_ref[rows, i * hd:(i + 1) * hd], k) * scale
            e = jnp.exp(s - jnp.max(s, axis=-1, keepdims=True))
            p = e * (1.0 / jnp.sum(e, axis=-1, keepdims=True))
            o_ref[rows, i * hd:(i + 1) * hd] = _dot(p.astype(BF16), v).astype(BF16)

        def o_part(c0):
            cols = slice(c0, c0 + V7X_MXU_COLUMNS)
            out_ref[rows, cols] = alpha * h_ref[rows, cols] + _dot(o_ref[rows, :], wo_ref[:, cols])

        column_tiles = range(0, d, V7X_MXU_COLUMNS)
        return ([functools.partial(q_part, c0) for c0 in column_tiles]
                + [functools.partial(head, i) for i in range(heads)]
                + [functools.partial(o_part, c0) for c0 in column_tiles])

    def norm_parts(r0):
        def part(r):
            rows = slice(r, r + ROW_CHUNK)
            out_ref[rows, :] = _layer_norm(out_ref[rows, :], g_ref[...], b_ref[...])
        return [functools.partial(part, r) for r in range(r0, r0 + tr, ROW_CHUNK)]

    pending = []
    for r0 in range(0, tm, tr):
        _interleave(block_parts(r0), pending)
        pending = norm_parts(r0)
    _interleave(pending)


def _mem_block(h, wq, wo, layer, kv, g, b, alpha, seq, tm, tr):
    m, d = h.shape
    mem_len = kv.shape[1]
    per_seq = seq // tm
    return pl.pallas_call(
        functools.partial(_mem_block_kernel, heads=MEM_HEADS, alpha=alpha, tr=tr),
        grid=(m // tm,),
        in_specs=[pl.BlockSpec((tm, d), lambda i: (i, 0)),
                  _resident((None, d, d), lambda i: (layer, 0, 0)),
                  pl.BlockSpec((1, mem_len, 2 * d), lambda i: (i // per_seq, 0, 0)),
                  _resident((None, d, d), lambda i: (layer, 0, 0)),
                  pl.BlockSpec((1, d), lambda i: (0, 0)), pl.BlockSpec((1, d), lambda i: (0, 0))],
        out_specs=pl.BlockSpec((tm, d), lambda i: (i, 0)),
        out_shape=jax.ShapeDtypeStruct((m, d), F32),
        scratch_shapes=[pltpu.VMEM((tm, d), BF16), pltpu.VMEM((tm, d), BF16), pltpu.VMEM((tm, d), BF16)],
        compiler_params=_params(("arbitrary",)),
    )(h, wq, kv, wo, g, b)


def _mem_attention(h, wq, layer, kv, seq, tm):
    m, d = h.shape
    mem_len = kv.shape[1]
    per_seq = seq // tm
    return pl.pallas_call(
        functools.partial(_mem_attn_kernel, heads=MEM_HEADS),
        grid=(m // tm,),
        in_specs=[pl.BlockSpec((tm, d), lambda i: (i, 0)),
                  _resident((None, d, d), lambda i: (layer, 0, 0)),
                  pl.BlockSpec((1, mem_len, 2 * d), lambda i: (i // per_seq, 0, 0))],
        out_specs=pl.BlockSpec((tm, d), lambda i: (i, 0)),
        out_shape=jax.ShapeDtypeStruct((m, d), BF16),
        compiler_params=_params(("arbitrary",)),
    )(h, wq, kv)


def _ffn_kernel(x_ref, wg_ref, wu_ref, cwg_ref, cwu_ref, cbg_ref, cbu_ref, wd_ref, g_ref, b_ref, o_ref,
                xb_ref, hg0_ref, hu0_ref, hg1_ref, hu1_ref, act0_ref, act1_ref, carry_g_ref, carry_u_ref,
                *, tm, tiles_per_seq, alpha):
    i = pl.program_id(0)
    f = pl.program_id(1)
    nf = pl.num_programs(1) - 2
    seq_start = (i % tiles_per_seq) == 0
    h_refs = ((hg0_ref, hu0_ref), (hg1_ref, hu1_ref))
    act_refs = (act0_ref, act1_ref)
    acc_ref = o_ref

    tf = wg_ref.shape[1]
    d = wd_ref.shape[1]

    def up_project(slot):
        def part(which, c0):
            w_ref, hs_ref, carry_ref = ((wg_ref, wu_ref)[which], h_refs[slot][which],
                                        (carry_g_ref, carry_u_ref)[which])
            cols = slice(c0, c0 + V7X_MXU_COLUMNS)
            hs_ref[HALO:, cols] = _dot(xb_ref[...], w_ref[:, cols])
            hs_ref[:HALO, cols] = jnp.where(seq_start, 0.0, carry_ref[f, :, cols])
            carry_ref[f, :, cols] = hs_ref[tm:, cols]
        return [functools.partial(part, which, c0) for which in range(2)
                for c0 in range(0, tf, V7X_MXU_COLUMNS)]

    def conv(hs_ref, cw_ref, cb_ref, row0, cols):
        out = cb_ref[:, cols]
        for tap in range(CONV_WIDTH):
            off = row0 + HALO - (CONV_WIDTH - 1) + tap
            out = out + hs_ref[off:off + ROW_CHUNK, cols] * cw_ref[tap:tap + 1, cols]
        return out

    def activate(slot):
        def part(c0, row0):
            cols = slice(c0, c0 + BLOCK)
            gate = conv(h_refs[slot][0], cwg_ref, cbg_ref, row0, cols)
            up = conv(h_refs[slot][1], cwu_ref, cbu_ref, row0, cols)
            act = gate * (1.0 / (1.0 + jnp.exp2(gate * -math.log2(math.e)))) * up
            act_refs[slot][row0:row0 + ROW_CHUNK, cols] = act.astype(BF16)
        return [functools.partial(part, c0, row0) for c0 in range(0, tf, BLOCK)
                for row0 in range(0, tm, ROW_CHUNK)]

    def down_project(slot):
        def part(c0):
            cols = slice(c0, c0 + V7X_MXU_COLUMNS)
            acc_ref[:, cols] += _dot(act_refs[slot][...], wd_ref[:, cols])
        return [functools.partial(part, c0) for c0 in range(0, d, V7X_MXU_COLUMNS)]

    run = _interleave

    @pl.when(f == 0)
    def _():
        @pl.when(i == 0)
        def _():
            carry_g_ref[...] = jnp.zeros(carry_g_ref.shape, F32)
            carry_u_ref[...] = jnp.zeros(carry_u_ref.shape, F32)

        xb_ref[...] = x_ref[...].astype(BF16)
        acc_ref[...] = jnp.zeros(acc_ref.shape, F32)
        run(up_project(0))

    @pl.when(f == 1)
    def _():
        run(up_project(1), activate(0))

    for parity in range(2):
        @pl.when((f > 1) & (f < nf) & (f % 2 == parity))
        def _(parity=parity):
            run(up_project(parity), activate(1 - parity), down_project(parity))

    @pl.when(f == nf)
    def _():
        run(down_project((nf - 2) % 2), activate((nf - 1) % 2))

    @pl.when(f == nf + 1)
    def _():
        run(down_project((nf - 1) % 2))
        o_ref[...] = _layer_norm(alpha * x_ref[...] + acc_ref[...], g_ref[...], b_ref[...])


def _conv_ffn_ln(x, w_up, w_down, layer, conv_w, conv_b, g, b, alpha, seq, tm, tf):
    m, d = x.shape
    d_ff = w_down.shape[1]
    nf = d_ff // tf
    row = lambda i, f: (i, 0)
    fixed = lambda i, f: (0, 0)
    chunk = lambda f, lag: jnp.clip(f - lag, 0, nf - 1)
    h_buf = pltpu.VMEM((tm + HALO, tf), F32)
    act_buf = pltpu.VMEM((tm, tf), BF16)
    return pl.pallas_call(
        functools.partial(_ffn_kernel, tm=tm, tiles_per_seq=seq // tm, alpha=alpha),
        grid=(m // tm, nf + 2),
        in_specs=[pl.BlockSpec((tm, d), row),
                  pl.BlockSpec((None, d, tf), lambda i, f: (layer, 0, chunk(f, 0))),
                  pl.BlockSpec((None, d, tf), lambda i, f: (layer, 0, nf + chunk(f, 0))),
                  pl.BlockSpec((CONV_WIDTH, tf), lambda i, f: (0, chunk(f, 1))),
                  pl.BlockSpec((CONV_WIDTH, tf), lambda i, f: (0, nf + chunk(f, 1))),
                  pl.BlockSpec((1, tf), lambda i, f: (0, chunk(f, 1))),
                  pl.BlockSpec((1, tf), lambda i, f: (0, nf + chunk(f, 1))),
                  pl.BlockSpec((None, tf, d), lambda i, f: (layer, chunk(f, 2), 0)),
                  pl.BlockSpec((1, d), fixed), pl.BlockSpec((1, d), fixed)],
        out_specs=pl.BlockSpec((tm, d), row),
        out_shape=jax.ShapeDtypeStruct((m, d), F32),
        scratch_shapes=[pltpu.VMEM((tm, d), BF16), h_buf, h_buf, h_buf, h_buf, act_buf, act_buf,
                        pltpu.VMEM((nf, HALO, tf), F32), pltpu.VMEM((nf, HALO, tf), F32)],
        compiler_params=_params(("arbitrary", "arbitrary")),
    )(x, w_up, w_up, conv_w, conv_w, conv_b, conv_b, w_down, g, b)


def _alibi_slopes(n):
    return (2.0 ** (-8.0 * np.arange(1, n + 1) / n)).astype(np.float32)


def kernel(x, mem, w_in, w_mix_out, lambda_q1, lambda_k1, lambda_q2, lambda_k2, g_diff, g_dil, ln1_g, ln1_b,
           w_mem_q, w_mem_kv, w_mem_o, ln2_g, ln2_b, w_up, conv_w, conv_b, w_down, ln3_g, ln3_b):
    batch, seq, d = x.shape
    depth = w_in.shape[0]
    mem_len = mem.shape[1]
    diff_heads = dil_heads = d // 256
    alpha = (2 * depth) ** 0.25
    slopes = _alibi_slopes(diff_heads + dil_heads)
    log2e = math.log2(math.e)
    slopes_diff = _split_slopes(slopes[0::2])
    slopes_dil = jnp.asarray(slopes[1::2].astype(np.float64) * log2e, F32)
    row = lambda a: a.reshape(1, -1)

    group = diff_heads * BLOCK
    col_scale = np.ones((1, 6 * group), np.float32)
    col_scale[:, :group] = DIFF_HALF_DIM ** -0.5 * log2e
    col_scale[:, 3 * group:4 * group] = BLOCK ** -0.5 * log2e
    col_scale = jnp.asarray(col_scale)
    kv_scale = jnp.ones((1, 2 * d), F32)

    w_in = w_in.astype(BF16)
    later_weights = (w_mix_out, w_mem_q, w_mem_kv, w_mem_o, w_up, w_down)

    h = x.reshape(batch * seq, d)
    mem2 = mem.reshape(batch * mem_len, d)
    for l in range(depth):
        lam_init = 0.8 - 0.6 * math.exp(-0.3 * l)
        if l == 0:
            proj, w_mix_out, w_mem_q, w_mem_kv, w_mem_o, w_up, w_down = _project(
                h, w_in, l, col_scale, tm=1024, tn=1024, cast=later_weights)
        else:
            proj, = _project(h, w_in, l, col_scale, tm=1024, tn=1024)
        proj = proj.reshape(batch, seq, -1)
        o_diff, o_dil = _mixer_attention(proj, slopes_diff, slopes_dil, row(lambda_q1[l]), row(lambda_k1[l]),
                                         row(lambda_q2[l]), row(lambda_k2[l]), row(g_diff[l]), row(g_dil[l]),
                                         lam_init, diff_heads, t=512)
        h = _outproj_ln([o_diff.reshape(batch * seq, -1), o_dil.reshape(batch * seq, -1)],
                        w_mix_out, l, h, row(ln1_g[l]), row(ln1_b[l]), alpha, tm=1024, tr=256)

        kv, = _project(mem2, w_mem_kv, l, kv_scale, tm=1024, tn=1024)
        kv = kv.reshape(batch, mem_len, 2 * d)
        h = _mem_block(h, w_mem_q, w_mem_o, l, kv, row(ln2_g[l]), row(ln2_b[l]), alpha, seq, tm=512, tr=256)

        h = _conv_ffn_ln(h, w_up, w_down, l, conv_w[l], row(conv_b[l]), row(ln3_g[l]), row(ln3_b[l]),
                         alpha, seq, tm=512, tf=512)
    return h.reshape(batch, seq, d)
```
